```python
import math
import jax, jax.numpy as jnp
from jax import lax
import numpy as np

D_MODEL = 1024
BATCH = 1
SEQ = 16384
DEPTH = 1
DEC_BATCH = 128
DEC_SEQ = 1
PAST_LEN = 8192
PAGE_SIZE = 128

HEAD_DIM = 64
HEADS_PER_GROUP = 8
ATTN_GROUPS = ((128, 1), (512, 4), (2048, 16))
N_ATTN_GROUPS = len(ATTN_GROUPS)
ATTN_QKV_W = N_ATTN_GROUPS * HEADS_PER_GROUP * HEAD_DIM
ATTN_OUT_W = HEADS_PER_GROUP * HEAD_DIM
ROT_DIM = HEAD_DIM // 4
ROPE_THETA = 500000.0
Q_BLOCK = 128
SSM_WIDTH = D_MODEL // 2
SSM_GROUP = 16
SSM_GROUPS = SSM_WIDTH // SSM_GROUP
SSM_STATE = 64
DT_MIN = 1e-3
DT_MAX = 1e-1
D_FF = 4 * D_MODEL
EPS = 1e-6
IN_W = 3 * ATTN_QKV_W + SSM_WIDTH + 2 * D_MODEL
SPLITS = (ATTN_QKV_W, 2 * ATTN_QKV_W, 3 * ATTN_QKV_W, 3 * ATTN_QKV_W + SSM_WIDTH,
          3 * ATTN_QKV_W + SSM_WIDTH + D_MODEL)

kernel_name = "hybrid_s5_dilated_attn_decode_step"


def rmsnorm(x, g):
    x32 = x.astype(jnp.float32)
    y = x32 * lax.rsqrt(jnp.mean(x32 * x32, axis=-1, keepdims=True) + EPS)
    return (y * g.astype(jnp.float32)).astype(x.dtype)


def rotary(x, pos):
    half = ROT_DIM // 2
    inv_freq = jnp.exp(-math.log(ROPE_THETA) * jnp.arange(half, dtype=jnp.float32) * (2.0 / ROT_DIM))
    ang = pos.astype(jnp.float32)[:, None] * inv_freq[None, :]
    bshape = (1, pos.shape[0]) + (1,) * (x.ndim - 3) + (half,)
    cos = jnp.cos(ang).reshape(bshape)
    sin = jnp.sin(ang).reshape(bshape)
    x1 = x[..., :half].astype(jnp.float32)
    x2 = x[..., half:ROT_DIM].astype(jnp.float32)
    rot = jnp.concatenate([x1 * cos - x2 * sin, x2 * cos + x1 * sin], axis=-1).astype(x.dtype)
    return jnp.concatenate([rot, x[..., ROT_DIM:]], axis=-1)


def dilated_attn_prompt(q, k, v, window, dil):
    b, s, h, e = q.shape
    n = s // dil
    wk = window // dil
    bq = math.gcd(n, Q_BLOCK)
    nb = n // bq

    def sub(t):
        return t.reshape(b, n, dil, h, e).transpose(0, 2, 1, 3, 4)

    qs = sub(q).reshape(b, dil, nb, bq, h, e)
    pad = ((0, 0), (0, 0), (wk, 0), (0, 0), (0, 0))
    kp = jnp.pad(sub(k), pad)
    vp = jnp.pad(sub(v), pad)
    idx = jnp.arange(nb)[:, None] * bq + jnp.arange(bq + wk)[None, :]
    kw = kp[:, :, idx]
    vw = vp[:, :, idx]
    scores = jnp.einsum('brnqhe,brnkhe->brnhqk', qs, kw,
                        preferred_element_type=jnp.float32) * (HEAD_DIM ** -0.5)
    qi = jnp.arange(bq)[:, None]
    kj = jnp.arange(bq + wk)[None, :]
    dist = qi + wk - kj
    kidx = (jnp.arange(nb) * bq)[:, None, None] + kj[None] - wk
    mask = ((dist >= 0) & (dist <= wk))[None] & (kidx >= 0)
    scores = jnp.where(mask[None, None, :, None], scores, -jnp.inf)
    m = jnp.max(scores, axis=-1, keepdims=True)
    p = jnp.exp(scores - m)
    l = jnp.sum(p, axis=-1, keepdims=True)
    o = jnp.einsum('brnhqk,brnkhe->brnhqe', p / l, vw.astype(jnp.float32))
    lse = (m + jnp.log(l))[..., 0]
    o = o.transpose(0, 1, 2, 4, 3, 5).reshape(b, dil, n, h, e).transpose(0, 2, 1, 3, 4).reshape(b, s, h, e)
    lse = lse.transpose(0, 1, 2, 4, 3).reshape(b, dil, n, h).transpose(0, 2, 1, 3).reshape(b, s, h)
    return o, lse


def dilated_attn_sample(q, k, v, kv_cache, window, dil):
    bd, t, h, e = q.shape
    lc = kv_cache.shape[1]
    kv_all = jnp.concatenate([kv_cache, jnp.stack([k, v], axis=2).astype(kv_cache.dtype)], axis=1)
    nk = window // dil + 1
    idx = lc + jnp.arange(t)[:, None] - dil * jnp.arange(nk)[None, :]
    valid = idx >= 0
    g = kv_all[:, jnp.maximum(idx, 0)]
    scores = jnp.einsum('bthe,btkhe->bthk', q, g[:, :, :, 0],
                        preferred_element_type=jnp.float32) * (HEAD_DIM ** -0.5)
    scores = jnp.where(valid[None, :, None, :], scores, -jnp.inf)
    m = jnp.max(scores, axis=-1, keepdims=True)
    p = jnp.exp(scores - m)
    l = jnp.sum(p, axis=-1, keepdims=True)
    o = jnp.einsum('bthk,btkhe->bthe', p / l, g[:, :, :, 1].astype(jnp.float32))
    lse = (m + jnp.log(l))[..., 0]
    return o, lse, kv_all[:, t:]


def s5_branch(u, h0, lam_re, lam_im, log_dt, b_re, b_im, c_re, c_im, d_skip, w_glu, b_glu):
    f32 = jnp.float32
    b, s = u.shape[:2]
    u32 = u.astype(f32)
    lam = lax.complex(lam_re.astype(f32), lam_im.astype(f32))
    dt = jnp.exp(log_dt.astype(f32))[:, None]
    lam_bar = jnp.exp(lam * dt)
    b_bar = ((lam_bar - 1.0) / lam)[..., None] * lax.complex(b_re.astype(f32), b_im.astype(f32))
    ug = u32.reshape(b, s, SSM_GROUPS, SSM_GROUP)
    bu = jnp.einsum('bsgp,gnp->bsgn', ug, b_bar)
    if h0 is not None:
        bu = bu.at[:, 0].add(lam_bar * h0)
    a = jnp.broadcast_to(lam_bar, bu.shape)

    def combine(e1, e2):
        a1, x1 = e1
        a2, x2 = e2
        return a1 * a2, a2 * x1 + x2

    _, hs = lax.associative_scan(combine, (a, bu), axis=1)
    c = lax.complex(c_re.astype(f32), c_im.astype(f32))
    y = jnp.einsum('bsgn,gpn->bsgp', hs, c).real.reshape(b, s, SSM_WIDTH) + d_skip.astype(f32) * u32
    y = jax.nn.gelu(y)
    y = y * jax.nn.sigmoid(y @ w_glu.astype(f32) + b_glu.astype(f32))
    return y.astype(u.dtype), hs[:, -1]


def decoder_layer(x, pos, kv_caches, h0, lw):
    (norm_mix, w_in, lam_re, lam_im, log_dt, b_re, b_im, c_re, c_im, d_skip, w_glu, b_glu,
     w_branch_ssm, w_branch_attn, w_out, norm_mlp, w_up, w_down) = lw
    b, s = x.shape[:2]
    h = rmsnorm(x, norm_mix)
    q, k, v, u, g_ssm, g_attn = jnp.split(h @ w_in, SPLITS, axis=-1)
    shp = (b, s, N_ATTN_GROUPS, HEADS_PER_GROUP, HEAD_DIM)
    q = rotary(q.reshape(shp), pos)
    k = rotary(k.reshape(shp), pos)
    v = v.reshape(shp)
    outs, lses, new_kv = [], [], []
    for gi, (window, dil) in enumerate(ATTN_GROUPS):
        if kv_caches is None:
            o, lse = dilated_attn_prompt(q[:, :, gi], k[:, :, gi], v[:, :, gi], window, dil)
            nkv = jnp.stack([k[:, :, gi], v[:, :, gi]], axis=2)[:, s - min(window, s):]
        else:
            o, lse, nkv = dilated_attn_sample(q[:, :, gi], k[:, :, gi], v[:, :, gi], kv_caches[gi], window, dil)
        outs.append(o)
        lses.append(lse)
        new_kv.append(nkv)
    alpha = jax.nn.softmax(jnp.stack(lses), axis=0)
    attn = jnp.einsum('gbsh,gbshe->bshe', alpha, jnp.stack(outs)).reshape(b, s, ATTN_OUT_W).astype(x.dtype)
    ssm_y, h_last = s5_branch(u, h0, lam_re, lam_im, log_dt, b_re, b_im, c_re, c_im, d_skip, w_glu, b_glu)
    z = jax.nn.sigmoid(g_ssm) * (ssm_y @ w_branch_ssm) + jax.nn.sigmoid(g_attn) * (attn @ w_branch_attn)
    x = x + z @ w_out
    hm = rmsnorm(x, norm_mlp)
    x = x + jnp.square(jax.nn.relu(hm @ w_up)) @ w_down
    return x, new_kv, h_last


def setup_inputs(seed: int = 0) -> dict:
    key = jax.random.key(seed)
    ks = jax.random.split(key, 32)
    f32 = jnp.float32

    def nrm(k, shape, scale):
        return jax.random.normal(k, shape, f32) * scale

    G, N = SSM_GROUPS, SSM_STATE
    lam_re = -0.5 + nrm(ks[6], (DEPTH, G, N), 0.01)
    lam_im = math.pi * jnp.arange(N, dtype=f32)[None, None, :] + nrm(ks[7], (DEPTH, G, N), 0.01)
    log_dt = jax.random.uniform(ks[8], (DEPTH, G), f32, math.log(DT_MIN), math.log(DT_MAX))
    return {
        "x_prompt": nrm(ks[0], (BATCH, SEQ, D_MODEL), 1.0),
        "x_sample": nrm(ks[1], (DEC_BATCH, DEC_SEQ, D_MODEL), 1.0),
        "cache_kv_g1": nrm(ks[2], (DEPTH, DEC_BATCH, min(ATTN_GROUPS[0][0], PAST_LEN), 2, HEADS_PER_GROUP, HEAD_DIM), 1.0),
        "cache_kv_g2": nrm(ks[3], (DEPTH, DEC_BATCH, min(ATTN_GROUPS[1][0], PAST_LEN), 2, HEADS_PER_GROUP, HEAD_DIM), 1.0),
        "cache_kv_g3": nrm(ks[4], (DEPTH, DEC_BATCH, min(ATTN_GROUPS[2][0], PAST_LEN), 2, HEADS_PER_GROUP, HEAD_DIM), 1.0),
        "state_ssm": nrm(ks[5], (DEPTH, DEC_BATCH, G, N, 2), 0.3),
        "norm_mix": 1.0 + nrm(ks[9], (DEPTH, D_MODEL), 0.02),
        "w_in": nrm(ks[10], (DEPTH, D_MODEL, IN_W), D_MODEL ** -0.5),
        "ssm_lambda_re": lam_re,
        "ssm_lambda_im": lam_im,
        "ssm_log_dt": log_dt,
        "ssm_b_re": nrm(ks[11], (DEPTH, G, N, SSM_GROUP), SSM_GROUP ** -0.5),
        "ssm_b_im": nrm(ks[12], (DEPTH, G, N, SSM_GROUP), SSM_GROUP ** -0.5),
        "ssm_c_re": nrm(ks[13], (DEPTH, G, SSM_GROUP, N), N ** -0.5),
        "ssm_c_im": nrm(ks[14], (DEPTH, G, SSM_GROUP, N), N ** -0.5),
        "ssm_d": nrm(ks[15], (DEPTH, SSM_WIDTH), 1.0),
        "w_glu": nrm(ks[16], (DEPTH, SSM_WIDTH, SSM_WIDTH), SSM_WIDTH ** -0.5),
        "b_glu": nrm(ks[17], (DEPTH, SSM_WIDTH), 0.01),
        "w_branch_ssm": nrm(ks[18], (DEPTH, SSM_WIDTH, D_MODEL), SSM_WIDTH ** -0.5),
        "w_branch_attn": nrm(ks[19], (DEPTH, ATTN_OUT_W, D_MODEL), ATTN_OUT_W ** -0.5),
        "w_out": nrm(ks[20], (DEPTH, D_MODEL, D_MODEL), D_MODEL ** -0.5),
        "norm_mlp": 1.0 + nrm(ks[21], (DEPTH, D_MODEL), 0.02),
        "w_up": nrm(ks[22], (DEPTH, D_MODEL, D_FF), D_MODEL ** -0.5),
        "w_down": nrm(ks[23], (DEPTH, D_FF, D_MODEL), D_FF ** -0.5),
        "norm_final": 1.0 + nrm(ks[24], (D_MODEL,), 0.02),
    }


def reference(x_prompt, x_sample, cache_kv_g1, cache_kv_g2, cache_kv_g3, state_ssm,
              norm_mix, w_in, ssm_lambda_re, ssm_lambda_im, ssm_log_dt, ssm_b_re, ssm_b_im,
              ssm_c_re, ssm_c_im, ssm_d, w_glu, b_glu, w_branch_ssm, w_branch_attn, w_out,
              norm_mlp, w_up, w_down, norm_final):
    pos_p = jnp.arange(x_prompt.shape[1])
    pos_s = PAST_LEN + jnp.arange(x_sample.shape[1])
    yp, ys = x_prompt, x_sample
    kvp = ([], [], [])
    kvs = ([], [], [])
    hp_list, hs_list = [], []
    for l in range(DEPTH):
        lw = (norm_mix[l], w_in[l], ssm_lambda_re[l], ssm_lambda_im[l], ssm_log_dt[l], ssm_b_re[l],
              ssm_b_im[l], ssm_c_re[l], ssm_c_im[l], ssm_d[l], w_glu[l], b_glu[l], w_branch_ssm[l],
              w_branch_attn[l], w_out[l], norm_mlp[l], w_up[l], w_down[l])
        yp, nkv_p, h_p = decoder_layer(yp, pos_p, None, None, lw)
        st = state_ssm[l]
        h0 = lax.complex(st[..., 0].astype(jnp.float32), st[..., 1].astype(jnp.float32))
        ys, nkv_s, h_s = decoder_layer(ys, pos_s, (cache_kv_g1[l], cache_kv_g2[l], cache_kv_g3[l]), h0, lw)
        for gi in range(N_ATTN_GROUPS):
            kvp[gi].append(nkv_p[gi])
            kvs[gi].append(nkv_s[gi])
        hp_list.append(jnp.stack([h_p.real, h_p.imag], axis=-1).astype(state_ssm.dtype))
        hs_list.append(jnp.stack([h_s.real, h_s.imag], axis=-1).astype(state_ssm.dtype))
    y_prompt = rmsnorm(yp, norm_final)
    y_sample = rmsnorm(ys, norm_final)
    kv_g1_prompt = jnp.stack(kvp[0], axis=0)
    kv_g2_prompt = jnp.stack(kvp[1], axis=0)
    kv_g3_prompt = jnp.stack(kvp[2], axis=0)
    ssm_prompt = jnp.stack(hp_list, axis=0)
    kv_g1_sample = jnp.stack(kvs[0], axis=0)
    kv_g2_sample = jnp.stack(kvs[1], axis=0)
    kv_g3_sample = jnp.stack(kvs[2], axis=0)
    ssm_sample = jnp.stack(hs_list, axis=0)
    return (y_prompt, y_sample, kv_g1_prompt, kv_g2_prompt, kv_g3_prompt, ssm_prompt,
            kv_g1_sample, kv_g2_sample, kv_g3_sample, ssm_sample)
```

```python
import functools
import math

import jax
import jax.numpy as jnp
from jax import lax
from jax.experimental import pallas as pl
from jax.experimental.pallas import tpu as pltpu

F32 = jnp.float32
BF16 = jnp.bfloat16

D_MODEL = 1024
HEAD_DIM = 64
HEADS = 8
GROUP_W = HEADS * HEAD_DIM
ATTN_GROUPS = ((128, 1), (512, 4), (2048, 16))
N_GROUPS = len(ATTN_GROUPS)
QKV_W = N_GROUPS * GROUP_W
BAND = 128
ROT_DIM = HEAD_DIM // 4
ROT_HALF = ROT_DIM // 2
ROPE_THETA = 500000.0
PAST_LEN = 8192
SSM_WIDTH = 512
SSM_P = 16
SSM_G = SSM_WIDTH // SSM_P
SSM_N = 64
SSM_T = 16
EPS = 1e-6
NEG = -1e30
LANES = 128
VMEM_LIMIT = 56 * 1024 * 1024

OFF_Q, OFF_K, OFF_V = 0, QKV_W, 2 * QKV_W
OFF_U = 3 * QKV_W
OFF_GS = OFF_U + SSM_WIDTH
OFF_GA = OFF_GS + D_MODEL
IN_W = OFF_GA + D_MODEL


def _const_spec(shape):
    nd = len(shape)
    return pl.BlockSpec(shape, lambda *_: (0,) * nd, pipeline_mode=pl.Buffered(1))


def _params(n_axes=1):
    return pltpu.CompilerParams(dimension_semantics=("arbitrary",) * n_axes,
                                vmem_limit_bytes=VMEM_LIMIT)


def _sigmoid(x):
    return 1.0 / (1.0 + jnp.exp(-x))


def _rms(x, g):
    return x * lax.rsqrt(jnp.mean(x * x, axis=-1, keepdims=True) + EPS) * g


def _dot(a, b):
    return jnp.dot(a, b, preferred_element_type=F32)


def _inproj_kernel(x_ref, g_ref, w_ref, cos_ref, s1_ref, s2_ref, *outs, tm, tail_rows, first_tail):
    q_refs, k_refs, v_refs, tail_refs = outs[0:3], outs[3:6], outs[6:9], outs[9:12]
    u_ref, gs_ref, ga_ref = outs[12:15]
    i = pl.program_id(0)
    h = _rms(x_ref[...], g_ref[...]).astype(BF16)
    cos, s1, s2 = cos_ref[...], s1_ref[...], s2_ref[...]

    def rot(a):
        parts = []
        for j in range(GROUP_W // LANES):
            seg = a[:, j * LANES:(j + 1) * LANES]
            parts.append(seg * cos + pltpu.roll(seg, LANES - ROT_HALF, 1) * s1
                         + pltpu.roll(seg, ROT_HALF, 1) * s2)
        return jnp.concatenate(parts, axis=1)

    for gi in range(N_GROUPS):
        c0 = gi * GROUP_W
        qa = rot(_dot(h, w_ref[:, OFF_Q + c0:OFF_Q + c0 + GROUP_W]))
        ka = rot(_dot(h, w_ref[:, OFF_K + c0:OFF_K + c0 + GROUP_W]))
        va = _dot(h, w_ref[:, OFF_V + c0:OFF_V + c0 + GROUP_W])
        q_refs[gi][...] = (qa * (HEAD_DIM ** -0.5)).astype(BF16)
        k_refs[gi][...] = ka.astype(BF16)
        v_refs[gi][...] = va.astype(BF16)
        tb = tail_rows[gi]

        @pl.when(i >= first_tail[gi])
        def _(ka=ka, va=va, gi=gi, tb=tb):
            tail_refs[gi][:, 0:GROUP_W] = ka[tm - tb:, :]
            tail_refs[gi][:, GROUP_W:2 * GROUP_W] = va[tm - tb:, :]

    u_ref[...] = _dot(h, w_ref[:, OFF_U:OFF_GS])
    gs_ref[...] = _dot(h, w_ref[:, OFF_GS:OFF_GA])
    ga_ref[...] = _dot(h, w_ref[:, OFF_GA:IN_W])


def _inproj(x, gain, w_bf, cos, s1, s2, tm, windows):
    s = x.shape[0]
    ntiles = s // tm
    tail_rows = tuple(min(w, tm) for w in windows)
    first_tail = tuple(ntiles - w // tb for w, tb in zip(windows, tail_rows))
    row = lambda i: (i, 0)
    out_shape = ([jax.ShapeDtypeStruct((s, GROUP_W), BF16)] * 9
                 + [jax.ShapeDtypeStruct((w, 2 * GROUP_W), F32) for w in windows]
                 + [jax.ShapeDtypeStruct((s, SSM_WIDTH), F32),
                    jax.ShapeDtypeStruct((s, D_MODEL), F32),
                    jax.ShapeDtypeStruct((s, D_MODEL), F32)])
    out_specs = ([pl.BlockSpec((tm, GROUP_W), row)] * 9
                 + [pl.BlockSpec((tb, 2 * GROUP_W), functools.partial(
                     lambda i, ft: (jnp.maximum(i - ft, 0), 0), ft=ft))
                    for tb, ft in zip(tail_rows, first_tail)]
                 + [pl.BlockSpec((tm, SSM_WIDTH), row),
                    pl.BlockSpec((tm, D_MODEL), row),
                    pl.BlockSpec((tm, D_MODEL), row)])
    kern = functools.partial(_inproj_kernel, tm=tm, tail_rows=tail_rows, first_tail=first_tail)
    return pl.pallas_call(
        kern,
        grid=(ntiles,),
        in_specs=[pl.BlockSpec((tm, D_MODEL), row), _const_spec((1, D_MODEL)),
                  _const_spec((D_MODEL, IN_W)),
                  pl.BlockSpec((tm, LANES), row), pl.BlockSpec((tm, LANES), row),
                  pl.BlockSpec((tm, LANES), row)],
        out_specs=out_specs, out_shape=out_shape,
        compiler_params=_params(), name="inproj",
    )(x, gain, w_bf, cos, s1, s2)


def _rotary_tables(pos):
    inv_freq = jnp.exp(-math.log(ROPE_THETA) * jnp.arange(ROT_HALF, dtype=F32) * (2.0 / ROT_DIM))
    ang = pos.astype(F32)[:, None] * inv_freq[None, :]
    cos, sin = jnp.cos(ang), jnp.sin(ang)
    d = jnp.arange(LANES) % HEAD_DIM
    f = d % ROT_HALF
    cos_t = jnp.where(d[None, :] < ROT_DIM, cos[:, f], 1.0)
    s1_t = jnp.where(d[None, :] < ROT_HALF, -sin[:, f], 0.0)
    s2_t = jnp.where((d[None, :] >= ROT_HALF) & (d[None, :] < ROT_DIM), sin[:, f], 0.0)
    return cos_t, s1_t, s2_t


def _attn_prompt_kernel(q_ref, kp_ref, kc_ref, vp_ref, vc_ref, o_ref, l_ref):
    j = pl.program_id(1)
    lane = lax.broadcasted_iota(jnp.int32, (BAND, LANES), 1)
    head_a = lane < HEAD_DIM
    qi = lax.broadcasted_iota(jnp.int32, (BAND, 2 * BAND), 0)
    kj = lax.broadcasted_iota(jnp.int32, (BAND, 2 * BAND), 1)
    valid = (kj >= qi) & (kj <= qi + BAND) & ((j > 0) | (kj >= BAND))
    for hp in range(GROUP_W // LANES):
        sl = slice(hp * LANES, (hp + 1) * LANES)
        q2 = q_ref[:, sl]
        kk = jnp.concatenate([kp_ref[:, sl], kc_ref[:, sl]], axis=0)
        vv = jnp.concatenate([vp_ref[:, sl], vc_ref[:, sl]], axis=0)
        res = []
        for msk in (head_a, jnp.logical_not(head_a)):
            qm = jnp.where(msk, q2, jnp.zeros_like(q2))
            s = lax.dot_general(qm, kk, (((1,), (1,)), ((), ())), preferred_element_type=F32)
            s = jnp.where(valid, s, NEG)
            m = jnp.max(s, axis=1, keepdims=True)
            p = jnp.exp(s - m)
            l = jnp.sum(p, axis=1, keepdims=True)
            o = _dot((p / l).astype(BF16), vv)
            res.append((o, m + jnp.log(l)))
        o_ref[:, sl] = jnp.where(head_a, res[0][0], res[1][0])
        l_ref[:, sl] = jnp.where(head_a, res[0][1], res[1][1])


def _attn_prompt(q, k, v, dil):
    s = q.shape[0]
    n = s // dil
    view = lambda t: t.reshape(n, dil * GROUP_W)
    cur = lambda r, j: (j, r)
    prev = lambda r, j: (jnp.maximum(j - 1, 0), r)
    blk = (BAND, GROUP_W)
    o, lse = pl.pallas_call(
        _attn_prompt_kernel,
        grid=(dil, n // BAND),
        in_specs=[pl.BlockSpec(blk, cur), pl.BlockSpec(blk, prev), pl.BlockSpec(blk, cur),
                  pl.BlockSpec(blk, prev), pl.BlockSpec(blk, cur)],
        out_specs=[pl.BlockSpec(blk, cur), pl.BlockSpec(blk, cur)],
        out_shape=[jax.ShapeDtypeStruct((n, dil * GROUP_W), F32)] * 2,
        compiler_params=_params(2), name=f"attn_prompt_d{dil}",
    )(view(q), view(k), view(k), view(v), view(v))
    return o.reshape(s, GROUP_W), lse.reshape(s, GROUP_W)


def _attn_sample_kernel(c_ref, qt_ref, kt_ref, vt_ref, out_ref, ot_ref, lt_ref,
                        p_ref, pn_ref, *, dil, lc, nb):
    b = pl.program_id(0)
    kv = pl.program_id(1)
    lane_b = lax.broadcasted_iota(jnp.int32, (GROUP_W, nb), 1)
    pos = lax.broadcasted_iota(jnp.int32, (GROUP_W, lc), 1)

    def col(ref):
        return jnp.sum(jnp.where(lane_b == b, ref[...], 0.0), axis=1, keepdims=True)

    @pl.when((b == 0) & (kv == 0))
    def _():
        ot_ref[...] = jnp.zeros_like(ot_ref)
        lt_ref[...] = jnp.zeros_like(lt_ref)

    cin = c_ref[...]

    def shifted(new_col):
        return jnp.where(pos == lc - 1, new_col, pltpu.roll(cin, lc - 1, 1))

    @pl.when(kv == 0)
    def _():
        qc, kc = col(qt_ref), col(kt_ref)
        out_ref[...] = shifted(kc)
        s = jnp.sum((cin * qc).reshape(HEADS, HEAD_DIM, lc), axis=1)
        s_new = jnp.sum((kc * qc).reshape(HEADS, HEAD_DIM, 1), axis=1)
        hpos = lax.broadcasted_iota(jnp.int32, (HEADS, lc), 1)
        ok = (hpos & (dil - 1)) == 0
        s = jnp.where(ok, s, NEG)
        m = jnp.maximum(jnp.max(s, axis=1, keepdims=True), s_new)
        p = jnp.where(ok, jnp.exp(s - m), 0.0)
        pn = jnp.exp(s_new - m)
        l = jnp.sum(p, axis=1, keepdims=True) + pn
        p_ref[...] = p / l
        pn_ref[...] = jnp.broadcast_to(pn / l, pn_ref.shape)
        lane8 = lax.broadcasted_iota(jnp.int32, (HEADS, nb), 1)
        lt_ref[...] = jnp.where(lane8 == b, m + jnp.log(l), lt_ref[...])

    @pl.when(kv == 1)
    def _():
        vc = col(vt_ref)
        out_ref[...] = shifted(vc)
        pe = jnp.broadcast_to(p_ref[...][:, None, :], (HEADS, HEAD_DIM, lc)).reshape(GROUP_W, lc)
        pne = jnp.broadcast_to(pn_ref[:, 0:1][:, None, :], (HEADS, HEAD_DIM, 1)).reshape(GROUP_W, 1)
        o = jnp.sum(cin * pe, axis=1, keepdims=True) + vc * pne
        ot_ref[...] = jnp.where(lane_b == b, o, ot_ref[...])


def _attn_sample(cache_t, qt, kt, vt, dil):
    lc = cache_t.shape[1]
    nb = qt.shape[1]
    kern = functools.partial(_attn_sample_kernel, dil=dil, lc=lc, nb=nb)
    blk = pl.BlockSpec((GROUP_W, lc), lambda b, kv: (2 * b + kv, 0))
    tab = pl.BlockSpec((GROUP_W, nb), lambda b, kv: (0, 0))
    return pl.pallas_call(
        kern,
        grid=(nb, 2),
        in_specs=[blk, tab, tab, tab],
        out_specs=[blk, tab, pl.BlockSpec((HEADS, nb), lambda b, kv: (0, 0))],
        out_shape=[jax.ShapeDtypeStruct(cache_t.shape, F32),
                   jax.ShapeDtypeStruct((GROUP_W, nb), F32),
                   jax.ShapeDtypeStruct((HEADS, nb), F32)],
        scratch_shapes=[pltpu.VMEM((HEADS, lc), F32), pltpu.VMEM((HEADS, LANES), F32)],
        compiler_params=_params(2), name=f"attn_sample_d{dil}",
    )(cache_t, qt, kt, vt)


def _ssm_tables(lam_re, lam_im, log_dt, b_re, b_im, c_re, c_im):
    t = SSM_T
    lam = lax.complex(lam_re.astype(F32), lam_im.astype(F32))
    dt = jnp.exp(log_dt.astype(F32))[:, None]
    ldt = lam * dt
    lam_bar = jnp.exp(ldt)
    b_bar = ((lam_bar - 1.0) / lam)[..., None] * lax.complex(b_re.astype(F32), b_im.astype(F32))
    c = lax.complex(c_re.astype(F32), c_im.astype(F32))
    steps = jnp.arange(t + 1, dtype=F32)
    pw = jnp.exp(ldt[None] * steps[:, None, None])
    hi = lax.Precision.HIGHEST
    kern = jnp.einsum('gqn,kgn,gnp->gkpq', c, pw[:t], b_bar, precision=hi).real
    lag = jnp.arange(t)[None, :] - jnp.arange(t)[:, None]
    toep = jnp.where((lag >= 0)[None, :, None, :, None],
                     kern[:, jnp.clip(lag, 0, t - 1)].transpose(0, 1, 3, 2, 4), 0.0)
    toep = toep.reshape(SSM_G // 2, 2, t * SSM_P, t * SSM_P)
    z = jnp.zeros_like(toep[:, 0])
    toep2 = jnp.concatenate([jnp.concatenate([toep[:, 0], z], axis=2),
                             jnp.concatenate([z, toep[:, 1]], axis=2)], axis=1)
    win = (pw[:t][::-1][:, :, :, None] * b_bar[None]).transpose(1, 0, 3, 2)
    win = win.reshape(SSM_G // 2, 2, t * SSM_P, SSM_N)
    zn = jnp.zeros((SSM_G // 2, t * SSM_P, SSM_N), F32)
    w_in = jnp.concatenate([
        jnp.concatenate([win[:, 0].real, zn, win[:, 0].imag, zn], axis=2),
        jnp.concatenate([zn, win[:, 1].real, zn, win[:, 1].imag], axis=2)], axis=1)
    cl = (c[:, None] * pw[1:, :, None, :].transpose(1, 0, 2, 3)).transpose(0, 3, 1, 2)
    cl = cl.reshape(SSM_G // 2, 2, SSM_N, t * SSM_P)
    zc = jnp.zeros((SSM_G // 2, SSM_N, t * SSM_P), F32)
    w_out = jnp.concatenate([
        jnp.concatenate([cl[:, 0].real, zc], axis=2), jnp.concatenate([zc, cl[:, 1].real], axis=2),
        jnp.concatenate([-cl[:, 0].imag, zc], axis=2), jnp.concatenate([zc, -cl[:, 1].imag], axis=2)],
        axis=1)
    return toep2.astype(BF16), w_in.astype(BF16), w_out.astype(BF16), ldt


def _pair_lanes(re, im):
    lead = re.shape[:-2]
    re = re.reshape(lead + (SSM_G // 2, 2 * SSM_N))
    im = im.reshape(lead + (SSM_G // 2, 2 * SSM_N))
    return jnp.concatenate([re, im], axis=-1)


def _ssm_prompt_kernel(a_ref, toep_ref, win_ref, wout_ref, lp_ref, y_ref, hl_ref, *, nsteps):
    a = a_ref[...]
    nc = a.shape[0]
    half = 2 * SSM_N
    y_intra = _dot(a, toep_ref[...])
    s = _dot(a, win_ref[...])
    xr, xi = s[:, :half], s[:, half:]
    row = lax.broadcasted_iota(jnp.int32, (nc, half), 0)

    def down(v, d):
        return jnp.where(row >= d, pltpu.roll(v, d, 0), 0.0)

    for k in range(nsteps):
        d = 1 << k
        lr, li = lp_ref[k:k + 1, :half], lp_ref[k:k + 1, half:]
        sr, si = down(xr, d), down(xi, d)
        xr, xi = xr + lr * sr - li * si, xi + lr * si + li * sr
    hl_ref[...] = jnp.concatenate([xr[nc - 8:], xi[nc - 8:]], axis=1)
    h_in = jnp.concatenate([down(xr, 1), down(xi, 1)], axis=1).astype(BF16)
    y_ref[...] = y_intra + _dot(h_in, wout_ref[...])


def _ssm_prompt(u, toep2, w_in, w_out, ldt):
    s = u.shape[0]
    nc = s // SSM_T
    gp = SSM_G // 2
    nsteps = int(math.log2(nc))
    kw = 2 * SSM_T * SSM_P
    a = u.astype(BF16).reshape(nc, SSM_T, gp, 2, SSM_P).transpose(2, 0, 3, 1, 4).reshape(gp, nc, kw)
    scale = (SSM_T * (2.0 ** jnp.arange(nsteps, dtype=F32)))[:, None, None]
    lp = jnp.exp(ldt[None] * scale)
    lp = _pair_lanes(lp.real, lp.imag).transpose(1, 0, 2)
    sq = lambda *shape: pl.BlockSpec((None,) + shape, lambda g: (g, 0, 0))
    y, hl = pl.pallas_call(
        functools.partial(_ssm_prompt_kernel, nsteps=nsteps),
        grid=(gp,),
        in_specs=[sq(nc, kw), sq(kw, kw), sq(kw, 4 * SSM_N), sq(4 * SSM_N, kw), sq(nsteps, 4 * SSM_N)],
        out_specs=[sq(nc, kw), sq(8, 4 * SSM_N)],
        out_shape=[jax.ShapeDtypeStruct((gp, nc, kw), F32),
                   jax.ShapeDtypeStruct((gp, 8, 4 * SSM_N), F32)],
        compiler_params=_params(), name="ssm_prompt",
    )(a, toep2, w_in, w_out, lp)
    y = y.reshape(gp, nc, 2, SSM_T, SSM_P).transpose(1, 3, 0, 2, 4).reshape(s, SSM_WIDTH)
    hl = hl[:, 7, :]
    h_re = hl[:, :2 * SSM_N].reshape(SSM_G, SSM_N)
    h_im = hl[:, 2 * SSM_N:].reshape(SSM_G, SSM_N)
    return y, h_re, h_im


def _ssm_sample_kernel(u_ref, hr_ref, hi_ref, lr_ref, li_ref, br_ref, bi_ref, cr_ref, ci_ref,
                       y_ref, or_ref, oi_ref):
    u = u_ref[...].astype(BF16)
    hr, hi = hr_ref[...], hi_ref[...]
    lr, li = lr_ref[...], li_ref[...]
    nr = lr * hr - li * hi + _dot(u, br_ref[...])
    ni = lr * hi + li * hr + _dot(u, bi_ref[...])
    or_ref[...] = nr
    oi_ref[...] = ni
    y_ref[...] = _dot(nr.astype(BF16), cr_ref[...]) + _dot(ni.astype(BF16), ci_ref[...])


def _ssm_sample(u, h_re, h_im, lam_re, lam_im, log_dt, b_re, b_im, c_re, c_im):
    nb = u.shape[0]
    lam = lax.complex(lam_re.astype(F32), lam_im.astype(F32))
    dt = jnp.exp(log_dt.astype(F32))[:, None]
    lam_bar = jnp.exp(lam * dt)
    b_bar = ((lam_bar - 1.0) / lam)[..., None] * lax.complex(b_re.astype(F32), b_im.astype(F32))
    eye = jnp.eye(SSM_G, dtype=F32)
    bd_b = jnp.einsum('gnp,gh->gphn', b_bar, eye).reshape(SSM_WIDTH, SSM_G * SSM_N)
    bd_cr = jnp.einsum('gpn,gh->gnhp', c_re.astype(F32), eye).reshape(SSM_G * SSM_N, SSM_WIDTH)
    bd_ci = jnp.einsum('gpn,gh->gnhp', -c_im.astype(F32), eye).reshape(SSM_G * SSM_N, SSM_WIDTH)
    gn = SSM_G * SSM_N
    args = (u, h_re, h_im, lam_bar.real.reshape(1, gn), lam_bar.imag.reshape(1, gn),
            bd_b.real.astype(BF16), bd_b.imag.astype(BF16), bd_cr.astype(BF16), bd_ci.astype(BF16))
    return pl.pallas_call(
        _ssm_sample_kernel,
        grid=(1,),
        in_specs=[_const_spec(a.shape) for a in args],
        out_specs=[_const_spec((nb, SSM_WIDTH)), _const_spec((nb, gn)), _const_spec((nb, gn))],
        out_shape=[jax.ShapeDtypeStruct((nb, SSM_WIDTH), F32),
                   jax.ShapeDtypeStruct((nb, gn), F32), jax.ShapeDtypeStruct((nb, gn), F32)],
        compiler_params=_params(), name="ssm_sample",
    )(*args)


def _post_kernel(x_ref, o1_ref, o2_ref, o3_ref, l1_ref, l2_ref, l3_ref, y_ref, u_ref, gs_ref, ga_ref,
                 d_ref, wglu_ref, bglu_ref, wbs_ref, wba_ref, wout_ref, nm_ref, wup_ref, wdn_ref,
                 nf_ref, out_ref):
    l1, l2, l3 = l1_ref[...], l2_ref[...], l3_ref[...]
    m = jnp.maximum(jnp.maximum(l1, l2), l3)
    e1, e2, e3 = jnp.exp(l1 - m), jnp.exp(l2 - m), jnp.exp(l3 - m)
    den = e1 + e2 + e3
    attn = (e1 / den) * o1_ref[...] + (e2 / den) * o2_ref[...] + (e3 / den) * o3_ref[...]
    y = jax.nn.gelu(y_ref[...] + d_ref[...] * u_ref[...])
    ssm_y = y * _sigmoid(_dot(y.astype(BF16), wglu_ref[...]) + bglu_ref[...])
    z = (_sigmoid(gs_ref[...]) * _dot(ssm_y.astype(BF16), wbs_ref[...])
         + _sigmoid(ga_ref[...]) * _dot(attn.astype(BF16), wba_ref[...]))
    x1 = x_ref[...] + _dot(z.astype(BF16), wout_ref[...])
    hm = _rms(x1, nm_ref[...]).astype(BF16)
    up = jnp.maximum(_dot(hm, wup_ref[...]), 0.0)
    x2 = x1 + _dot((up * up).astype(BF16), wdn_ref[...])
    out_ref[...] = _rms(x2, nf_ref[...])


def _post(x, os_, ls_, y_ssm, u, gs, ga, consts, tm):
    s = x.shape[0]
    row = lambda i: (i, 0)
    acts = [x, *os_, *ls_, y_ssm, u, gs, ga]
    return pl.pallas_call(
        _post_kernel,
        grid=(s // tm,),
        in_specs=([pl.BlockSpec((tm, a.shape[1]), row) for a in acts]
                  + [_const_spec(c.shape) for c in consts]),
        out_specs=pl.BlockSpec((tm, D_MODEL), row),
        out_shape=jax.ShapeDtypeStruct((s, D_MODEL), F32),
        compiler_params=_params(), name="post",
    )(*acts, *consts)


def kernel(x_prompt, x_sample, cache_kv_g1, cache_kv_g2, cache_kv_g3, state_ssm, norm_mix, w_in,
           ssm_lambda_re, ssm_lambda_im, ssm_log_dt, ssm_b_re, ssm_b_im, ssm_c_re, ssm_c_im, ssm_d,
           w_glu, b_glu, w_branch_ssm, w_branch_attn, w_out, norm_mlp, w_up, w_down, norm_final):
    depth = w_in.shape[0]
    assert depth == 1, "one decoder layer"
    seq = x_prompt.shape[1]
    nb = x_sample.shape[0]
    assert x_prompt.shape[0] == 1 and x_sample.shape[1] == 1
    caches = (cache_kv_g1, cache_kv_g2, cache_kv_g3)
    windows = tuple(w for w, _ in ATTN_GROUPS)
    for c, w in zip(caches, windows):
        assert c.shape[2] == w, "cache holds exactly one window"

    w_in_bf = w_in[0].astype(BF16)
    gain_mix = norm_mix[0][None, :]
    consts = (ssm_d[0][None, :], w_glu[0].astype(BF16), b_glu[0][None, :],
              w_branch_ssm[0].astype(BF16), w_branch_attn[0].astype(BF16), w_out[0].astype(BF16),
              norm_mlp[0][None, :], w_up[0].astype(BF16), w_down[0].astype(BF16), norm_final[None, :])
    ssm_p = (ssm_lambda_re[0], ssm_lambda_im[0], ssm_log_dt[0], ssm_b_re[0], ssm_b_im[0],
             ssm_c_re[0], ssm_c_im[0])

    xp = x_prompt[0]
    tm_p = 256
    res = _inproj(xp, gain_mix, w_in_bf, *_rotary_tables(jnp.arange(seq)), tm_p, windows)
    qs, ks, vs, tails = res[0:3], res[3:6], res[6:9], res[9:12]
    u_p, gs_p, ga_p = res[12:15]
    os_p, ls_p = zip(*[_attn_prompt(qs[g], ks[g], vs[g], ATTN_GROUPS[g][1]) for g in range(N_GROUPS)])
    toep2, tw_in, tw_out, ldt = _ssm_tables(*ssm_p)
    y_p, hp_re, hp_im = _ssm_prompt(u_p, toep2, tw_in, tw_out, ldt)
    y_prompt = _post(xp, os_p, ls_p, y_p, u_p, gs_p, ga_p, consts, tm_p)[None]
    kv_prompt = [t.reshape(1, 1, w, 2, HEADS, HEAD_DIM) for t, w in zip(tails, windows)]
    ssm_prompt = jnp.stack([hp_re, hp_im], axis=-1)[None, None].astype(state_ssm.dtype)

    xs = x_sample[:, 0]
    pos_s = jnp.full((nb,), PAST_LEN, jnp.int32)
    res = _inproj(xs, gain_mix, w_in_bf, *_rotary_tables(pos_s), nb, (nb,) * N_GROUPS)
    qs, tails = res[0:3], res[9:12]
    u_s, gs_s, ga_s = res[12:15]
    os_s, ls_s, kv_sample = [], [], []
    for g, (window, dil) in enumerate(ATTN_GROUPS):
        lc = caches[g].shape[2]
        cache_t = caches[g][0].transpose(0, 2, 3, 4, 1).reshape(nb * 2 * GROUP_W, lc)
        kt = tails[g][:, :GROUP_W].T
        vt = tails[g][:, GROUP_W:].T
        new_t, ot, lt = _attn_sample(cache_t, qs[g].astype(F32).T, kt, vt, dil)
        kv_sample.append(new_t.reshape(nb, 2, HEADS, HEAD_DIM, lc).transpose(0, 4, 1, 2, 3)[None])
        os_s.append(ot.T)
        ls_s.append(jnp.repeat(lt.T, HEAD_DIM, axis=1))
    st = state_ssm[0].astype(F32)
    gn = SSM_G * SSM_N
    y_s, hs_re, hs_im = _ssm_sample(u_s, st[..., 0].reshape(nb, gn), st[..., 1].reshape(nb, gn), *ssm_p)
    y_sample = _post(xs, os_s, ls_s, y_s, u_s, gs_s, ga_s, consts, nb)[:, None]
    ssm_sample = jnp.stack([hs_re.reshape(nb, SSM_G, SSM_N), hs_im.reshape(nb, SSM_G, SSM_N)],
                           axis=-1)[None].astype(state_ssm.dtype)

    return (y_prompt, y_sample, kv_prompt[0], kv_prompt[1], kv_prompt[2], ssm_prompt,
            kv_sample[0], kv_sample[1], kv_sample[2], ssm_sample)
```

```python
import functools
import math

import jax
import jax.numpy as jnp
from jax import lax
from jax.experimental import pallas as pl
from jax.experimental.pallas import tpu as pltpu

F32 = jnp.float32
BF16 = jnp.bfloat16

D_MODEL = 1024
HEAD_DIM = 64
HEADS = 8
GROUP_W = HEADS * HEAD_DIM
ATTN_GROUPS = ((128, 1), (512, 4), (2048, 16))
N_GROUPS = len(ATTN_GROUPS)
QKV_W = N_GROUPS * GROUP_W
BAND = 128
ROT_DIM = HEAD_DIM // 4
ROT_HALF = ROT_DIM // 2
ROPE_THETA = 500000.0
PAST_LEN = 8192
SSM_WIDTH = 512
SSM_P = 16
SSM_G = SSM_WIDTH // SSM_P
SSM_N = 64
SSM_T = 16
EPS = 1e-6
NEG = -1e30
LANES = 128
N_TILES = GROUP_W // LANES
MIB = 1024 * 1024

SSM_Q = SSM_WIDTH // LANES
SSM_GQ = SSM_G // SSM_Q
SSM_SQ = SSM_GQ * SSM_N
SSM_PAIRS = SSM_T // 2

OFF_Q, OFF_K, OFF_V = 0, QKV_W, 2 * QKV_W
OFF_U = 3 * QKV_W
OFF_GS = OFF_U + SSM_WIDTH
OFF_GA = OFF_GS + D_MODEL
IN_W = OFF_GA + D_MODEL


def _const_spec(shape):
    nd = len(shape)
    return pl.BlockSpec(shape, lambda *_: (0,) * nd, pipeline_mode=pl.Buffered(1))


def _params(n_axes, vmem_mib):
    return pltpu.CompilerParams(dimension_semantics=("arbitrary",) * n_axes,
                                vmem_limit_bytes=vmem_mib * MIB)


def _sigmoid(x):
    return 1.0 / (1.0 + jnp.exp(-x))


def _rms(x, g):
    return x * lax.rsqrt(jnp.mean(x * x, axis=-1, keepdims=True) + EPS) * g


def _dot(a, b):
    return jnp.dot(a, b, preferred_element_type=F32)


def _inproj_kernel(x_ref, g_ref, w_ref, cos_ref, s1_ref, s2_ref, *refs, tm, dils, tail_rows, first_tail):
    q_refs, k_refs, v_refs, tail_refs = refs[0:3], refs[3:6], refs[6:9], refs[9:12]
    u_ref, gs_ref, ga_ref, scr = refs[12:16]
    i = pl.program_id(0)
    h = _rms(x_ref[...], g_ref[...]).astype(BF16)
    cos, s1, s2 = cos_ref[...], s1_ref[...], s2_ref[...]

    def rot(a):
        parts = []
        for j in range(GROUP_W // LANES):
            seg = a[:, j * LANES:(j + 1) * LANES]
            parts.append(seg * cos + pltpu.roll(seg, LANES - ROT_HALF, 1) * s1
                         + pltpu.roll(seg, ROT_HALF, 1) * s2)
        return jnp.concatenate(parts, axis=1)

    def put(out_ref, val, dil):
        if dil == 1:
            out_ref[0] = val.astype(BF16)
            return
        for t in range(N_TILES):
            scr[t] = val[:, t * LANES:(t + 1) * LANES]
        for r in range(dil):
            rows = [scr[t, pl.ds(r, tm // dil, stride=dil), :] for t in range(N_TILES)]
            out_ref[r] = jnp.concatenate(rows, axis=1).astype(BF16)

    for gi in range(N_GROUPS):
        c0 = gi * GROUP_W
        qa = rot(_dot(h, w_ref[:, OFF_Q + c0:OFF_Q + c0 + GROUP_W]))
        ka = rot(_dot(h, w_ref[:, OFF_K + c0:OFF_K + c0 + GROUP_W]))
        va = _dot(h, w_ref[:, OFF_V + c0:OFF_V + c0 + GROUP_W])
        put(q_refs[gi], qa * (HEAD_DIM ** -0.5), dils[gi])
        put(k_refs[gi], ka, dils[gi])
        put(v_refs[gi], va, dils[gi])
        tb = tail_rows[gi]

        @pl.when(i >= first_tail[gi])
        def _(ka=ka, va=va, gi=gi, tb=tb):
            tail_refs[gi][:, 0:GROUP_W] = ka[tm - tb:, :]
            tail_refs[gi][:, GROUP_W:2 * GROUP_W] = va[tm - tb:, :]

    u_ref[...] = _dot(h, w_ref[:, OFF_U:OFF_GS])
    gs_ref[...] = _dot(h, w_ref[:, OFF_GS:OFF_GA])
    ga_ref[...] = _dot(h, w_ref[:, OFF_GA:IN_W])


def _inproj(x, gain, w_bf, cos, s1, s2, tm, windows, dils):
    s = x.shape[0]
    ntiles = s // tm
    tail_rows = tuple(min(w, tm) for w in windows)
    first_tail = tuple(ntiles - w // tb for w, tb in zip(windows, tail_rows))
    row = lambda i: (i, 0)
    res = lambda i: (0, i, 0)
    qkv_shapes = [jax.ShapeDtypeStruct((d, s // d, GROUP_W), BF16) for d in dils]
    qkv_specs = [pl.BlockSpec((d, tm // d, GROUP_W), res) for d in dils]
    out_shape = (qkv_shapes * 3
                 + [jax.ShapeDtypeStruct((w, 2 * GROUP_W), F32) for w in windows]
                 + [jax.ShapeDtypeStruct((s, SSM_WIDTH), F32),
                    jax.ShapeDtypeStruct((s, D_MODEL), F32),
                    jax.ShapeDtypeStruct((s, D_MODEL), F32)])
    out_specs = (qkv_specs * 3
                 + [pl.BlockSpec((tb, 2 * GROUP_W), functools.partial(
                     lambda i, ft: (jnp.maximum(i - ft, 0), 0), ft=ft))
                    for tb, ft in zip(tail_rows, first_tail)]
                 + [pl.BlockSpec((tm, SSM_WIDTH), row),
                    pl.BlockSpec((tm, D_MODEL), row),
                    pl.BlockSpec((tm, D_MODEL), row)])
    kern = functools.partial(_inproj_kernel, tm=tm, dils=dils, tail_rows=tail_rows,
                             first_tail=first_tail)
    return pl.pallas_call(
        kern,
        grid=(ntiles,),
        in_specs=[pl.BlockSpec((tm, D_MODEL), row), _const_spec((1, D_MODEL)),
                  _const_spec((D_MODEL, IN_W)),
                  pl.BlockSpec((tm, LANES), row), pl.BlockSpec((tm, LANES), row),
                  pl.BlockSpec((tm, LANES), row)],
        out_specs=out_specs, out_shape=out_shape,
        scratch_shapes=[pltpu.VMEM((N_TILES, tm, LANES), F32)],
        compiler_params=_params(1, 56), name="inproj",
    )(x, gain, w_bf, cos, s1, s2)


def _rotary_tables(pos):
    inv_freq = jnp.exp(-math.log(ROPE_THETA) * jnp.arange(ROT_HALF, dtype=F32) * (2.0 / ROT_DIM))
    ang = pos.astype(F32)[:, None] * inv_freq[None, :]
    cos, sin = jnp.cos(ang), jnp.sin(ang)
    n = pos.shape[0]
    one = jnp.ones((n, HEAD_DIM - ROT_DIM), F32)
    zero = jnp.zeros((n, HEAD_DIM - ROT_DIM), F32)
    z8 = jnp.zeros((n, ROT_HALF), F32)
    two_heads = lambda parts: jnp.concatenate(parts * (LANES // HEAD_DIM), axis=1)
    return two_heads([cos, cos, one]), two_heads([-sin, z8, zero]), two_heads([z8, sin, zero])


def _attn_prompt_kernel(q_ref, kp_ref, kc_ref, vp_ref, vc_ref, o_ref, l_ref):
    j = pl.program_id(1)
    lane = lax.broadcasted_iota(jnp.int32, (BAND, LANES), 1)
    head_a = lane < HEAD_DIM
    qi = lax.broadcasted_iota(jnp.int32, (BAND, 2 * BAND), 0)
    kj = lax.broadcasted_iota(jnp.int32, (BAND, 2 * BAND), 1)
    valid = (kj >= qi) & (kj <= qi + BAND) & ((j > 0) | (kj >= BAND))
    for hp in range(GROUP_W // LANES):
        sl = slice(hp * LANES, (hp + 1) * LANES)
        q2 = q_ref[:, sl]
        kk = jnp.concatenate([kp_ref[:, sl], kc_ref[:, sl]], axis=0)
        vv = jnp.concatenate([vp_ref[:, sl], vc_ref[:, sl]], axis=0)
        res = []
        for msk in (head_a, jnp.logical_not(head_a)):
            qm = jnp.where(msk, q2, jnp.zeros_like(q2))
            s = lax.dot_general(qm, kk, (((1,), (1,)), ((), ())), preferred_element_type=F32)
            s = jnp.where(valid, s, NEG)
            m = jnp.max(s, axis=1, keepdims=True)
            p = jnp.exp(s - m)
            l = jnp.sum(p, axis=1, keepdims=True)
            o = _dot((p / l).astype(BF16), vv)
            res.append((o, m + jnp.log(l)))
        o_ref[:, sl] = jnp.where(head_a, res[0][0], res[1][0])
        l_ref[:, sl] = jnp.where(head_a, res[0][1], res[1][1])


def _attn_prompt(q, k, v):
    dil, n, _ = q.shape
    cur = lambda r, j: (r, j, 0)
    prev = lambda r, j: (r, jnp.maximum(j - 1, 0), 0)
    blk = (None, BAND, GROUP_W)
    return pl.pallas_call(
        _attn_prompt_kernel,
        grid=(dil, n // BAND),
        in_specs=[pl.BlockSpec(blk, cur), pl.BlockSpec(blk, prev), pl.BlockSpec(blk, cur),
                  pl.BlockSpec(blk, prev), pl.BlockSpec(blk, cur)],
        out_specs=[pl.BlockSpec(blk, cur), pl.BlockSpec(blk, cur)],
        out_shape=[jax.ShapeDtypeStruct((dil, n, GROUP_W), F32)] * 2,
        compiler_params=_params(2, 32), name=f"attn_prompt_d{dil}",
    )(q, k, k, v, v)


def _attn_sample_kernel(c_ref, q_ref, k_ref, v_ref, out_ref, o_ref, l_ref, *, dil, lc, bb):
    lane_b = lax.broadcasted_iota(jnp.int32, (GROUP_W, bb), 1)
    lane_h = lax.broadcasted_iota(jnp.int32, (HEADS, bb), 1)
    last = lax.broadcasted_iota(jnp.int32, (GROUP_W, LANES), 1) == LANES - 1
    hpos = lax.broadcasted_iota(jnp.int32, (HEADS, lc), 1)
    ok = (hpos & (dil - 1)) == 0
    o_acc = jnp.zeros((GROUP_W, bb), F32)
    l_acc = jnp.zeros((HEADS, bb), F32)

    def shift_into(row0, cin, new_col):
        rolled = pltpu.roll(cin, lc - 1, 1)
        out_ref[row0:row0 + GROUP_W, :] = rolled
        out_ref[row0:row0 + GROUP_W, lc - LANES:] = jnp.where(last, new_col, rolled[:, lc - LANES:])

    for jb in range(bb):
        qc, kc, vc = q_ref[:, jb:jb + 1], k_ref[:, jb:jb + 1], v_ref[:, jb:jb + 1]
        r0 = jb * 2 * GROUP_W
        kin = c_ref[r0:r0 + GROUP_W, :]
        shift_into(r0, kin, kc)
        s = jnp.sum((kin * qc).reshape(HEADS, HEAD_DIM, lc), axis=1)
        s_new = jnp.sum((kc * qc).reshape(HEADS, HEAD_DIM, 1), axis=1)
        s = jnp.where(ok, s, NEG)
        m = jnp.maximum(jnp.max(s, axis=1, keepdims=True), s_new)
        p = jnp.where(ok, jnp.exp(s - m), 0.0)
        pn = jnp.exp(s_new - m)
        l = jnp.sum(p, axis=1, keepdims=True) + pn
        p = p / l
        pn = pn / l
        vin = c_ref[r0 + GROUP_W:r0 + 2 * GROUP_W, :]
        shift_into(r0 + GROUP_W, vin, vc)
        pe = jnp.broadcast_to(p[:, None, :], (HEADS, HEAD_DIM, lc)).reshape(GROUP_W, lc)
        pne = jnp.broadcast_to(pn[:, None, :], (HEADS, HEAD_DIM, 1)).reshape(GROUP_W, 1)
        o = jnp.sum(vin * pe, axis=1, keepdims=True) + vc * pne
        o_acc = jnp.where(lane_b == jb, o, o_acc)
        l_acc = jnp.where(lane_h == jb, m + jnp.log(l), l_acc)
    o_ref[...] = o_acc
    l_ref[...] = l_acc


def _attn_sample(cache_t, q, k_new, v_new, dil, bb):
    lc = cache_t.shape[1]
    nb = q.shape[0]
    nblk = nb // bb
    cols = lambda t: t.astype(F32).reshape(nblk, bb, GROUP_W).transpose(0, 2, 1)
    kern = functools.partial(_attn_sample_kernel, dil=dil, lc=lc, bb=bb)
    blk = pl.BlockSpec((bb * 2 * GROUP_W, lc), lambda i: (i, 0))
    tab = pl.BlockSpec((None, GROUP_W, bb), lambda i: (i, 0, 0))
    new_t, o, lse = pl.pallas_call(
        kern,
        grid=(nblk,),
        in_specs=[blk, tab, tab, tab],
        out_specs=[blk, tab, pl.BlockSpec((None, HEADS, bb), lambda i: (i, 0, 0))],
        out_shape=[jax.ShapeDtypeStruct(cache_t.shape, F32),
                   jax.ShapeDtypeStruct((nblk, GROUP_W, bb), F32),
                   jax.ShapeDtypeStruct((nblk, HEADS, bb), F32)],
        compiler_params=_params(1, 56), name=f"attn_sample_d{dil}",
    )(cache_t, cols(q), cols(k_new), cols(v_new))
    return (new_t, o.transpose(0, 2, 1).reshape(nb, GROUP_W), lse.transpose(0, 2, 1).reshape(nb, HEADS))


def _ssm_discretise(lam_re, lam_im, log_dt, b_re, b_im, c_re, c_im):
    lam = lax.complex(lam_re.astype(F32), lam_im.astype(F32))
    dt = jnp.exp(log_dt.astype(F32))[:, None]
    ldt = lam * dt
    lam_bar = jnp.exp(ldt)
    b_bar = ((lam_bar - 1.0) / lam)[..., None] * lax.complex(b_re.astype(F32), b_im.astype(F32))
    c = lax.complex(c_re.astype(F32), c_im.astype(F32))
    return ldt, lam_bar, b_bar, c


def _ssm_tables(ssm_p):
    ldt, _, b_bar, c = _ssm_discretise(*ssm_p)
    t, q, gq, pr = SSM_T, SSM_Q, SSM_GQ, SSM_PAIRS
    steps = jnp.arange(t + 1, dtype=F32)
    pw = jnp.exp(ldt[None] * steps[:, None, None])
    hi = lax.Precision.HIGHEST
    eye = jnp.eye(gq, dtype=F32)
    kern = jnp.einsum('gqn,kgn,gnp->gkpq', c, pw[:t], b_bar, precision=hi).real
    kern = kern.reshape(q, gq, t, SSM_P, SSM_P)
    d_, a_, b_ = jnp.meshgrid(jnp.arange(pr), jnp.arange(2), jnp.arange(2), indexing='ij')
    lag = 2 * d_ + b_ - a_
    kl = jnp.where((lag >= 0)[None, None, :, :, :, None, None], kern[:, :, jnp.maximum(lag, 0)], 0.0)
    toep = jnp.einsum('qgdabpr,gh->qdagpbhr', kl, eye).reshape(q, pr, 2 * LANES, 2 * LANES)
    win = pw[:t][::-1][:, :, :, None] * b_bar[None]
    win = win.reshape(pr, 2, q, gq, SSM_N, SSM_P)
    w_in = jnp.stack([jnp.einsum('iaqgnp,gh->qiagphn', part, eye) for part in (win.real, win.imag)],
                     axis=5)
    w_in = w_in.reshape(q, pr, 2 * LANES, 2 * SSM_SQ)
    cl = c[None] * pw[1:, :, None, :]
    cl = cl.reshape(pr, 2, q, gq, SSM_P, SSM_N)
    w_out = jnp.stack([jnp.einsum('jbqgpn,gh->qjgnbhp', part, eye) for part in (cl.real, -cl.imag)],
                      axis=2)
    w_out = w_out.reshape(q, pr, 2 * SSM_SQ, 2 * LANES)
    return toep.astype(BF16), w_in.astype(BF16), w_out.astype(BF16), ldt


def _ssm_prompt_kernel(u_ref, toep_ref, win_ref, wout_ref, lp_ref, y_ref, hl_ref, carry_ref, *, nsteps):
    tt = pl.program_id(1)
    nc = u_ref.shape[0] // SSM_T
    sq = SSM_SQ

    @pl.when(tt == 0)
    def _():
        carry_ref[...] = jnp.zeros_like(carry_ref)

    step = lambda t: u_ref[pl.ds(t, nc, stride=SSM_T), :].astype(BF16)
    a = [jnp.concatenate([step(2 * i), step(2 * i + 1)], axis=1) for i in range(SSM_PAIRS)]
    s = _dot(a[0], win_ref[0])
    for i in range(1, SSM_PAIRS):
        s = s + _dot(a[i], win_ref[i])
    xr, xi = s[:, :sq], s[:, sq:]
    row = lax.broadcasted_iota(jnp.int32, (nc, sq), 0)
    cr, ci = carry_ref[7:8, :sq], carry_ref[7:8, sq:]
    lr, li = lp_ref[0:1, :sq], lp_ref[0:1, sq:]
    xr = xr + jnp.where(row == 0, lr * cr - li * ci, 0.0)
    xi = xi + jnp.where(row == 0, lr * ci + li * cr, 0.0)

    def down(v, d):
        return jnp.where(row >= d, pltpu.roll(v, d, 0), 0.0)

    for k in range(nsteps):
        d = 1 << k
        lr, li = lp_ref[k:k + 1, :sq], lp_ref[k:k + 1, sq:]
        sr, si = down(xr, d), down(xi, d)
        xr, xi = xr + lr * sr - li * si, xi + lr * si + li * sr
    hr = jnp.where(row == 0, cr, pltpu.roll(xr, 1, 0))
    hi = jnp.where(row == 0, ci, pltpu.roll(xi, 1, 0))
    h_in = jnp.concatenate([hr, hi], axis=1).astype(BF16)
    last = jnp.concatenate([xr[nc - 8:], xi[nc - 8:]], axis=1)
    carry_ref[...] = last
    hl_ref[...] = last
    for j in range(SSM_PAIRS):
        acc = _dot(h_in, wout_ref[j])
        for i in range(j + 1):
            acc = acc + _dot(a[i], toep_ref[j - i])
        y_ref[pl.ds(2 * j, nc, stride=SSM_T), :] = acc[:, :LANES]
        y_ref[pl.ds(2 * j + 1, nc, stride=SSM_T), :] = acc[:, LANES:]


def _ssm_prompt(u, tables):
    toep, w_in, w_out, ldt = tables
    s = u.shape[0]
    tile = min(s, 8192)
    nc = tile // SSM_T
    nsteps = int(math.log2(nc))
    scale = (SSM_T * (2.0 ** jnp.arange(nsteps, dtype=F32)))[:, None, None]
    lp = jnp.exp(ldt[None] * scale)
    lp = jnp.concatenate([lp.real.reshape(nsteps, SSM_Q, SSM_SQ), lp.imag.reshape(nsteps, SSM_Q, SSM_SQ)],
                         axis=2).transpose(1, 0, 2)
    per_q = lambda *shape: pl.BlockSpec((None,) + shape, lambda q, t: (q,) + (0,) * len(shape))
    y, hl = pl.pallas_call(
        functools.partial(_ssm_prompt_kernel, nsteps=nsteps),
        grid=(SSM_Q, s // tile),
        in_specs=[pl.BlockSpec((tile, LANES), lambda q, t: (t, q)),
                  per_q(SSM_PAIRS, 2 * LANES, 2 * LANES), per_q(SSM_PAIRS, 2 * LANES, 2 * SSM_SQ),
                  per_q(SSM_PAIRS, 2 * SSM_SQ, 2 * LANES), per_q(nsteps, 2 * SSM_SQ)],
        out_specs=[pl.BlockSpec((tile, LANES), lambda q, t: (t, q)), per_q(8, 2 * SSM_SQ)],
        out_shape=[jax.ShapeDtypeStruct((s, SSM_WIDTH), F32),
                   jax.ShapeDtypeStruct((SSM_Q, 8, 2 * SSM_SQ), F32)],
        scratch_shapes=[pltpu.VMEM((8, 2 * SSM_SQ), F32)],
        compiler_params=_params(2, 56), name="ssm_prompt",
    )(u, toep, w_in, w_out, lp)
    hl = hl[:, 7, :]
    return y, hl[:, :SSM_SQ].reshape(SSM_G, SSM_N), hl[:, SSM_SQ:].reshape(SSM_G, SSM_N)


def _ssm_sample_kernel(u_ref, hr_ref, hi_ref, lr_ref, li_ref, br_ref, bi_ref, cr_ref, ci_ref,
                       y_ref, or_ref, oi_ref):
    u = u_ref[...].astype(BF16)
    hr, hi = hr_ref[...], hi_ref[...]
    lr, li = lr_ref[...], li_ref[...]
    nr = lr * hr - li * hi + _dot(u, br_ref[...])
    ni = lr * hi + li * hr + _dot(u, bi_ref[...])
    or_ref[...] = nr
    oi_ref[...] = ni
    y_ref[...] = _dot(nr.astype(BF16), cr_ref[...]) + _dot(ni.astype(BF16), ci_ref[...])


def _ssm_sample(u, h_re, h_im, ssm_p):
    nb = u.shape[0]
    _, lam_bar, b_bar, c = _ssm_discretise(*ssm_p)
    eye = jnp.eye(SSM_G, dtype=F32)
    gn = SSM_G * SSM_N
    bd = lambda part, eq, shape: jnp.einsum(eq, part, eye).reshape(shape).astype(BF16)
    args = (u, h_re, h_im, lam_bar.real.reshape(1, gn), lam_bar.imag.reshape(1, gn),
            bd(b_bar.real, 'gnp,gh->gphn', (SSM_WIDTH, gn)), bd(b_bar.imag, 'gnp,gh->gphn', (SSM_WIDTH, gn)),
            bd(c.real, 'gpn,gh->gnhp', (gn, SSM_WIDTH)), bd(-c.imag, 'gpn,gh->gnhp', (gn, SSM_WIDTH)))
    return pl.pallas_call(
        _ssm_sample_kernel,
        grid=(1,),
        in_specs=[_const_spec(a.shape) for a in args],
        out_specs=[_const_spec((nb, SSM_WIDTH)), _const_spec((nb, gn)), _const_spec((nb, gn))],
        out_shape=[jax.ShapeDtypeStruct((nb, SSM_WIDTH), F32),
                   jax.ShapeDtypeStruct((nb, gn), F32), jax.ShapeDtypeStruct((nb, gn), F32)],
        compiler_params=_params(1, 32), name="ssm_sample",
    )(*args)


def _post_kernel(x_ref, o1_ref, o2_ref, o3_ref, l1_ref, l2_ref, l3_ref, y_ref, u_ref, gs_ref, ga_ref,
                 d_ref, wglu_ref, bglu_ref, wbs_ref, wba_ref, wout_ref, nm_ref, wup_ref, wdn_ref,
                 nf_ref, out_ref, *scr, dils):
    tm = x_ref.shape[0]

    def natural(ref, dil, buf):
        if dil == 1:
            return ref[0]
        for r in range(dil):
            for t in range(N_TILES):
                buf[t, pl.ds(r, tm // dil, stride=dil), :] = ref[r, :, t * LANES:(t + 1) * LANES]
        return jnp.concatenate([buf[t] for t in range(N_TILES)], axis=1)

    o1, o2, o3 = (natural(r, d, scr[k]) for k, (r, d) in enumerate(zip((o1_ref, o2_ref, o3_ref), dils)))
    l1, l2, l3 = (natural(r, d, scr[3 + k]) for k, (r, d) in enumerate(zip((l1_ref, l2_ref, l3_ref), dils)))
    m = jnp.maximum(jnp.maximum(l1, l2), l3)
    e1, e2, e3 = jnp.exp(l1 - m), jnp.exp(l2 - m), jnp.exp(l3 - m)
    den = e1 + e2 + e3
    attn = (e1 / den) * o1 + (e2 / den) * o2 + (e3 / den) * o3
    y = jax.nn.gelu(y_ref[...] + d_ref[...] * u_ref[...])
    ssm_y = y * _sigmoid(_dot(y.astype(BF16), wglu_ref[...]) + bglu_ref[...])
    z = (_sigmoid(gs_ref[...]) * _dot(ssm_y.astype(BF16), wbs_ref[...])
         + _sigmoid(ga_ref[...]) * _dot(attn.astype(BF16), wba_ref[...]))
    x1 = x_ref[...] + _dot(z.astype(BF16), wout_ref[...])
    hm = _rms(x1, nm_ref[...]).astype(BF16)
    up = jnp.maximum(_dot(hm, wup_ref[...]), 0.0)
    x2 = x1 + _dot((up * up).astype(BF16), wdn_ref[...])
    out_ref[...] = _rms(x2, nf_ref[...])


def _post(x, os_, ls_, y_ssm, u, gs, ga, consts, tm):
    s = x.shape[0]
    row = lambda i: (i, 0)
    res = lambda i: (0, i, 0)
    dils = tuple(o.shape[0] for o in os_)
    grouped = [*os_, *ls_]
    flat = [y_ssm, u, gs, ga]
    return pl.pallas_call(
        functools.partial(_post_kernel, dils=dils),
        grid=(s // tm,),
        in_specs=([pl.BlockSpec((tm, D_MODEL), row)]
                  + [pl.BlockSpec((a.shape[0], tm // a.shape[0], GROUP_W), res) for a in grouped]
                  + [pl.BlockSpec((tm, a.shape[1]), row) for a in flat]
                  + [_const_spec(c.shape) for c in consts]),
        out_specs=pl.BlockSpec((tm, D_MODEL), row),
        out_shape=jax.ShapeDtypeStruct((s, D_MODEL), F32),
        scratch_shapes=[pltpu.VMEM((N_TILES, tm, LANES), F32)] * 6,
        compiler_params=_params(1, 56), name="post",
    )(x, *grouped, *flat, *consts)


def kernel(x_prompt, x_sample, cache_kv_g1, cache_kv_g2, cache_kv_g3, state_ssm, norm_mix, w_in,
           ssm_lambda_re, ssm_lambda_im, ssm_log_dt, ssm_b_re, ssm_b_im, ssm_c_re, ssm_c_im, ssm_d,
           w_glu, b_glu, w_branch_ssm, w_branch_attn, w_out, norm_mlp, w_up, w_down, norm_final):
    depth = w_in.shape[0]
    assert depth == 1, "one decoder layer"
    seq = x_prompt.shape[1]
    nb = x_sample.shape[0]
    assert x_prompt.shape[0] == 1 and x_sample.shape[1] == 1
    caches = (cache_kv_g1, cache_kv_g2, cache_kv_g3)
    windows = tuple(w for w, _ in ATTN_GROUPS)
    dils = tuple(d for _, d in ATTN_GROUPS)
    for c, w in zip(caches, windows):
        assert c.shape[2] == w, "cache holds exactly one window"

    w_in_bf = w_in[0].astype(BF16)
    gain_mix = norm_mix[0][None, :]
    consts = (ssm_d[0][None, :], w_glu[0].astype(BF16), b_glu[0][None, :],
              w_branch_ssm[0].astype(BF16), w_branch_attn[0].astype(BF16), w_out[0].astype(BF16),
              norm_mlp[0][None, :], w_up[0].astype(BF16), w_down[0].astype(BF16), norm_final[None, :])
    ssm_p = (ssm_lambda_re[0], ssm_lambda_im[0], ssm_log_dt[0], ssm_b_re[0], ssm_b_im[0],
             ssm_c_re[0], ssm_c_im[0])

    xp = x_prompt[0]
    tm_p = 256
    res = _inproj(xp, gain_mix, w_in_bf, *_rotary_tables(jnp.arange(seq)), tm_p, windows, dils)
    qs, ks, vs, tails = res[0:3], res[3:6], res[6:9], res[9:12]
    u_p, gs_p, ga_p = res[12:15]
    os_p, ls_p = zip(*[_attn_prompt(qs[g], ks[g], vs[g]) for g in range(N_GROUPS)])
    y_p, hp_re, hp_im = _ssm_prompt(u_p, _ssm_tables(ssm_p))
    y_prompt = _post(xp, os_p, ls_p, y_p, u_p, gs_p, ga_p, consts, tm_p)[None]
    kv_prompt = [t.reshape(1, 1, w, 2, HEADS, HEAD_DIM) for t, w in zip(tails, windows)]
    ssm_prompt = jnp.stack([hp_re, hp_im], axis=-1)[None, None].astype(state_ssm.dtype)

    xs = x_sample[:, 0]
    pos_s = jnp.full((nb,), PAST_LEN, jnp.int32)
    res = _inproj(xs, gain_mix, w_in_bf, *_rotary_tables(pos_s), nb, (nb,) * N_GROUPS, (1,) * N_GROUPS)
    qs, tails = res[0:3], res[9:12]
    u_s, gs_s, ga_s = res[12:15]
    os_s, ls_s, kv_sample = [], [], []
    for g, (window, dil) in enumerate(ATTN_GROUPS):
        lc = caches[g].shape[2]
        cache_t = caches[g][0].transpose(0, 2, 3, 4, 1).reshape(nb * 2 * GROUP_W, lc)
        bb = max(1, min(8, 1024 // lc))
        new_t, o, lse = _attn_sample(cache_t, qs[g][0], tails[g][:, :GROUP_W], tails[g][:, GROUP_W:],
                                     dil, bb)
        kv_sample.append(new_t.reshape(nb, 2, HEADS, HEAD_DIM, lc).transpose(0, 4, 1, 2, 3)[None])
        os_s.append(o[None])
        ls_s.append(jnp.repeat(lse, HEAD_DIM, axis=1)[None])
    st = state_ssm[0].astype(F32)
    gn = SSM_G * SSM_N
    y_s, hs_re, hs_im = _ssm_sample(u_s, st[..., 0].reshape(nb, gn), st[..., 1].reshape(nb, gn), ssm_p)
    y_sample = _post(xs, os_s, ls_s, y_s, u_s, gs_s, ga_s, consts, nb)[:, None]
    ssm_sample = jnp.stack([hs_re.reshape(nb, SSM_G, SSM_N), hs_im.reshape(nb, SSM_G, SSM_N)],
                           axis=-1)[None].astype(state_ssm.dtype)

    return (y_prompt, y_sample, kv_prompt[0], kv_prompt[1], kv_prompt[2], ssm_prompt,
            kv_sample[0], kv_sample[1], kv_sample[2], ssm_sample)
```

```python
import functools
import math

import jax
import jax.numpy as jnp
from jax import lax
from jax.experimental import pallas as pl
from jax.experimental.pallas import tpu as pltpu

F32 = jnp.float32
BF16 = jnp.bfloat16

D_MODEL = 1024
HEAD_DIM = 64
HEADS = 8
GROUP_W = HEADS * HEAD_DIM
ATTN_GROUPS = ((128, 1), (512, 4), (2048, 16))
N_GROUPS = len(ATTN_GROUPS)
QKV_W = N_GROUPS * GROUP_W
BAND = 128
Q_BLOCK = 512
SEQ_TAB = 8
ROT_DIM = HEAD_DIM // 4
ROT_HALF = ROT_DIM // 2
ROPE_THETA = 500000.0
PAST_LEN = 8192
SSM_WIDTH = 512
SSM_P = 16
SSM_G = SSM_WIDTH // SSM_P
SSM_N = 64
SSM_T = 16
EPS = 1e-6
NEG = -1e30
LANES = 128
N_TILES = GROUP_W // LANES
MIB = 1024 * 1024

SSM_Q = SSM_WIDTH // LANES
SSM_GQ = SSM_G // SSM_Q
SSM_SQ = SSM_GQ * SSM_N
SSM_PAIRS = SSM_T // 2

OFF_Q, OFF_K, OFF_V = 0, QKV_W, 2 * QKV_W
OFF_U = 3 * QKV_W
OFF_GS = OFF_U + SSM_WIDTH
OFF_GA = OFF_GS + D_MODEL
IN_W = OFF_GA + D_MODEL


def _const_spec(shape):
    nd = len(shape)
    return pl.BlockSpec(shape, lambda *_: (0,) * nd, pipeline_mode=pl.Buffered(1))


def _params(n_axes, vmem_mib):
    return pltpu.CompilerParams(dimension_semantics=("arbitrary",) * n_axes,
                                vmem_limit_bytes=vmem_mib * MIB)


def _sigmoid(x):
    return 1.0 / (1.0 + jnp.exp(-x))


def _rms(x, g):
    return x * lax.rsqrt(jnp.mean(x * x, axis=-1, keepdims=True) + EPS) * g


def _dot(a, b):
    return jnp.dot(a, b, preferred_element_type=F32)


def _inproj_kernel(x_ref, g_ref, w_ref, ca_ref, sa_ref, cb_ref, sb_ref, *refs, tm, dils, tail_rows,
                   first_tail):
    q_refs, k_refs, v_refs, tail_refs = refs[0:3], refs[3:6], refs[6:9], refs[9:12]
    u_ref, gs_ref, ga_ref, scr = refs[12:16]
    i = pl.program_id(0)
    h = _rms(x_ref[...], g_ref[...]).astype(BF16)
    ca, sa = ca_ref[pl.ds(i, 1), :], sa_ref[pl.ds(i, 1), :]
    cb, sb = cb_ref[...], sb_ref[...]
    cos_f = ca * cb - sa * sb
    sin_f = sa * cb + ca * sb
    hd = lax.broadcasted_iota(jnp.int32, (tm, LANES), 1) & (HEAD_DIM - 1)
    cos = jnp.where(hd < ROT_DIM, cos_f, 1.0)
    s1 = jnp.where(hd < ROT_HALF, -sin_f, 0.0)
    s2 = jnp.where((hd >= ROT_HALF) & (hd < ROT_DIM), sin_f, 0.0)

    def rot(a):
        parts = []
        for j in range(GROUP_W // LANES):
            seg = a[:, j * LANES:(j + 1) * LANES]
            parts.append(seg * cos + pltpu.roll(seg, LANES - ROT_HALF, 1) * s1
                         + pltpu.roll(seg, ROT_HALF, 1) * s2)
        return jnp.concatenate(parts, axis=1)

    def put(out_ref, val, dil):
        if dil == 1:
            out_ref[0] = val.astype(BF16)
            return
        for t in range(N_TILES):
            scr[t] = val[:, t * LANES:(t + 1) * LANES]
        for r in range(dil):
            rows = [scr[t, pl.ds(r, tm // dil, stride=dil), :] for t in range(N_TILES)]
            out_ref[r] = jnp.concatenate(rows, axis=1).astype(BF16)

    for gi in range(N_GROUPS):
        c0 = gi * GROUP_W
        qa = rot(_dot(h, w_ref[:, OFF_Q + c0:OFF_Q + c0 + GROUP_W]))
        ka = rot(_dot(h, w_ref[:, OFF_K + c0:OFF_K + c0 + GROUP_W]))
        va = _dot(h, w_ref[:, OFF_V + c0:OFF_V + c0 + GROUP_W])
        put(q_refs[gi], qa * (HEAD_DIM ** -0.5), dils[gi])
        put(k_refs[gi], ka, dils[gi])
        put(v_refs[gi], va, dils[gi])
        tb = tail_rows[gi]

        @pl.when(i >= first_tail[gi])
        def _(ka=ka, va=va, gi=gi, tb=tb):
            tail_refs[gi][:, 0:GROUP_W] = ka[tm - tb:, :]
            tail_refs[gi][:, GROUP_W:2 * GROUP_W] = va[tm - tb:, :]

    u_ref[...] = _dot(h, w_ref[:, OFF_U:OFF_GS])
    gs_ref[...] = _dot(h, w_ref[:, OFF_GS:OFF_GA])
    ga_ref[...] = _dot(h, w_ref[:, OFF_GA:IN_W])


def _inproj(x, gain, w_bf, tile_pos, row_pos, tm, windows, dils):
    s = x.shape[0]
    ntiles = s // tm
    inv_freq = jnp.exp(-math.log(ROPE_THETA) * jnp.arange(ROT_HALF, dtype=F32) * (2.0 / ROT_DIM))
    lane_freq = jnp.tile(inv_freq, LANES // ROT_HALF)[None, :]
    ang_a = tile_pos.astype(F32)[:, None] * lane_freq
    ang_b = row_pos.astype(F32)[:, None] * lane_freq
    tail_rows = tuple(min(w, tm) for w in windows)
    first_tail = tuple(ntiles - w // tb for w, tb in zip(windows, tail_rows))
    row = lambda i: (i, 0)
    res = lambda i: (0, i, 0)
    qkv_shapes = [jax.ShapeDtypeStruct((d, s // d, GROUP_W), BF16) for d in dils]
    qkv_specs = [pl.BlockSpec((d, tm // d, GROUP_W), res) for d in dils]
    out_shape = (qkv_shapes * 3
                 + [jax.ShapeDtypeStruct((w, 2 * GROUP_W), F32) for w in windows]
                 + [jax.ShapeDtypeStruct((s, SSM_WIDTH), F32),
                    jax.ShapeDtypeStruct((s, D_MODEL), F32),
                    jax.ShapeDtypeStruct((s, D_MODEL), F32)])
    out_specs = (qkv_specs * 3
                 + [pl.BlockSpec((tb, 2 * GROUP_W), functools.partial(
                     lambda i, ft: (jnp.maximum(i - ft, 0), 0), ft=ft))
                    for tb, ft in zip(tail_rows, first_tail)]
                 + [pl.BlockSpec((tm, SSM_WIDTH), row),
                    pl.BlockSpec((tm, D_MODEL), row),
                    pl.BlockSpec((tm, D_MODEL), row)])
    kern = functools.partial(_inproj_kernel, tm=tm, dils=dils, tail_rows=tail_rows,
                             first_tail=first_tail)
    return pl.pallas_call(
        kern,
        grid=(ntiles,),
        in_specs=[pl.BlockSpec((tm, D_MODEL), row), _const_spec((1, D_MODEL)),
                  _const_spec((D_MODEL, IN_W)),
                  _const_spec((ntiles, LANES)), _const_spec((ntiles, LANES)),
                  _const_spec((tm, LANES)), _const_spec((tm, LANES))],
        out_specs=out_specs, out_shape=out_shape,
        scratch_shapes=[pltpu.VMEM((N_TILES, tm, LANES), F32)],
        compiler_params=_params(1, 56), name="inproj",
    )(x, gain, w_bf, jnp.cos(ang_a), jnp.sin(ang_a), jnp.cos(ang_b), jnp.sin(ang_b))


def _attn_prompt_kernel(q_ref, kp_ref, kc_ref, vp_ref, vc_ref, o_ref, l_ref):
    j = pl.program_id(1)
    lane = lax.broadcasted_iota(jnp.int32, (BAND, LANES), 1)
    head_a = lane < HEAD_DIM
    qi = lax.broadcasted_iota(jnp.int32, (2 * BAND, 2 * BAND), 0) & (BAND - 1)
    kj = lax.broadcasted_iota(jnp.int32, (2 * BAND, 2 * BAND), 1)
    band = (kj >= qi) & (kj <= qi + BAND)
    bias = jnp.where(band, 0.0, NEG)
    bias_first = jnp.where(band & (kj >= BAND), 0.0, NEG)
    for t in range(q_ref.shape[0] // BAND):
        rows = slice(t * BAND, (t + 1) * BAND)
        b = bias if t > 0 else jnp.where(j > 0, bias, bias_first)
        for hp in range(N_TILES):
            sl = slice(hp * LANES, (hp + 1) * LANES)
            q2 = q_ref[rows, sl]
            if t == 0:
                kk = jnp.concatenate([kp_ref[:, sl], kc_ref[0:BAND, sl]], axis=0)
                vv = jnp.concatenate([vp_ref[:, sl], vc_ref[0:BAND, sl]], axis=0)
            else:
                kk = kc_ref[(t - 1) * BAND:(t + 1) * BAND, sl]
                vv = vc_ref[(t - 1) * BAND:(t + 1) * BAND, sl]
            zero = jnp.zeros_like(q2)
            qm = jnp.concatenate([jnp.where(head_a, q2, zero), jnp.where(head_a, zero, q2)], axis=0)
            s = lax.dot_general(qm, kk, (((1,), (1,)), ((), ())), preferred_element_type=F32) + b
            m = jnp.max(s, axis=1, keepdims=True)
            p = jnp.exp(s - m)
            l = jnp.sum(p, axis=1, keepdims=True)
            o = _dot((p / l).astype(BF16), vv)
            lse = m + jnp.log(l)
            o_ref[rows, sl] = jnp.where(head_a, o[:BAND], o[BAND:])
            l_ref[rows, sl] = jnp.where(head_a, lse[:BAND], lse[BAND:])


def _attn_prompt(q, k, v):
    dil, n, _ = q.shape
    qb = min(Q_BLOCK, n)
    cur = lambda r, j: (r, j, 0)
    prev = lambda r, j: (r, jnp.maximum(j * (qb // BAND) - 1, 0), 0)
    blk = (None, qb, GROUP_W)
    hist = (None, BAND, GROUP_W)
    return pl.pallas_call(
        _attn_prompt_kernel,
        grid=(dil, n // qb),
        in_specs=[pl.BlockSpec(blk, cur), pl.BlockSpec(hist, prev), pl.BlockSpec(blk, cur),
                  pl.BlockSpec(hist, prev), pl.BlockSpec(blk, cur)],
        out_specs=[pl.BlockSpec(blk, cur), pl.BlockSpec(blk, cur)],
        out_shape=[jax.ShapeDtypeStruct((dil, n, GROUP_W), F32)] * 2,
        compiler_params=_params(2, 32), name=f"attn_prompt_d{dil}",
    )(q, k, k, v, v)


def _attn_sample_kernel(c_ref, q_ref, k_ref, v_ref, out_ref, o_ref, l_ref, *, dil, lc, bb):
    lane_b = lax.broadcasted_iota(jnp.int32, (GROUP_W, SEQ_TAB), 1)
    lane_h = lax.broadcasted_iota(jnp.int32, (HEADS, SEQ_TAB), 1)
    last = lax.broadcasted_iota(jnp.int32, (GROUP_W, LANES), 1) == LANES - 1
    hpos = lax.broadcasted_iota(jnp.int32, (HEADS, lc), 1)
    ok = (hpos & (dil - 1)) == 0
    base = (pl.program_id(0) & (SEQ_TAB // bb - 1)) * bb

    @pl.when(base == 0)
    def _():
        o_ref[...] = jnp.zeros_like(o_ref)
        l_ref[...] = jnp.zeros_like(l_ref)

    o_acc = o_ref[...]
    l_acc = l_ref[...]

    def col(ref, jb):
        if bb == SEQ_TAB:
            return ref[:, jb:jb + 1]
        return jnp.sum(jnp.where(lane_b == base + jb, ref[...], 0.0), axis=1, keepdims=True)

    def shift_into(row0, cin, new_col):
        rolled = pltpu.roll(cin, lc - 1, 1)
        out_ref[row0:row0 + GROUP_W, :] = rolled
        out_ref[row0:row0 + GROUP_W, lc - LANES:] = jnp.where(last, new_col, rolled[:, lc - LANES:])

    for jb in range(bb):
        qc, kc, vc = col(q_ref, jb), col(k_ref, jb), col(v_ref, jb)
        r0 = jb * 2 * GROUP_W
        kin = c_ref[r0:r0 + GROUP_W, :]
        shift_into(r0, kin, kc)
        s = jnp.sum((kin * qc).reshape(HEADS, HEAD_DIM, lc), axis=1)
        s_new = jnp.sum((kc * qc).reshape(HEADS, HEAD_DIM, 1), axis=1)
        s = jnp.where(ok, s, NEG)
        m = jnp.maximum(jnp.max(s, axis=1, keepdims=True), s_new)
        p = jnp.where(ok, jnp.exp(s - m), 0.0)
        pn = jnp.exp(s_new - m)
        l = jnp.sum(p, axis=1, keepdims=True) + pn
        p = p / l
        pn = pn / l
        vin = c_ref[r0 + GROUP_W:r0 + 2 * GROUP_W, :]
        shift_into(r0 + GROUP_W, vin, vc)
        pe = jnp.broadcast_to(p[:, None, :], (HEADS, HEAD_DIM, lc)).reshape(GROUP_W, lc)
        pne = jnp.broadcast_to(pn[:, None, :], (HEADS, HEAD_DIM, 1)).reshape(GROUP_W, 1)
        o = jnp.sum(vin * pe, axis=1, keepdims=True) + vc * pne
        o_acc = jnp.where(lane_b == base + jb, o, o_acc)
        l_acc = jnp.where(lane_h == base + jb, m + jnp.log(l), l_acc)
    o_ref[...] = o_acc
    l_ref[...] = l_acc


def _attn_sample(cache_t, q, k_new, v_new, dil, bb):
    lc = cache_t.shape[1]
    nb = q.shape[0]
    ntab = nb // SEQ_TAB
    per_tab = SEQ_TAB // bb
    cols = lambda t: t.astype(F32).reshape(ntab, SEQ_TAB, GROUP_W).transpose(0, 2, 1)
    kern = functools.partial(_attn_sample_kernel, dil=dil, lc=lc, bb=bb)
    blk = pl.BlockSpec((bb * 2 * GROUP_W, lc), lambda i: (i, 0))
    tab = pl.BlockSpec((None, GROUP_W, SEQ_TAB), lambda i: (i // per_tab, 0, 0))
    new_t, o, lse = pl.pallas_call(
        kern,
        grid=(nb // bb,),
        in_specs=[blk, tab, tab, tab],
        out_specs=[blk, tab, pl.BlockSpec((None, HEADS, SEQ_TAB), lambda i: (i // per_tab, 0, 0))],
        out_shape=[jax.ShapeDtypeStruct(cache_t.shape, F32),
                   jax.ShapeDtypeStruct((ntab, GROUP_W, SEQ_TAB), F32),
                   jax.ShapeDtypeStruct((ntab, HEADS, SEQ_TAB), F32)],
        compiler_params=_params(1, 56), name=f"attn_sample_d{dil}",
    )(cache_t, cols(q), cols(k_new), cols(v_new))
    return (new_t, o.transpose(0, 2, 1).reshape(nb, GROUP_W), lse.transpose(0, 2, 1).reshape(nb, HEADS))


def _ssm_discretise(lam_re, lam_im, log_dt, b_re, b_im, c_re, c_im):
    lam = lax.complex(lam_re.astype(F32), lam_im.astype(F32))
    dt = jnp.exp(log_dt.astype(F32))[:, None]
    ldt = lam * dt
    lam_bar = jnp.exp(ldt)
    b_bar = ((lam_bar - 1.0) / lam)[..., None] * lax.complex(b_re.astype(F32), b_im.astype(F32))
    c = lax.complex(c_re.astype(F32), c_im.astype(F32))
    return ldt, lam_bar, b_bar, c


def _ssm_tables(ssm_p):
    ldt, _, b_bar, c = _ssm_discretise(*ssm_p)
    t, q, gq, pr = SSM_T, SSM_Q, SSM_GQ, SSM_PAIRS
    steps = jnp.arange(t + 1, dtype=F32)
    pw = jnp.exp(ldt[None] * steps[:, None, None])
    hi = lax.Precision.HIGHEST
    eye = jnp.eye(gq, dtype=F32)
    kern = jnp.einsum('gqn,kgn,gnp->gkpq', c, pw[:t], b_bar, precision=hi).real
    kern = kern.reshape(q, gq, t, SSM_P, SSM_P)
    d_, a_, b_ = jnp.meshgrid(jnp.arange(pr), jnp.arange(2), jnp.arange(2), indexing='ij')
    lag = 2 * d_ + b_ - a_
    kl = jnp.where((lag >= 0)[None, None, :, :, :, None, None], kern[:, :, jnp.maximum(lag, 0)], 0.0)
    toep = jnp.einsum('qgdabpr,gh->qdagpbhr', kl, eye).reshape(q, pr, 2 * LANES, 2 * LANES)
    win = pw[:t][::-1][:, :, :, None] * b_bar[None]
    win = win.reshape(pr, 2, q, gq, SSM_N, SSM_P).transpose(2, 0, 1, 3, 5, 4)
    w_in = jnp.concatenate([win.real, win.real, win.imag, win.imag], axis=-1)
    w_in = w_in.reshape(q, pr, 2 * LANES, 2 * LANES)
    cl = c[None] * pw[1:, :, None, :]
    cl = cl.reshape(pr, 2, q, gq, SSM_P, SSM_N).transpose(2, 0, 1, 3, 4, 5)
    w_out = jnp.concatenate([cl.real, cl.real, -cl.imag, -cl.imag], axis=-1)
    w_out = w_out.reshape(q, pr, 2 * LANES, 2 * LANES)
    return toep.astype(BF16), w_in.astype(BF16), w_out.astype(BF16), ldt


def _ssm_prompt_kernel(u_ref, toep_ref, win_ref, wout_ref, lp_ref, y_ref, hl_ref, carry_ref, win_bd,
                       wout_bd, *, nsteps):
    tt = pl.program_id(1)
    nc = u_ref.shape[0] // SSM_T
    sq = SSM_SQ

    @pl.when(tt == 0)
    def _():
        carry_ref[...] = jnp.zeros_like(carry_ref)
        r = lax.broadcasted_iota(jnp.int32, (2 * LANES, 2 * sq), 0)
        col = lax.broadcasted_iota(jnp.int32, (2 * LANES, 2 * sq), 1)
        own = (((r & (LANES - 1)) >> int(math.log2(SSM_P)))
               == ((col & (sq - 1)) >> int(math.log2(SSM_N))))
        for i in range(SSM_PAIRS):
            for src, dst in ((win_ref, win_bd), (wout_ref, wout_bd)):
                c = src[i]
                full = jnp.concatenate([c[:, :LANES]] * (sq // LANES) + [c[:, LANES:]] * (sq // LANES), axis=1)
                dst[i] = jnp.where(own, full, jnp.zeros_like(full))

    step = lambda t: u_ref[pl.ds(t, nc, stride=SSM_T), :].astype(BF16)
    a = [jnp.concatenate([step(2 * i), step(2 * i + 1)], axis=1) for i in range(SSM_PAIRS)]
    s = _dot(a[0], win_bd[0])
    for i in range(1, SSM_PAIRS):
        s = s + _dot(a[i], win_bd[i])
    xr, xi = s[:, :sq], s[:, sq:]
    row = lax.broadcasted_iota(jnp.int32, (nc, sq), 0)
    cr, ci = carry_ref[7:8, :sq], carry_ref[7:8, sq:]
    lr, li = lp_ref[0:1, :sq], lp_ref[0:1, sq:]
    xr = xr + jnp.where(row == 0, lr * cr - li * ci, 0.0)
    xi = xi + jnp.where(row == 0, lr * ci + li * cr, 0.0)

    def down(v, d):
        return jnp.where(row >= d, pltpu.roll(v, d, 0), 0.0)

    for k in range(nsteps):
        d = 1 << k
        lr, li = lp_ref[k:k + 1, :sq], lp_ref[k:k + 1, sq:]
        sr, si = down(xr, d), down(xi, d)
        xr, xi = xr + lr * sr - li * si, xi + lr * si + li * sr
    hr = jnp.where(row == 0, cr, pltpu.roll(xr, 1, 0))
    hi = jnp.where(row == 0, ci, pltpu.roll(xi, 1, 0))
    h_in = jnp.concatenate([hr, hi], axis=1).astype(BF16)
    last = jnp.concatenate([xr[nc - 8:], xi[nc - 8:]], axis=1)
    carry_ref[...] = last
    hl_ref[...] = last
    for j in range(SSM_PAIRS):
        acc = lax.dot_general(h_in, wout_bd[j], (((1,), (1,)), ((), ())), preferred_element_type=F32)
        for i in range(j + 1):
            acc = acc + _dot(a[i], toep_ref[j - i])
        y_ref[pl.ds(2 * j, nc, stride=SSM_T), :] = acc[:, :LANES]
        y_ref[pl.ds(2 * j + 1, nc, stride=SSM_T), :] = acc[:, LANES:]


def _ssm_prompt(u, tables):
    toep, w_in, w_out, ldt = tables
    s = u.shape[0]
    tile = min(s, 8192)
    nc = tile // SSM_T
    nsteps = int(math.log2(nc))
    scale = (SSM_T * (2.0 ** jnp.arange(nsteps, dtype=F32)))[:, None, None]
    lp = jnp.exp(ldt[None] * scale)
    lp = jnp.concatenate([lp.real.reshape(nsteps, SSM_Q, SSM_SQ), lp.imag.reshape(nsteps, SSM_Q, SSM_SQ)],
                         axis=2).transpose(1, 0, 2)
    per_q = lambda *shape: pl.BlockSpec((None,) + shape, lambda q, t: (q,) + (0,) * len(shape))
    y, hl = pl.pallas_call(
        functools.partial(_ssm_prompt_kernel, nsteps=nsteps),
        grid=(SSM_Q, s // tile),
        in_specs=[pl.BlockSpec((tile, LANES), lambda q, t: (t, q)),
                  per_q(SSM_PAIRS, 2 * LANES, 2 * LANES), per_q(SSM_PAIRS, 2 * LANES, 2 * LANES),
                  per_q(SSM_PAIRS, 2 * LANES, 2 * LANES), per_q(nsteps, 2 * SSM_SQ)],
        out_specs=[pl.BlockSpec((tile, LANES), lambda q, t: (t, q)), per_q(8, 2 * SSM_SQ)],
        out_shape=[jax.ShapeDtypeStruct((s, SSM_WIDTH), F32),
                   jax.ShapeDtypeStruct((SSM_Q, 8, 2 * SSM_SQ), F32)],
        scratch_shapes=[pltpu.VMEM((8, 2 * SSM_SQ), F32),
                        pltpu.VMEM((SSM_PAIRS, 2 * LANES, 2 * SSM_SQ), BF16),
                        pltpu.VMEM((SSM_PAIRS, 2 * LANES, 2 * SSM_SQ), BF16)],
        compiler_params=_params(2, 56), name="ssm_prompt",
    )(u, toep, w_in, w_out, lp)
    hl = hl[:, 7, :]
    return y, hl[:, :SSM_SQ].reshape(SSM_G, SSM_N), hl[:, SSM_SQ:].reshape(SSM_G, SSM_N)


def _ssm_sample_kernel(u_ref, hr_ref, hi_ref, lr_ref, li_ref, br_ref, bi_ref, cr_ref, ci_ref,
                       y_ref, or_ref, oi_ref):
    u = u_ref[...].astype(BF16)
    hr, hi = hr_ref[...], hi_ref[...]
    lr, li = lr_ref[...], li_ref[...]
    nr = lr * hr - li * hi + _dot(u, br_ref[...])
    ni = lr * hi + li * hr + _dot(u, bi_ref[...])
    or_ref[...] = nr
    oi_ref[...] = ni
    y_ref[...] = _dot(nr.astype(BF16), cr_ref[...]) + _dot(ni.astype(BF16), ci_ref[...])


def _ssm_sample(u, h_re, h_im, ssm_p):
    nb = u.shape[0]
    _, lam_bar, b_bar, c = _ssm_discretise(*ssm_p)
    eye = jnp.eye(SSM_G, dtype=F32)
    gn = SSM_G * SSM_N
    bd = lambda part, eq, shape: jnp.einsum(eq, part, eye).reshape(shape).astype(BF16)
    args = (u, h_re, h_im, lam_bar.real.reshape(1, gn), lam_bar.imag.reshape(1, gn),
            bd(b_bar.real, 'gnp,gh->gphn', (SSM_WIDTH, gn)), bd(b_bar.imag, 'gnp,gh->gphn', (SSM_WIDTH, gn)),
            bd(c.real, 'gpn,gh->gnhp', (gn, SSM_WIDTH)), bd(-c.imag, 'gpn,gh->gnhp', (gn, SSM_WIDTH)))
    return pl.pallas_call(
        _ssm_sample_kernel,
        grid=(1,),
        in_specs=[_const_spec(a.shape) for a in args],
        out_specs=[_const_spec((nb, SSM_WIDTH)), _const_spec((nb, gn)), _const_spec((nb, gn))],
        out_shape=[jax.ShapeDtypeStruct((nb, SSM_WIDTH), F32),
                   jax.ShapeDtypeStruct((nb, gn), F32), jax.ShapeDtypeStruct((nb, gn), F32)],
        compiler_params=_params(1, 32), name="ssm_sample",
    )(*args)


def _post_kernel(x_ref, o1_ref, o2_ref, o3_ref, l1_ref, l2_ref, l3_ref, y_ref, u_ref, gs_ref, ga_ref,
                 d_ref, wglu_ref, bglu_ref, wbs_ref, wba_ref, wout_ref, nm_ref, wup_ref, wdn_ref,
                 nf_ref, out_ref, *scr, dils):
    tm = x_ref.shape[0]

    def natural(ref, dil, buf):
        if dil == 1:
            return ref[0]
        for r in range(dil):
            for t in range(N_TILES):
                buf[t, pl.ds(r, tm // dil, stride=dil), :] = ref[r, :, t * LANES:(t + 1) * LANES]
        return jnp.concatenate([buf[t] for t in range(N_TILES)], axis=1)

    o1, o2, o3 = (natural(r, d, scr[k]) for k, (r, d) in enumerate(zip((o1_ref, o2_ref, o3_ref), dils)))
    l1, l2, l3 = (natural(r, d, scr[3 + k]) for k, (r, d) in enumerate(zip((l1_ref, l2_ref, l3_ref), dils)))
    m = jnp.maximum(jnp.maximum(l1, l2), l3)
    e1, e2, e3 = jnp.exp(l1 - m), jnp.exp(l2 - m), jnp.exp(l3 - m)
    den = e1 + e2 + e3
    attn = (e1 / den) * o1 + (e2 / den) * o2 + (e3 / den) * o3
    y = jax.nn.gelu(y_ref[...] + d_ref[...] * u_ref[...])
    ssm_y = y * _sigmoid(_dot(y.astype(BF16), wglu_ref[...]) + bglu_ref[...])
    z = (_sigmoid(gs_ref[...]) * _dot(ssm_y.astype(BF16), wbs_ref[...])
         + _sigmoid(ga_ref[...]) * _dot(attn.astype(BF16), wba_ref[...]))
    x1 = x_ref[...] + _dot(z.astype(BF16), wout_ref[...])
    hm = _rms(x1, nm_ref[...]).astype(BF16)
    up = jnp.maximum(_dot(hm, wup_ref[...]), 0.0)
    x2 = x1 + _dot((up * up).astype(BF16), wdn_ref[...])
    out_ref[...] = _rms(x2, nf_ref[...])


def _post(x, os_, ls_, y_ssm, u, gs, ga, consts, tm):
    s = x.shape[0]
    row = lambda i: (i, 0)
    res = lambda i: (0, i, 0)
    dils = tuple(o.shape[0] for o in os_)
    grouped = [*os_, *ls_]
    flat = [y_ssm, u, gs, ga]
    return pl.pallas_call(
        functools.partial(_post_kernel, dils=dils),
        grid=(s // tm,),
        in_specs=([pl.BlockSpec((tm, D_MODEL), row)]
                  + [pl.BlockSpec((a.shape[0], tm // a.shape[0], GROUP_W), res) for a in grouped]
                  + [pl.BlockSpec((tm, a.shape[1]), row) for a in flat]
                  + [_const_spec(c.shape) for c in consts]),
        out_specs=pl.BlockSpec((tm, D_MODEL), row),
        out_shape=jax.ShapeDtypeStruct((s, D_MODEL), F32),
        scratch_shapes=[pltpu.VMEM((N_TILES, tm, LANES), F32)] * 6,
        compiler_params=_params(1, 56), name="post",
    )(x, *grouped, *flat, *consts)


def kernel(x_prompt, x_sample, cache_kv_g1, cache_kv_g2, cache_kv_g3, state_ssm, norm_mix, w_in,
           ssm_lambda_re, ssm_lambda_im, ssm_log_dt, ssm_b_re, ssm_b_im, ssm_c_re, ssm_c_im, ssm_d,
           w_glu, b_glu, w_branch_ssm, w_branch_attn, w_out, norm_mlp, w_up, w_down, norm_final):
    depth = w_in.shape[0]
    assert depth == 1, "one decoder layer"
    seq = x_prompt.shape[1]
    nb = x_sample.shape[0]
    assert x_prompt.shape[0] == 1 and x_sample.shape[1] == 1
    caches = (cache_kv_g1, cache_kv_g2, cache_kv_g3)
    windows = tuple(w for w, _ in ATTN_GROUPS)
    dils = tuple(d for _, d in ATTN_GROUPS)
    for c, w in zip(caches, windows):
        assert c.shape[2] == w, "cache holds exactly one window"

    w_in_bf = w_in[0].astype(BF16)
    gain_mix = norm_mix[0][None, :]
    consts = (ssm_d[0][None, :], w_glu[0].astype(BF16), b_glu[0][None, :],
              w_branch_ssm[0].astype(BF16), w_branch_attn[0].astype(BF16), w_out[0].astype(BF16),
              norm_mlp[0][None, :], w_up[0].astype(BF16), w_down[0].astype(BF16), norm_final[None, :])
    ssm_p = (ssm_lambda_re[0], ssm_lambda_im[0], ssm_log_dt[0], ssm_b_re[0], ssm_b_im[0],
             ssm_c_re[0], ssm_c_im[0])

    xp = x_prompt[0]
    tm_p = 256
    res = _inproj(xp, gain_mix, w_in_bf, jnp.arange(seq // tm_p) * tm_p, jnp.arange(tm_p), tm_p,
                  windows, dils)
    qs, ks, vs, tails = res[0:3], res[3:6], res[6:9], res[9:12]
    u_p, gs_p, ga_p = res[12:15]
    os_p, ls_p = zip(*[_attn_prompt(qs[g], ks[g], vs[g]) for g in range(N_GROUPS)])
    y_p, hp_re, hp_im = _ssm_prompt(u_p, _ssm_tables(ssm_p))
    y_prompt = _post(xp, os_p, ls_p, y_p, u_p, gs_p, ga_p, consts, tm_p)[None]
    kv_prompt = [t.reshape(1, 1, w, 2, HEADS, HEAD_DIM) for t, w in zip(tails, windows)]
    ssm_prompt = jnp.stack([hp_re, hp_im], axis=-1)[None, None].astype(state_ssm.dtype)

    xs = x_sample[:, 0]
    res = _inproj(xs, gain_mix, w_in_bf, jnp.full((1,), PAST_LEN), jnp.zeros((nb,)), nb,
                  (nb,) * N_GROUPS, (1,) * N_GROUPS)
    qs, tails = res[0:3], res[9:12]
    u_s, gs_s, ga_s = res[12:15]
    os_s, ls_s, kv_sample = [], [], []
    for g, (window, dil) in enumerate(ATTN_GROUPS):
        lc = caches[g].shape[2]
        cache_t = caches[g][0].transpose(0, 2, 3, 4, 1).reshape(nb * 2 * GROUP_W, lc)
        bb = max(1, min(8, 1024 // lc))
        new_t, o, lse = _attn_sample(cache_t, qs[g][0], tails[g][:, :GROUP_W], tails[g][:, GROUP_W:],
                                     dil, bb)
        kv_sample.append(new_t.reshape(nb, 2, HEADS, HEAD_DIM, lc).transpose(0, 4, 1, 2, 3)[None])
        os_s.append(o[None])
        ls_s.append(jnp.repeat(lse, HEAD_DIM, axis=1)[None])
    st = state_ssm[0].astype(F32)
    gn = SSM_G * SSM_N
    y_s, hs_re, hs_im = _ssm_sample(u_s, st[..., 0].reshape(nb, gn), st[..., 1].reshape(nb, gn), ssm_p)
    y_sample = _post(xs, os_s, ls_s, y_s, u_s, gs_s, ga_s, consts, nb)[:, None]
    ssm_sample = jnp.stack([hs_re.reshape(nb, SSM_G, SSM_N), hs_im.reshape(nb, SSM_G, SSM_N)],
                           axis=-1)[None].astype(state_ssm.dtype)

    return (y_prompt, y_sample, kv_prompt[0], kv_prompt[1], kv_prompt[2], ssm_prompt,
            kv_sample[0], kv_sample[1], kv_sample[2], ssm_sample)
```

```python
import functools
import math
from typing import NamedTuple

import jax
import jax.numpy as jnp
from jax import lax
from jax.experimental import pallas as pl
from jax.experimental.pallas import tpu as pltpu

F32 = jnp.float32
BF16 = jnp.bfloat16

D_MODEL = 1024
HEAD_DIM = 64
HEADS = 8
GROUP_W = HEADS * HEAD_DIM
ATTN_GROUPS = ((128, 1), (512, 4), (2048, 16))
N_GROUPS = len(ATTN_GROUPS)
QKV_W = N_GROUPS * GROUP_W
BAND = 128
Q_BLOCK = 512
SEQ_TAB = 8
ROT_DIM = HEAD_DIM // 4
ROT_HALF = ROT_DIM // 2
ROPE_THETA = 500000.0
PAST_LEN = 8192
SSM_WIDTH = 512
SSM_P = 16
SSM_G = SSM_WIDTH // SSM_P
SSM_N = 64
SSM_T = 16
EPS = 1e-6
NEG = -1e30
LANES = 128
N_TILES = GROUP_W // LANES
MIB = 1024 * 1024

SSM_Q = SSM_WIDTH // LANES
SSM_GQ = SSM_G // SSM_Q
SSM_SQ = SSM_GQ * SSM_N
SSM_PAIRS = SSM_T // 2

OFF_Q, OFF_K, OFF_V = 0, QKV_W, 2 * QKV_W
OFF_U = 3 * QKV_W
OFF_GS = OFF_U + SSM_WIDTH
OFF_GA = OFF_GS + D_MODEL
IN_W = OFF_GA + D_MODEL


def _const_spec(shape):
    nd = len(shape)
    return pl.BlockSpec(shape, lambda *_: (0,) * nd, pipeline_mode=pl.Buffered(1))


def _params(n_axes, vmem_mib):
    return pltpu.CompilerParams(dimension_semantics=("arbitrary",) * n_axes,
                                vmem_limit_bytes=vmem_mib * MIB)


def _sigmoid(x):
    return 1.0 / (1.0 + jnp.exp(-x))


def _rms(x, g):
    return x * lax.rsqrt(jnp.mean(x * x, axis=-1, keepdims=True) + EPS) * g


def _dot(a, b):
    return jnp.dot(a, b, preferred_element_type=F32)


def _phase_runner(n_phase):
    step = pl.program_id(0)
    if n_phase == 1:
        return step, lambda ph: (lambda fn: fn())
    phase = step & (n_phase - 1)
    return lax.shift_right_logical(step, int(math.log2(n_phase))), lambda ph: pl.when(phase == ph)


def _inproj_kernel(*refs, tm, dils, tail_rows, first_tail, n_phase, jobs):
    n_in = 7 + 4 * len(jobs)
    x_ref, g_ref, w_ref, ca_ref, sa_ref, cb_ref, sb_ref = refs[:7]
    outs = refs[n_in:n_in + 15]
    q_refs, k_refs, v_refs, tail_refs = outs[0:3], outs[3:6], outs[6:9], outs[9:12]
    u_ref, gs_ref, ga_ref = outs[12:15]
    n_out = 15 + 3 * len(jobs)
    scr, h_scr, rot_scr = refs[n_in + n_out:n_in + n_out + 3]
    i, in_phase = _phase_runner(n_phase)

    def rot(a):
        cos, s1, s2 = rot_scr[0], rot_scr[1], rot_scr[2]
        parts = []
        for j in range(GROUP_W // LANES):
            seg = a[:, j * LANES:(j + 1) * LANES]
            parts.append(seg * cos + pltpu.roll(seg, LANES - ROT_HALF, 1) * s1
                         + pltpu.roll(seg, ROT_HALF, 1) * s2)
        return jnp.concatenate(parts, axis=1)

    def put(out_ref, val, dil):
        if dil == 1:
            out_ref[0] = val.astype(BF16)
            return
        for t in range(N_TILES):
            scr[t] = val[:, t * LANES:(t + 1) * LANES]
        for r in range(dil):
            rows = [scr[t, pl.ds(r, tm // dil, stride=dil), :] for t in range(N_TILES)]
            out_ref[r] = jnp.concatenate(rows, axis=1).astype(BF16)

    def proj(off, gi):
        c0 = off + gi * GROUP_W
        return _dot(h_scr[...], w_ref[:, c0:c0 + GROUP_W])

    def put_tail(gi, half, val):
        tb = tail_rows[gi]

        @pl.when(i >= first_tail[gi])
        def _():
            tail_refs[gi][:, half * GROUP_W:(half + 1) * GROUP_W] = val[tm - tb:, :]

    @in_phase(0)
    def _():
        h_scr[...] = _rms(x_ref[...], g_ref[...]).astype(BF16)
        ca, sa = ca_ref[pl.ds(i, 1), :], sa_ref[pl.ds(i, 1), :]
        cb, sb = cb_ref[...], sb_ref[...]
        cos_f = ca * cb - sa * sb
        sin_f = sa * cb + ca * sb
        hd = lax.broadcasted_iota(jnp.int32, (tm, LANES), 1) & (HEAD_DIM - 1)
        rot_scr[0] = jnp.where(hd < ROT_DIM, cos_f, 1.0)
        rot_scr[1] = jnp.where(hd < ROT_HALF, -sin_f, 0.0)
        rot_scr[2] = jnp.where((hd >= ROT_HALF) & (hd < ROT_DIM), sin_f, 0.0)
        for gi in range(N_GROUPS):
            put(q_refs[gi], rot(proj(OFF_Q, gi)) * (HEAD_DIM ** -0.5), dils[gi])

    @in_phase(1 % n_phase)
    def _():
        for gi in range(N_GROUPS):
            ka = rot(proj(OFF_K, gi))
            put(k_refs[gi], ka, dils[gi])
            put_tail(gi, 0, ka)

    @in_phase(2 % n_phase)
    def _():
        for gi in range(N_GROUPS):
            va = proj(OFF_V, gi)
            put(v_refs[gi], va, dils[gi])
            put_tail(gi, 1, va)

    @in_phase(3 % n_phase)
    def _():
        h = h_scr[...]
        u_ref[...] = _dot(h, w_ref[:, OFF_U:OFF_GS])
        gs_ref[...] = _dot(h, w_ref[:, OFF_GS:OFF_GA])
        ga_ref[...] = _dot(h, w_ref[:, OFF_GA:IN_W])

    _run_cache_jobs(jobs, refs[7:n_in], refs[n_in + 15:n_in + n_out], refs[n_in + n_out + 3:])


def _inproj(x, gain, w_bf, tile_pos, row_pos, tm, windows, dils, n_phase=1, jobs=(), job_args=()):
    s = x.shape[0]
    ntiles = s // tm
    inv_freq = jnp.exp(-math.log(ROPE_THETA) * jnp.arange(ROT_HALF, dtype=F32) * (2.0 / ROT_DIM))
    lane_freq = jnp.tile(inv_freq, LANES // ROT_HALF)[None, :]
    ang_a = tile_pos.astype(F32)[:, None] * lane_freq
    ang_b = row_pos.astype(F32)[:, None] * lane_freq
    tail_rows = tuple(min(w, tm) for w in windows)
    first_tail = tuple(ntiles - w // tb for w, tb in zip(windows, tail_rows))
    row = lambda st: (st // n_phase, 0)
    res = lambda st: (0, st // n_phase, 0)
    qkv_shapes = [jax.ShapeDtypeStruct((d, s // d, GROUP_W), BF16) for d in dils]
    qkv_specs = [pl.BlockSpec((d, tm // d, GROUP_W), res) for d in dils]
    out_shape = (qkv_shapes * 3
                 + [jax.ShapeDtypeStruct((w, 2 * GROUP_W), F32) for w in windows]
                 + [jax.ShapeDtypeStruct((s, SSM_WIDTH), F32),
                    jax.ShapeDtypeStruct((s, D_MODEL), F32),
                    jax.ShapeDtypeStruct((s, D_MODEL), F32)])
    out_specs = (qkv_specs * 3
                 + [pl.BlockSpec((tb, 2 * GROUP_W), functools.partial(
                     lambda st, ft: (jnp.maximum(st // n_phase - ft, 0), 0), ft=ft))
                    for tb, ft in zip(tail_rows, first_tail)]
                 + [pl.BlockSpec((tm, SSM_WIDTH), row),
                    pl.BlockSpec((tm, D_MODEL), row),
                    pl.BlockSpec((tm, D_MODEL), row)])
    nb = job_args[0][1].shape[0] if jobs else 0
    job_in, job_out, job_shape, job_scr = _cache_job_specs(jobs, nb, ntiles * n_phase)
    kern = functools.partial(_inproj_kernel, tm=tm, dils=dils, tail_rows=tail_rows,
                             first_tail=first_tail, n_phase=n_phase, jobs=jobs)
    return pl.pallas_call(
        kern,
        grid=(ntiles * n_phase,),
        in_specs=[pl.BlockSpec((tm, D_MODEL), row), _const_spec((1, D_MODEL)),
                  _const_spec((D_MODEL, IN_W)),
                  _const_spec((ntiles, LANES)), _const_spec((ntiles, LANES)),
                  _const_spec((tm, LANES)), _const_spec((tm, LANES))] + job_in,
        out_specs=out_specs + job_out, out_shape=out_shape + job_shape,
        scratch_shapes=[pltpu.VMEM((N_TILES, tm, LANES), F32), pltpu.VMEM((tm, D_MODEL), BF16),
                        pltpu.VMEM((3, tm, LANES), F32)] + job_scr,
        compiler_params=_params(1, 58), name="inproj",
    )(x, gain, w_bf, jnp.cos(ang_a), jnp.sin(ang_a), jnp.cos(ang_b), jnp.sin(ang_b),
      *[a for args in job_args for a in _cache_job_operands(*args)])


def _attn_prompt_kernel(q_ref, kp_ref, kc_ref, vp_ref, vc_ref, o_ref, l_ref):
    j = pl.program_id(1)
    lane = lax.broadcasted_iota(jnp.int32, (BAND, LANES), 1)
    head_a = lane < HEAD_DIM
    qi = lax.broadcasted_iota(jnp.int32, (2 * BAND, 2 * BAND), 0) & (BAND - 1)
    kj = lax.broadcasted_iota(jnp.int32, (2 * BAND, 2 * BAND), 1)
    band = (kj >= qi) & (kj <= qi + BAND)
    bias = jnp.where(band, 0.0, NEG)
    bias_first = jnp.where(band & (kj >= BAND), 0.0, NEG)
    for t in range(q_ref.shape[0] // BAND):
        rows = slice(t * BAND, (t + 1) * BAND)
        b = bias if t > 0 else jnp.where(j > 0, bias, bias_first)
        for hp in range(N_TILES):
            sl = slice(hp * LANES, (hp + 1) * LANES)
            q2 = q_ref[rows, sl]
            if t == 0:
                kk = jnp.concatenate([kp_ref[:, sl], kc_ref[0:BAND, sl]], axis=0)
                vv = jnp.concatenate([vp_ref[:, sl], vc_ref[0:BAND, sl]], axis=0)
            else:
                kk = kc_ref[(t - 1) * BAND:(t + 1) * BAND, sl]
                vv = vc_ref[(t - 1) * BAND:(t + 1) * BAND, sl]
            zero = jnp.zeros_like(q2)
            qm = jnp.concatenate([jnp.where(head_a, q2, zero), jnp.where(head_a, zero, q2)], axis=0)
            s = lax.dot_general(qm, kk, (((1,), (1,)), ((), ())), preferred_element_type=F32) + b
            m = jnp.max(s, axis=1, keepdims=True)
            p = jnp.exp(s - m)
            l = jnp.sum(p, axis=1, keepdims=True)
            o = _dot((p / l).astype(BF16), vv)
            lse = m + jnp.log(l)
            o_ref[rows, sl] = jnp.where(head_a, o[:BAND], o[BAND:])
            l_ref[rows, sl] = jnp.where(head_a, lse[:BAND], lse[BAND:])


def _attn_prompt(q, k, v):
    dil, n, _ = q.shape
    qb = min(Q_BLOCK, n)
    cur = lambda r, j: (r, j, 0)
    prev = lambda r, j: (r, jnp.maximum(j * (qb // BAND) - 1, 0), 0)
    blk = (None, qb, GROUP_W)
    hist = (None, BAND, GROUP_W)
    return pl.pallas_call(
        _attn_prompt_kernel,
        grid=(dil, n // qb),
        in_specs=[pl.BlockSpec(blk, cur), pl.BlockSpec(hist, prev), pl.BlockSpec(blk, cur),
                  pl.BlockSpec(hist, prev), pl.BlockSpec(blk, cur)],
        out_specs=[pl.BlockSpec(blk, cur), pl.BlockSpec(blk, cur)],
        out_shape=[jax.ShapeDtypeStruct((dil, n, GROUP_W), F32)] * 2,
        compiler_params=_params(2, 32), name=f"attn_prompt_d{dil}",
    )(q, k, k, v, v)


class _CacheJob(NamedTuple):
    dil: int
    lc: int
    halves: bool


def _cache_step(job, c_ref, q_ref, k_ref, v_ref, out_ref, o_ref, l_ref, *scr):
    dil, lc = job.dil, job.lc
    step = pl.program_id(0)
    seq = lax.shift_right_logical(step, 1) if job.halves else step
    lane = seq & (SEQ_TAB - 1)
    lane_b = lax.broadcasted_iota(jnp.int32, (GROUP_W, SEQ_TAB), 1)
    lane_h = lax.broadcasted_iota(jnp.int32, (HEADS, SEQ_TAB), 1)
    last = lax.broadcasted_iota(jnp.int32, (GROUP_W, LANES), 1) == LANES - 1
    hpos = lax.broadcasted_iota(jnp.int32, (HEADS, lc), 1)
    ok = (hpos & (dil - 1)) == 0
    col = lambda ref: jnp.sum(jnp.where(lane_b == lane, ref[...], 0.0), axis=1, keepdims=True)
    is_k = (step & 1) == 0 if job.halves else True

    @pl.when((lane == 0) & is_k)
    def _():
        o_ref[...] = jnp.zeros_like(o_ref)
        l_ref[...] = jnp.zeros_like(l_ref)

    def shift_into(row0, cin, new_col):
        rolled = pltpu.roll(cin, lc - 1, 1)
        out_ref[row0:row0 + GROUP_W, :] = rolled
        out_ref[row0:row0 + GROUP_W, lc - LANES:] = jnp.where(last, new_col, rolled[:, lc - LANES:])

    def scores(row0):
        qc, kc = col(q_ref), col(k_ref)
        kin = c_ref[row0:row0 + GROUP_W, :]
        shift_into(row0, kin, kc)
        s = jnp.sum((kin * qc).reshape(HEADS, HEAD_DIM, lc), axis=1)
        s_new = jnp.sum((kc * qc).reshape(HEADS, HEAD_DIM, 1), axis=1)
        s = jnp.where(ok, s, NEG)
        m = jnp.maximum(jnp.max(s, axis=1, keepdims=True), s_new)
        p = jnp.where(ok, jnp.exp(s - m), 0.0)
        pn = jnp.exp(s_new - m)
        l = jnp.sum(p, axis=1, keepdims=True) + pn
        l_ref[...] = jnp.where(lane_h == lane, m + jnp.log(l), l_ref[...])
        return p / l, pn / l

    def attend(row0, p, pn):
        vc = col(v_ref)
        vin = c_ref[row0:row0 + GROUP_W, :]
        shift_into(row0, vin, vc)
        pe = jnp.broadcast_to(p[:, None, :], (HEADS, HEAD_DIM, lc)).reshape(GROUP_W, lc)
        pne = jnp.broadcast_to(pn[:, None, :], (HEADS, HEAD_DIM, 1)).reshape(GROUP_W, 1)
        o = jnp.sum(vin * pe, axis=1, keepdims=True) + vc * pne
        o_ref[...] = jnp.where(lane_b == lane, o, o_ref[...])

    if not job.halves:
        p, pn = scores(0)
        attend(GROUP_W, p, pn)
        return
    p_scr, pn_scr = scr

    @pl.when(is_k)
    def _():
        p, pn = scores(0)
        p_scr[...] = p
        pn_scr[...] = jnp.broadcast_to(pn, pn_scr.shape)

    @pl.when(jnp.logical_not(is_k))
    def _():
        attend(0, p_scr[...], pn_scr[:, 0:1])


def _run_cache_jobs(jobs, in_refs, out_refs, scr_refs):
    k = 0
    for n, job in enumerate(jobs):
        ns = 2 if job.halves else 0
        _cache_step(job, *in_refs[4 * n:4 * n + 4], *out_refs[3 * n:3 * n + 3], *scr_refs[k:k + ns])
        k += ns


def _cache_job_specs(jobs, nb, nsteps):
    in_specs, out_specs, out_shape, scratch = [], [], [], []
    for job in jobs:
        per_seq = 2 if job.halves else 1
        assert nsteps == nb * per_seq, "one cache unit per grid step"
        blk = pl.BlockSpec((2 * GROUP_W // per_seq, job.lc), lambda st: (st, 0))
        tab_idx = functools.partial(lambda st, n: (st // n, 0, 0), n=SEQ_TAB * per_seq)
        tab = pl.BlockSpec((None, GROUP_W, SEQ_TAB), tab_idx)
        in_specs += [blk, tab, tab, tab]
        out_specs += [blk, tab, pl.BlockSpec((None, HEADS, SEQ_TAB), tab_idx)]
        out_shape += [jax.ShapeDtypeStruct((nb * 2 * GROUP_W, job.lc), F32),
                      jax.ShapeDtypeStruct((nb // SEQ_TAB, GROUP_W, SEQ_TAB), F32),
                      jax.ShapeDtypeStruct((nb // SEQ_TAB, HEADS, SEQ_TAB), F32)]
        if job.halves:
            scratch += [pltpu.VMEM((HEADS, job.lc), F32), pltpu.VMEM((HEADS, LANES), F32)]
    return in_specs, out_specs, out_shape, scratch


def _cache_job_operands(cache_t, q, k_new, v_new):
    nb = q.shape[0]
    cols = lambda t: t.astype(F32).reshape(nb // SEQ_TAB, SEQ_TAB, GROUP_W).transpose(0, 2, 1)
    return cache_t, cols(q), cols(k_new), cols(v_new)


def _cache_job_results(new_t, o, lse):
    nb = o.shape[0] * SEQ_TAB
    return new_t, o.transpose(0, 2, 1).reshape(nb, GROUP_W), lse.transpose(0, 2, 1).reshape(nb, HEADS)


def _ssm_discretise(lam_re, lam_im, log_dt, b_re, b_im, c_re, c_im):
    lam = lax.complex(lam_re.astype(F32), lam_im.astype(F32))
    dt = jnp.exp(log_dt.astype(F32))[:, None]
    ldt = lam * dt
    lam_bar = jnp.exp(ldt)
    b_bar = ((lam_bar - 1.0) / lam)[..., None] * lax.complex(b_re.astype(F32), b_im.astype(F32))
    c = lax.complex(c_re.astype(F32), c_im.astype(F32))
    return ldt, lam_bar, b_bar, c


def _ssm_tables(ssm_p):
    ldt, _, b_bar, c = _ssm_discretise(*ssm_p)
    t, q, gq, pr = SSM_T, SSM_Q, SSM_GQ, SSM_PAIRS
    steps = jnp.arange(t + 1, dtype=F32)
    pw = jnp.exp(ldt[None] * steps[:, None, None])
    hi = lax.Precision.HIGHEST
    eye = jnp.eye(gq, dtype=F32)
    kern = jnp.einsum('gqn,kgn,gnp->gkpq', c, pw[:t], b_bar, precision=hi).real
    kern = kern.reshape(q, gq, t, SSM_P, SSM_P)
    d_, a_, b_ = jnp.meshgrid(jnp.arange(pr), jnp.arange(2), jnp.arange(2), indexing='ij')
    lag = 2 * d_ + b_ - a_
    kl = jnp.where((lag >= 0)[None, None, :, :, :, None, None], kern[:, :, jnp.maximum(lag, 0)], 0.0)
    toep = jnp.einsum('qgdabpr,gh->qdagpbhr', kl, eye).reshape(q, pr, 2 * LANES, 2 * LANES)
    win = pw[:t][::-1][:, :, :, None] * b_bar[None]
    win = win.reshape(pr, 2, q, gq, SSM_N, SSM_P).transpose(2, 0, 1, 3, 5, 4)
    w_in = jnp.concatenate([win.real, win.real, win.imag, win.imag], axis=-1)
    w_in = w_in.reshape(q, pr, 2 * LANES, 2 * LANES)
    cl = c[None] * pw[1:, :, None, :]
    cl = cl.reshape(pr, 2, q, gq, SSM_P, SSM_N).transpose(2, 0, 1, 3, 4, 5)
    w_out = jnp.concatenate([cl.real, cl.real, -cl.imag, -cl.imag], axis=-1)
    w_out = w_out.reshape(q, pr, 2 * LANES, 2 * LANES)
    return toep.astype(BF16), w_in.astype(BF16), w_out.astype(BF16), ldt


def _ssm_prompt_kernel(u_ref, toep_ref, win_ref, wout_ref, lp_ref, y_ref, hl_ref, carry_ref, win_bd,
                       wout_bd, *, nsteps):
    tt = pl.program_id(1)
    nc = u_ref.shape[0] // SSM_T
    sq = SSM_SQ

    @pl.when(tt == 0)
    def _():
        carry_ref[...] = jnp.zeros_like(carry_ref)
        r = lax.broadcasted_iota(jnp.int32, (2 * LANES, 2 * sq), 0)
        col = lax.broadcasted_iota(jnp.int32, (2 * LANES, 2 * sq), 1)
        own = (((r & (LANES - 1)) >> int(math.log2(SSM_P)))
               == ((col & (sq - 1)) >> int(math.log2(SSM_N))))
        for i in range(SSM_PAIRS):
            for src, dst in ((win_ref, win_bd), (wout_ref, wout_bd)):
                c = src[i]
                full = jnp.concatenate([c[:, :LANES]] * (sq // LANES) + [c[:, LANES:]] * (sq // LANES), axis=1)
                dst[i] = jnp.where(own, full, jnp.zeros_like(full))

    step = lambda t: u_ref[pl.ds(t, nc, stride=SSM_T), :].astype(BF16)
    a = [jnp.concatenate([step(2 * i), step(2 * i + 1)], axis=1) for i in range(SSM_PAIRS)]
    s = _dot(a[0], win_bd[0])
    for i in range(1, SSM_PAIRS):
        s = s + _dot(a[i], win_bd[i])
    xr, xi = s[:, :sq], s[:, sq:]
    row = lax.broadcasted_iota(jnp.int32, (nc, sq), 0)
    cr, ci = carry_ref[7:8, :sq], carry_ref[7:8, sq:]
    lr, li = lp_ref[0:1, :sq], lp_ref[0:1, sq:]
    xr = xr + jnp.where(row == 0, lr * cr - li * ci, 0.0)
    xi = xi + jnp.where(row == 0, lr * ci + li * cr, 0.0)

    def down(v, d):
        return jnp.where(row >= d, pltpu.roll(v, d, 0), 0.0)

    for k in range(nsteps):
        d = 1 << k
        lr, li = lp_ref[k:k + 1, :sq], lp_ref[k:k + 1, sq:]
        sr, si = down(xr, d), down(xi, d)
        xr, xi = xr + lr * sr - li * si, xi + lr * si + li * sr
    hr = jnp.where(row == 0, cr, pltpu.roll(xr, 1, 0))
    hi = jnp.where(row == 0, ci, pltpu.roll(xi, 1, 0))
    h_in = jnp.concatenate([hr, hi], axis=1).astype(BF16)
    last = jnp.concatenate([xr[nc - 8:], xi[nc - 8:]], axis=1)
    carry_ref[...] = last
    hl_ref[...] = last
    for j in range(SSM_PAIRS):
        acc = lax.dot_general(h_in, wout_bd[j], (((1,), (1,)), ((), ())), preferred_element_type=F32)
        for i in range(j + 1):
            acc = acc + _dot(a[i], toep_ref[j - i])
        y_ref[pl.ds(2 * j, nc, stride=SSM_T), :] = acc[:, :LANES]
        y_ref[pl.ds(2 * j + 1, nc, stride=SSM_T), :] = acc[:, LANES:]


def _ssm_prompt(u, tables):
    toep, w_in, w_out, ldt = tables
    s = u.shape[0]
    tile = min(s, 8192)
    nc = tile // SSM_T
    nsteps = int(math.log2(nc))
    scale = (SSM_T * (2.0 ** jnp.arange(nsteps, dtype=F32)))[:, None, None]
    lp = jnp.exp(ldt[None] * scale)
    lp = jnp.concatenate([lp.real.reshape(nsteps, SSM_Q, SSM_SQ), lp.imag.reshape(nsteps, SSM_Q, SSM_SQ)],
                         axis=2).transpose(1, 0, 2)
    per_q = lambda *shape: pl.BlockSpec((None,) + shape, lambda q, t: (q,) + (0,) * len(shape))
    y, hl = pl.pallas_call(
        functools.partial(_ssm_prompt_kernel, nsteps=nsteps),
        grid=(SSM_Q, s // tile),
        in_specs=[pl.BlockSpec((tile, LANES), lambda q, t: (t, q)),
                  per_q(SSM_PAIRS, 2 * LANES, 2 * LANES), per_q(SSM_PAIRS, 2 * LANES, 2 * LANES),
                  per_q(SSM_PAIRS, 2 * LANES, 2 * LANES), per_q(nsteps, 2 * SSM_SQ)],
        out_specs=[pl.BlockSpec((tile, LANES), lambda q, t: (t, q)), per_q(8, 2 * SSM_SQ)],
        out_shape=[jax.ShapeDtypeStruct((s, SSM_WIDTH), F32),
                   jax.ShapeDtypeStruct((SSM_Q, 8, 2 * SSM_SQ), F32)],
        scratch_shapes=[pltpu.VMEM((8, 2 * SSM_SQ), F32),
                        pltpu.VMEM((SSM_PAIRS, 2 * LANES, 2 * SSM_SQ), BF16),
                        pltpu.VMEM((SSM_PAIRS, 2 * LANES, 2 * SSM_SQ), BF16)],
        compiler_params=_params(2, 56), name="ssm_prompt",
    )(u, toep, w_in, w_out, lp)
    hl = hl[:, 7, :]
    return y, hl[:, :SSM_SQ].reshape(SSM_G, SSM_N), hl[:, SSM_SQ:].reshape(SSM_G, SSM_N)


def _ssm_sample_kernel(u_ref, hr_ref, hi_ref, lr_ref, li_ref, br_ref, bi_ref, cr_ref, ci_ref,
                       y_ref, or_ref, oi_ref):
    u = u_ref[...].astype(BF16)
    hr, hi = hr_ref[...], hi_ref[...]
    lr, li = lr_ref[...], li_ref[...]
    nr = lr * hr - li * hi + _dot(u, br_ref[...])
    ni = lr * hi + li * hr + _dot(u, bi_ref[...])
    or_ref[...] = nr
    oi_ref[...] = ni
    y_ref[...] = _dot(nr.astype(BF16), cr_ref[...]) + _dot(ni.astype(BF16), ci_ref[...])


def _ssm_sample(u, h_re, h_im, ssm_p):
    nb = u.shape[0]
    _, lam_bar, b_bar, c = _ssm_discretise(*ssm_p)
    eye = jnp.eye(SSM_G, dtype=F32)
    gn = SSM_G * SSM_N
    bd = lambda part, eq, shape: jnp.einsum(eq, part, eye).reshape(shape).astype(BF16)
    args = (u, h_re, h_im, lam_bar.real.reshape(1, gn), lam_bar.imag.reshape(1, gn),
            bd(b_bar.real, 'gnp,gh->gphn', (SSM_WIDTH, gn)), bd(b_bar.imag, 'gnp,gh->gphn', (SSM_WIDTH, gn)),
            bd(c.real, 'gpn,gh->gnhp', (gn, SSM_WIDTH)), bd(-c.imag, 'gpn,gh->gnhp', (gn, SSM_WIDTH)))
    return pl.pallas_call(
        _ssm_sample_kernel,
        grid=(1,),
        in_specs=[_const_spec(a.shape) for a in args],
        out_specs=[_const_spec((nb, SSM_WIDTH)), _const_spec((nb, gn)), _const_spec((nb, gn))],
        out_shape=[jax.ShapeDtypeStruct((nb, SSM_WIDTH), F32),
                   jax.ShapeDtypeStruct((nb, gn), F32), jax.ShapeDtypeStruct((nb, gn), F32)],
        compiler_params=_params(1, 32), name="ssm_sample",
    )(*args)


def _post_kernel(*refs, dils, n_phase, jobs):
    (x_ref, o1_ref, o2_ref, o3_ref, l1_ref, l2_ref, l3_ref, y_ref, u_ref, gs_ref, ga_ref,
     d_ref, wglu_ref, bglu_ref, wbs_ref, wba_ref, wout_ref, nm_ref, wup_ref, wdn_ref, nf_ref) = refs[:21]
    n_in = 21 + 4 * len(jobs)
    out_ref = refs[n_in]
    n_out = 1 + 3 * len(jobs)
    scr = refs[n_in + n_out:n_in + n_out + 6]
    x1_scr, hm_scr = refs[n_in + n_out + 6:n_in + n_out + 8]
    tm = x_ref.shape[0]
    _, in_phase = _phase_runner(n_phase)

    def natural(ref, dil, buf):
        if dil == 1:
            return ref[0]
        for r in range(dil):
            for t in range(N_TILES):
                buf[t, pl.ds(r, tm // dil, stride=dil), :] = ref[r, :, t * LANES:(t + 1) * LANES]
        return jnp.concatenate([buf[t] for t in range(N_TILES)], axis=1)

    @in_phase(0)
    def _():
        o1, o2, o3 = (natural(r, d, scr[k]) for k, (r, d) in enumerate(zip((o1_ref, o2_ref, o3_ref), dils)))
        l1, l2, l3 = (natural(r, d, scr[3 + k])
                      for k, (r, d) in enumerate(zip((l1_ref, l2_ref, l3_ref), dils)))
        m = jnp.maximum(jnp.maximum(l1, l2), l3)
        e1, e2, e3 = jnp.exp(l1 - m), jnp.exp(l2 - m), jnp.exp(l3 - m)
        den = e1 + e2 + e3
        attn = (e1 / den) * o1 + (e2 / den) * o2 + (e3 / den) * o3
        y = jax.nn.gelu(y_ref[...] + d_ref[...] * u_ref[...])
        ssm_y = y * _sigmoid(_dot(y.astype(BF16), wglu_ref[...]) + bglu_ref[...])
        z = (_sigmoid(gs_ref[...]) * _dot(ssm_y.astype(BF16), wbs_ref[...])
             + _sigmoid(ga_ref[...]) * _dot(attn.astype(BF16), wba_ref[...]))
        x1 = x_ref[...] + _dot(z.astype(BF16), wout_ref[...])
        x1_scr[...] = x1
        hm_scr[...] = _rms(x1, nm_ref[...]).astype(BF16)

    @in_phase(1 % n_phase)
    def _():
        up = jnp.maximum(_dot(hm_scr[...], wup_ref[...]), 0.0)
        x2 = x1_scr[...] + _dot((up * up).astype(BF16), wdn_ref[...])
        out_ref[...] = _rms(x2, nf_ref[...])

    _run_cache_jobs(jobs, refs[21:n_in], refs[n_in + 1:n_in + n_out], refs[n_in + n_out + 8:])


def _post(x, os_, ls_, y_ssm, u, gs, ga, consts, tm, n_phase=1, jobs=(), job_args=()):
    s = x.shape[0]
    ntiles = s // tm
    row = lambda st: (st // n_phase, 0)
    res = lambda st: (0, st // n_phase, 0)
    dils = tuple(o.shape[0] for o in os_)
    grouped = [*os_, *ls_]
    flat = [y_ssm, u, gs, ga]
    nb = job_args[0][1].shape[0] if jobs else 0
    job_in, job_out, job_shape, job_scr = _cache_job_specs(jobs, nb, ntiles * n_phase)
    return pl.pallas_call(
        functools.partial(_post_kernel, dils=dils, n_phase=n_phase, jobs=jobs),
        grid=(ntiles * n_phase,),
        in_specs=([pl.BlockSpec((tm, D_MODEL), row)]
                  + [pl.BlockSpec((a.shape[0], tm // a.shape[0], GROUP_W), res) for a in grouped]
                  + [pl.BlockSpec((tm, a.shape[1]), row) for a in flat]
                  + [_const_spec(c.shape) for c in consts] + job_in),
        out_specs=[pl.BlockSpec((tm, D_MODEL), row)] + job_out,
        out_shape=[jax.ShapeDtypeStruct((s, D_MODEL), F32)] + job_shape,
        scratch_shapes=([pltpu.VMEM((N_TILES, tm, LANES), F32)] * 6
                        + [pltpu.VMEM((tm, D_MODEL), F32), pltpu.VMEM((tm, D_MODEL), BF16)] + job_scr),
        compiler_params=_params(1, 58), name="post",
    )(x, *grouped, *flat, *consts, *[a for args in job_args for a in _cache_job_operands(*args)])


def kernel(x_prompt, x_sample, cache_kv_g1, cache_kv_g2, cache_kv_g3, state_ssm, norm_mix, w_in,
           ssm_lambda_re, ssm_lambda_im, ssm_log_dt, ssm_b_re, ssm_b_im, ssm_c_re, ssm_c_im, ssm_d,
           w_glu, b_glu, w_branch_ssm, w_branch_attn, w_out, norm_mlp, w_up, w_down, norm_final):
    depth = w_in.shape[0]
    assert depth == 1, "one decoder layer"
    seq = x_prompt.shape[1]
    nb = x_sample.shape[0]
    assert x_prompt.shape[0] == 1 and x_sample.shape[1] == 1
    caches = (cache_kv_g1, cache_kv_g2, cache_kv_g3)
    windows = tuple(w for w, _ in ATTN_GROUPS)
    dils = tuple(d for _, d in ATTN_GROUPS)
    for c, w in zip(caches, windows):
        assert c.shape[2] == w, "cache holds exactly one window"

    w_in_bf = w_in[0].astype(BF16)
    gain_mix = norm_mix[0][None, :]
    consts = (ssm_d[0][None, :], w_glu[0].astype(BF16), b_glu[0][None, :],
              w_branch_ssm[0].astype(BF16), w_branch_attn[0].astype(BF16), w_out[0].astype(BF16),
              norm_mlp[0][None, :], w_up[0].astype(BF16), w_down[0].astype(BF16), norm_final[None, :])
    ssm_p = (ssm_lambda_re[0], ssm_lambda_im[0], ssm_log_dt[0], ssm_b_re[0], ssm_b_im[0],
             ssm_c_re[0], ssm_c_im[0])

    xs = x_sample[:, 0]
    res = _inproj(xs, gain_mix, w_in_bf, jnp.full((1,), PAST_LEN), jnp.zeros((nb,)), nb,
                  (nb,) * N_GROUPS, (1,) * N_GROUPS)
    qs_s, tails_s = res[0:3], res[9:12]
    u_s, gs_s, ga_s = res[12:15]

    def cache_job(g, halves):
        lc = caches[g].shape[2]
        cache_t = caches[g][0].transpose(0, 2, 3, 4, 1).reshape(nb * 2 * GROUP_W, lc)
        return (_CacheJob(dils[g], lc, halves),
                (cache_t, qs_s[g][0], tails_s[g][:, :GROUP_W], tails_s[g][:, GROUP_W:]))

    xp = x_prompt[0]
    tm_p = 256
    ride_in = [cache_job(2, True)]
    res = _inproj(xp, gain_mix, w_in_bf, jnp.arange(seq // tm_p) * tm_p, jnp.arange(tm_p), tm_p,
                  windows, dils, n_phase=4, jobs=tuple(j for j, _ in ride_in),
                  job_args=tuple(a for _, a in ride_in))
    qs, ks, vs, tails = res[0:3], res[3:6], res[6:9], res[9:12]
    u_p, gs_p, ga_p = res[12:15]
    sample_attn = {2: _cache_job_results(*res[15:18])}
    os_p, ls_p = zip(*[_attn_prompt(qs[g], ks[g], vs[g]) for g in range(N_GROUPS)])
    y_p, hp_re, hp_im = _ssm_prompt(u_p, _ssm_tables(ssm_p))
    ride_post = [cache_job(1, False), cache_job(0, False)]
    res = _post(xp, os_p, ls_p, y_p, u_p, gs_p, ga_p, consts, tm_p, n_phase=2,
                jobs=tuple(j for j, _ in ride_post), job_args=tuple(a for _, a in ride_post))
    y_prompt = res[0][None]
    sample_attn[1] = _cache_job_results(*res[1:4])
    sample_attn[0] = _cache_job_results(*res[4:7])
    kv_prompt = [t.reshape(1, 1, w, 2, HEADS, HEAD_DIM) for t, w in zip(tails, windows)]
    ssm_prompt = jnp.stack([hp_re, hp_im], axis=-1)[None, None].astype(state_ssm.dtype)

    os_s, ls_s, kv_sample = [], [], []
    for g in range(N_GROUPS):
        new_t, o, lse = sample_attn[g]
        lc = caches[g].shape[2]
        kv_sample.append(new_t.reshape(nb, 2, HEADS, HEAD_DIM, lc).transpose(0, 4, 1, 2, 3)[None])
        os_s.append(o[None])
        ls_s.append(jnp.repeat(lse, HEAD_DIM, axis=1)[None])
    st = state_ssm[0].astype(F32)
    gn = SSM_G * SSM_N
    y_s, hs_re, hs_im = _ssm_sample(u_s, st[..., 0].reshape(nb, gn), st[..., 1].reshape(nb, gn), ssm_p)
    y_sample = _post(xs, os_s, ls_s, y_s, u_s, gs_s, ga_s, consts, nb)[0][:, None]
    ssm_sample = jnp.stack([hs_re.reshape(nb, SSM_G, SSM_N), hs_im.reshape(nb, SSM_G, SSM_N)],
                           axis=-1)[None].astype(state_ssm.dtype)

    return (y_prompt, y_sample, kv_prompt[0], kv_prompt[1], kv_prompt[2], ssm_prompt,
            kv_sample[0], kv_sample[1], kv_sample[2], ssm_sample)
```

```python
import functools
import math
from typing import NamedTuple

import jax
import jax.numpy as jnp
from jax import lax
from jax.experimental import pallas as pl
from jax.experimental.pallas import tpu as pltpu

F32 = jnp.float32
BF16 = jnp.bfloat16

D_MODEL = 1024
HEAD_DIM = 64
HEADS = 8
GROUP_W = HEADS * HEAD_DIM
ATTN_GROUPS = ((128, 1), (512, 4), (2048, 16))
N_GROUPS = len(ATTN_GROUPS)
QKV_W = N_GROUPS * GROUP_W
BAND = 128
Q_BLOCK = 512
JOB_IN, JOB_OUT = 6, 3
ROT_DIM = HEAD_DIM // 4
ROT_HALF = ROT_DIM // 2
ROPE_THETA = 500000.0
PAST_LEN = 8192
SSM_WIDTH = 512
SSM_P = 16
SSM_G = SSM_WIDTH // SSM_P
SSM_N = 64
SSM_T = 16
EPS = 1e-6
NEG = -1e30
LN2 = math.log(2.0)
Q_SCALE = HEAD_DIM ** -0.5 * math.log2(math.e)
LANES = 128
N_TILES = GROUP_W // LANES
MIB = 1024 * 1024

SSM_Q = SSM_WIDTH // LANES
SSM_GQ = SSM_G // SSM_Q
SSM_SQ = SSM_GQ * SSM_N
SSM_PAIRS = SSM_T // 2

OFF_Q, OFF_K, OFF_V = 0, QKV_W, 2 * QKV_W
OFF_U = 3 * QKV_W
OFF_GS = OFF_U + SSM_WIDTH
OFF_GA = OFF_GS + D_MODEL
IN_W = OFF_GA + D_MODEL


def _const_spec(shape):
    nd = len(shape)
    return pl.BlockSpec(shape, lambda *_: (0,) * nd, pipeline_mode=pl.Buffered(1))


def _params(n_axes, vmem_mib):
    return pltpu.CompilerParams(dimension_semantics=("arbitrary",) * n_axes,
                                vmem_limit_bytes=vmem_mib * MIB)


def _sigmoid(x):
    return 1.0 / (1.0 + jnp.exp(-x))


def _rms(x, g):
    return x * lax.rsqrt(jnp.mean(x * x, axis=-1, keepdims=True) + EPS) * g


def _dot(a, b):
    return jnp.dot(a, b, preferred_element_type=F32)


def _phase_runner(n_phase):
    step = pl.program_id(0)
    if n_phase == 1:
        return step, lambda ph: (lambda fn: fn())
    phase = step & (n_phase - 1)
    return lax.shift_right_logical(step, int(math.log2(n_phase))), lambda ph: pl.when(phase == ph)


def _inproj_kernel(*refs, tm, dils, tail_rows, first_tail, n_phase, jobs):
    n_in = 7 + JOB_IN * len(jobs)
    x_ref, g_ref, w_ref, ca_ref, sa_ref, cb_ref, sb_ref = refs[:7]
    outs = refs[n_in:n_in + 15]
    q_refs, k_refs, v_refs, tail_refs = outs[0:3], outs[3:6], outs[6:9], outs[9:12]
    u_ref, gs_ref, ga_ref = outs[12:15]
    n_out = 15 + JOB_OUT * len(jobs)
    scr, h_scr, rot_scr = refs[n_in + n_out:n_in + n_out + 3]
    i, in_phase = _phase_runner(n_phase)

    def rot(a):
        cos, s1, s2 = rot_scr[0], rot_scr[1], rot_scr[2]
        parts = []
        for j in range(GROUP_W // LANES):
            seg = a[:, j * LANES:(j + 1) * LANES]
            parts.append(seg * cos + pltpu.roll(seg, LANES - ROT_HALF, 1) * s1
                         + pltpu.roll(seg, ROT_HALF, 1) * s2)
        return jnp.concatenate(parts, axis=1)

    def put(out_ref, val, dil):
        if dil == 1:
            out_ref[0] = val.astype(BF16)
            return
        for t in range(N_TILES):
            scr[t] = val[:, t * LANES:(t + 1) * LANES]
        for r in range(dil):
            rows = [scr[t, pl.ds(r, tm // dil, stride=dil), :] for t in range(N_TILES)]
            out_ref[r] = jnp.concatenate(rows, axis=1).astype(BF16)

    def proj(off, gi):
        c0 = off + gi * GROUP_W
        return _dot(h_scr[...], w_ref[:, c0:c0 + GROUP_W])

    def put_tail(gi, half, val):
        tb = tail_rows[gi]

        @pl.when(i >= first_tail[gi])
        def _():
            tail_refs[gi][:, half * GROUP_W:(half + 1) * GROUP_W] = val[tm - tb:, :]

    @in_phase(0)
    def _():
        h_scr[...] = _rms(x_ref[...], g_ref[...]).astype(BF16)
        ca, sa = ca_ref[pl.ds(i, 1), :], sa_ref[pl.ds(i, 1), :]
        cb, sb = cb_ref[...], sb_ref[...]
        cos_f = ca * cb - sa * sb
        sin_f = sa * cb + ca * sb
        hd = lax.broadcasted_iota(jnp.int32, (tm, LANES), 1) & (HEAD_DIM - 1)
        rot_scr[0] = jnp.where(hd < ROT_DIM, cos_f, 1.0)
        rot_scr[1] = jnp.where(hd < ROT_HALF, -sin_f, 0.0)
        rot_scr[2] = jnp.where((hd >= ROT_HALF) & (hd < ROT_DIM), sin_f, 0.0)
        for gi in range(N_GROUPS):
            put(q_refs[gi], rot(proj(OFF_Q, gi)) * Q_SCALE, dils[gi])

    @in_phase(1 % n_phase)
    def _():
        for gi in range(N_GROUPS):
            ka = rot(proj(OFF_K, gi))
            put(k_refs[gi], ka, dils[gi])
            put_tail(gi, 0, ka)

    @in_phase(2 % n_phase)
    def _():
        for gi in range(N_GROUPS):
            va = proj(OFF_V, gi)
            put(v_refs[gi], va, dils[gi])
            put_tail(gi, 1, va)

    @in_phase(3 % n_phase)
    def _():
        h = h_scr[...]
        u_ref[...] = _dot(h, w_ref[:, OFF_U:OFF_GS])
        gs_ref[...] = _dot(h, w_ref[:, OFF_GS:OFF_GA])
        ga_ref[...] = _dot(h, w_ref[:, OFF_GA:IN_W])

    _run_cache_jobs(jobs, refs[7:n_in], refs[n_in + 15:n_in + n_out], refs[n_in + n_out + 3:])


def _inproj(x, gain, w_bf, tile_pos, row_pos, tm, windows, dils, n_phase=1, jobs=(), job_args=()):
    s = x.shape[0]
    ntiles = s // tm
    inv_freq = jnp.exp(-math.log(ROPE_THETA) * jnp.arange(ROT_HALF, dtype=F32) * (2.0 / ROT_DIM))
    lane_freq = jnp.tile(inv_freq, LANES // ROT_HALF)[None, :]
    ang_a = tile_pos.astype(F32)[:, None] * lane_freq
    ang_b = row_pos.astype(F32)[:, None] * lane_freq
    tail_rows = tuple(min(w, tm) for w in windows)
    first_tail = tuple(ntiles - w // tb for w, tb in zip(windows, tail_rows))
    row = lambda st: (st // n_phase, 0)
    res = lambda st: (0, st // n_phase, 0)
    qkv_shapes = [jax.ShapeDtypeStruct((d, s // d, GROUP_W), BF16) for d in dils]
    qkv_specs = [pl.BlockSpec((d, tm // d, GROUP_W), res) for d in dils]
    out_shape = (qkv_shapes * 3
                 + [jax.ShapeDtypeStruct((w, 2 * GROUP_W), F32) for w in windows]
                 + [jax.ShapeDtypeStruct((s, SSM_WIDTH), F32),
                    jax.ShapeDtypeStruct((s, D_MODEL), F32),
                    jax.ShapeDtypeStruct((s, D_MODEL), F32)])
    out_specs = (qkv_specs * 3
                 + [pl.BlockSpec((tb, 2 * GROUP_W), functools.partial(
                     lambda st, ft: (jnp.maximum(st // n_phase - ft, 0), 0), ft=ft))
                    for tb, ft in zip(tail_rows, first_tail)]
                 + [pl.BlockSpec((tm, SSM_WIDTH), row),
                    pl.BlockSpec((tm, D_MODEL), row),
                    pl.BlockSpec((tm, D_MODEL), row)])
    nb = job_args[0][1].shape[0] if jobs else 0
    job_in, job_out, job_shape, job_scr = _cache_job_specs(jobs, nb, ntiles * n_phase)
    kern = functools.partial(_inproj_kernel, tm=tm, dils=dils, tail_rows=tail_rows,
                             first_tail=first_tail, n_phase=n_phase, jobs=jobs)
    return pl.pallas_call(
        kern,
        grid=(ntiles * n_phase,),
        in_specs=[pl.BlockSpec((tm, D_MODEL), row), _const_spec((1, D_MODEL)),
                  _const_spec((D_MODEL, IN_W)),
                  _const_spec((ntiles, LANES)), _const_spec((ntiles, LANES)),
                  _const_spec((tm, LANES)), _const_spec((tm, LANES))] + job_in,
        out_specs=out_specs + job_out, out_shape=out_shape + job_shape,
        scratch_shapes=[pltpu.VMEM((N_TILES, tm, LANES), F32), pltpu.VMEM((tm, D_MODEL), BF16),
                        pltpu.VMEM((3, tm, LANES), F32)] + job_scr,
        compiler_params=_params(1, 58), name="inproj",
    )(x, gain, w_bf, jnp.cos(ang_a), jnp.sin(ang_a), jnp.cos(ang_b), jnp.sin(ang_b),
      *[a for args in job_args for a in _cache_job_operands(*args)])


def _attn_prompt_kernel(q_ref, kp_ref, kc_ref, vp_ref, vc_ref, o_ref, l_ref):
    j = pl.program_id(1)
    lane = lax.broadcasted_iota(jnp.int32, (BAND, LANES), 1)
    head_a = lane < HEAD_DIM
    qi = lax.broadcasted_iota(jnp.int32, (BAND, 2 * BAND), 0)
    kj = lax.broadcasted_iota(jnp.int32, (BAND, 2 * BAND), 1)
    band = (kj >= qi) & (kj <= qi + BAND)
    bias = jnp.where(band, 0.0, NEG)
    bias_first = jnp.where(band & (kj >= BAND), 0.0, NEG)
    for t in range(q_ref.shape[0] // BAND):
        rows = slice(t * BAND, (t + 1) * BAND)
        b = bias if t > 0 else jnp.where(j > 0, bias, bias_first)
        for hp in range(N_TILES):
            sl = slice(hp * LANES, (hp + 1) * LANES)
            q2 = q_ref[rows, sl]
            if t == 0:
                kk = jnp.concatenate([kp_ref[:, sl], kc_ref[0:BAND, sl]], axis=0)
                vv = jnp.concatenate([vp_ref[:, sl], vc_ref[0:BAND, sl]], axis=0)
            else:
                kk = kc_ref[(t - 1) * BAND:(t + 1) * BAND, sl]
                vv = vc_ref[(t - 1) * BAND:(t + 1) * BAND, sl]
            zero = jnp.zeros_like(q2)
            res = []
            for qm in (jnp.where(head_a, q2, zero), jnp.where(head_a, zero, q2)):
                s = lax.dot_general(qm, kk, (((1,), (1,)), ((), ())), preferred_element_type=F32) + b
                m = jnp.max(s, axis=1, keepdims=True)
                p = jnp.exp2(s - m)
                l = jnp.sum(p, axis=1, keepdims=True)
                o = _dot(p.astype(BF16), vv) * (1.0 / l)
                res.append((o, (m + jnp.log2(l)) * LN2))
            o_ref[rows, sl] = jnp.where(head_a, res[0][0], res[1][0])
            l_ref[rows, sl] = jnp.where(head_a, res[0][1], res[1][1])


def _attn_prompt(q, k, v):
    dil, n, _ = q.shape
    qb = min(Q_BLOCK, n)
    cur = lambda r, j: (r, j, 0)
    prev = lambda r, j: (r, jnp.maximum(j * (qb // BAND) - 1, 0), 0)
    blk = (None, qb, GROUP_W)
    hist = (None, BAND, GROUP_W)
    return pl.pallas_call(
        _attn_prompt_kernel,
        grid=(dil, n // qb),
        in_specs=[pl.BlockSpec(blk, cur), pl.BlockSpec(hist, prev), pl.BlockSpec(blk, cur),
                  pl.BlockSpec(hist, prev), pl.BlockSpec(blk, cur)],
        out_specs=[pl.BlockSpec(blk, cur), pl.BlockSpec(blk, cur)],
        out_shape=[jax.ShapeDtypeStruct((dil, n, GROUP_W), F32)] * 2,
        compiler_params=_params(2, 32), name=f"attn_prompt_d{dil}",
    )(q, k, k, v, v)


class _CacheJob(NamedTuple):
    dil: int
    lc: int
    halves: bool


def _cache_step(job, c_ref, q_ref, k_ref, v_ref, kt_ref, vt_ref, out_ref, o_ref, l_ref, *scr):
    dil, lc = job.dil, job.lc
    step = pl.program_id(0)
    seq = lax.shift_right_logical(step, 1) if job.halves else step
    lane = seq & (LANES - 1)
    lane_h = lax.broadcasted_iota(jnp.int32, (HEADS, LANES), 1)
    last = lax.broadcasted_iota(jnp.int32, (GROUP_W, LANES), 1) == LANES - 1
    own = (lax.broadcasted_iota(jnp.int32, (HEADS, GROUP_W), 1) >> int(math.log2(HEAD_DIM))
           == lax.broadcasted_iota(jnp.int32, (HEADS, GROUP_W), 0))
    row = lambda ref: ref[pl.ds(seq, 1), :]
    new_col = lambda ref: pltpu.roll(ref[...], LANES - 1 - lane, 1)
    is_k = (step & 1) == 0 if job.halves else True
    if dil > 1:
        sel_scr, scr = scr[0], scr[1:]

        @pl.when(step == 0)
        def _():
            pos = lax.broadcasted_iota(jnp.int32, (lc, BAND), 0)
            j = lax.broadcasted_iota(jnp.int32, (lc, BAND), 1)
            sel_scr[...] = jnp.where(pos == j * dil, 1.0, 0.0).astype(BF16)

    def picked(cin):
        if dil == 1:
            return cin.astype(BF16)
        return _dot(cin.astype(BF16), sel_scr[...]).astype(BF16)

    @pl.when(step == 0)
    def _():
        l_ref[...] = jnp.zeros_like(l_ref)

    def shift_into(row0, cin, col127):
        nt = lc // LANES
        rot = [pltpu.roll(cin[:, t * LANES:(t + 1) * LANES], LANES - 1, 1) for t in range(nt)]
        for t in range(nt):
            nxt = rot[t + 1] if t + 1 < nt else col127
            out_ref[row0:row0 + GROUP_W, t * LANES:(t + 1) * LANES] = jnp.where(last, nxt, rot[t])

    def scores(row0):
        kin = c_ref[row0:row0 + GROUP_W, :]
        shift_into(row0, kin, new_col(kt_ref))
        q_row = row(q_ref)
        q_heads = jnp.where(own, q_row, 0.0)
        s = _dot(q_heads.astype(BF16), picked(kin))
        s_new = jnp.sum(q_heads * row(k_ref), axis=1, keepdims=True)
        m = jnp.maximum(jnp.max(s, axis=1, keepdims=True), s_new)
        p = jnp.exp2(s - m)
        pn = jnp.exp2(s_new - m)
        l = jnp.sum(p, axis=1, keepdims=True) + pn
        l_ref[...] = jnp.where(lane_h == lane, (m + jnp.log2(l)) * LN2, l_ref[...])
        return p / l, pn / l

    def attend(row0, p, pn):
        vin = c_ref[row0:row0 + GROUP_W, :]
        shift_into(row0, vin, new_col(vt_ref))
        o_all = lax.dot_general(p.astype(BF16), picked(vin), (((1,), (1,)), ((), ())),
                                preferred_element_type=F32)
        o_all = o_all + pn * row(v_ref)
        o_ref[pl.ds(seq, 1), :] = jnp.sum(jnp.where(own, o_all, 0.0), axis=0, keepdims=True)

    if not job.halves:
        p, pn = scores(0)
        attend(GROUP_W, p, pn)
        return
    p_scr, pn_scr = scr

    @pl.when(is_k)
    def _():
        p, pn = scores(0)
        p_scr[...] = p
        pn_scr[...] = jnp.broadcast_to(pn, pn_scr.shape)

    @pl.when(jnp.logical_not(is_k))
    def _():
        attend(0, p_scr[...], pn_scr[:, 0:1])


def _run_cache_jobs(jobs, in_refs, out_refs, scr_refs):
    k = 0
    for n, job in enumerate(jobs):
        ns = (1 if job.dil > 1 else 0) + (2 if job.halves else 0)
        _cache_step(job, *in_refs[JOB_IN * n:JOB_IN * (n + 1)], *out_refs[JOB_OUT * n:JOB_OUT * (n + 1)],
                    *scr_refs[k:k + ns])
        k += ns


def _cache_job_specs(jobs, nb, nsteps):
    in_specs, out_specs, out_shape, scratch = [], [], [], []
    for job in jobs:
        per_seq = 2 if job.halves else 1
        assert nsteps == nb * per_seq, "one cache unit per grid step"
        blk = pl.BlockSpec((2 * GROUP_W // per_seq, job.lc), lambda st: (st, 0))
        rows = _const_spec((nb, GROUP_W))
        nbp = -(-nb // LANES) * LANES
        lane_blk = functools.partial(lambda st, n: (0, st // n), n=LANES * per_seq)
        cols = pl.BlockSpec((GROUP_W, LANES), lane_blk)
        in_specs += [blk, rows, rows, rows, cols, cols]
        out_specs += [blk, rows, pl.BlockSpec((HEADS, LANES), lane_blk)]
        out_shape += [jax.ShapeDtypeStruct((nb * 2 * GROUP_W, job.lc), F32),
                      jax.ShapeDtypeStruct((nb, GROUP_W), F32),
                      jax.ShapeDtypeStruct((HEADS, nbp), F32)]
        if job.dil > 1:
            scratch += [pltpu.VMEM((job.lc, BAND), BF16)]
        if job.halves:
            scratch += [pltpu.VMEM((HEADS, BAND), F32), pltpu.VMEM((HEADS, LANES), F32)]
    return in_specs, out_specs, out_shape, scratch


def _cache_job_operands(cache_t, q, k_new, v_new):
    nb = q.shape[0]
    cols = lambda t: jnp.pad(t.T, ((0, 0), (0, -nb % LANES)))
    return cache_t, q.astype(F32), k_new, v_new, cols(k_new), cols(v_new)


def _cache_job_results(new_t, o, lse):
    return new_t, o, lse[:, :o.shape[0]].T


def _ssm_discretise(lam_re, lam_im, log_dt, b_re, b_im, c_re, c_im):
    lam = lax.complex(lam_re.astype(F32), lam_im.astype(F32))
    dt = jnp.exp(log_dt.astype(F32))[:, None]
    ldt = lam * dt
    lam_bar = jnp.exp(ldt)
    b_bar = ((lam_bar - 1.0) / lam)[..., None] * lax.complex(b_re.astype(F32), b_im.astype(F32))
    c = lax.complex(c_re.astype(F32), c_im.astype(F32))
    return ldt, lam_bar, b_bar, c


def _ssm_tables(ssm_p):
    ldt, _, b_bar, c = _ssm_discretise(*ssm_p)
    t, q, gq, pr = SSM_T, SSM_Q, SSM_GQ, SSM_PAIRS
    steps = jnp.arange(t + 1, dtype=F32)
    pw = jnp.exp(ldt[None] * steps[:, None, None])
    hi = lax.Precision.HIGHEST
    eye = jnp.eye(gq, dtype=F32)
    kern = jnp.einsum('gqn,kgn,gnp->gkpq', c, pw[:t], b_bar, precision=hi).real
    kern = kern.reshape(q, gq, t, SSM_P, SSM_P)
    d_, a_, b_ = jnp.meshgrid(jnp.arange(pr), jnp.arange(2), jnp.arange(2), indexing='ij')
    lag = 2 * d_ + b_ - a_
    kl = jnp.where((lag >= 0)[None, None, :, :, :, None, None], kern[:, :, jnp.maximum(lag, 0)], 0.0)
    toep = jnp.einsum('qgdabpr,gh->qdagpbhr', kl, eye).reshape(q, pr, 2 * LANES, 2 * LANES)
    win = pw[:t][::-1][:, :, :, None] * b_bar[None]
    win = win.reshape(pr, 2, q, gq, SSM_N, SSM_P).transpose(2, 0, 1, 3, 5, 4)
    w_in = jnp.concatenate([win.real, win.real, win.imag, win.imag], axis=-1)
    w_in = w_in.reshape(q, pr, 2 * LANES, 2 * LANES)
    cl = c[None] * pw[1:, :, None, :]
    cl = cl.reshape(pr, 2, q, gq, SSM_P, SSM_N).transpose(2, 0, 1, 3, 4, 5)
    w_out = jnp.concatenate([cl.real, cl.real, -cl.imag, -cl.imag], axis=-1)
    w_out = w_out.reshape(q, pr, 2 * LANES, 2 * LANES)
    return toep.astype(BF16), w_in.astype(BF16), w_out.astype(BF16), ldt


def _ssm_prompt_kernel(u_ref, toep_ref, win_ref, wout_ref, lp_ref, y_ref, hl_ref, carry_ref, win_bd,
                       wout_bd, *, nsteps):
    tt = pl.program_id(1)
    nc = u_ref.shape[0] // SSM_T
    sq = SSM_SQ

    @pl.when(tt == 0)
    def _():
        carry_ref[...] = jnp.zeros_like(carry_ref)
        r = lax.broadcasted_iota(jnp.int32, (2 * LANES, 2 * sq), 0)
        col = lax.broadcasted_iota(jnp.int32, (2 * LANES, 2 * sq), 1)
        own = (((r & (LANES - 1)) >> int(math.log2(SSM_P)))
               == ((col & (sq - 1)) >> int(math.log2(SSM_N))))
        for i in range(SSM_PAIRS):
            for src, dst in ((win_ref, win_bd), (wout_ref, wout_bd)):
                c = src[i]
                full = jnp.concatenate([c[:, :LANES]] * (sq // LANES) + [c[:, LANES:]] * (sq // LANES), axis=1)
                dst[i] = jnp.where(own, full, jnp.zeros_like(full))

    step = lambda t: u_ref[pl.ds(t, nc, stride=SSM_T), :].astype(BF16)
    a = [jnp.concatenate([step(2 * i), step(2 * i + 1)], axis=1) for i in range(SSM_PAIRS)]
    s = _dot(a[0], win_bd[0])
    for i in range(1, SSM_PAIRS):
        s = s + _dot(a[i], win_bd[i])
    xr, xi = s[:, :sq], s[:, sq:]
    row = lax.broadcasted_iota(jnp.int32, (nc, sq), 0)
    cr, ci = carry_ref[7:8, :sq], carry_ref[7:8, sq:]
    lr, li = lp_ref[0:1, :sq], lp_ref[0:1, sq:]
    xr = xr + jnp.where(row == 0, lr * cr - li * ci, 0.0)
    xi = xi + jnp.where(row == 0, lr * ci + li * cr, 0.0)

    def down(v, d):
        return jnp.where(row >= d, pltpu.roll(v, d, 0), 0.0)

    for k in range(nsteps):
        d = 1 << k
        lr, li = lp_ref[k:k + 1, :sq], lp_ref[k:k + 1, sq:]
        sr, si = down(xr, d), down(xi, d)
        xr, xi = xr + lr * sr - li * si, xi + lr * si + li * sr
    hr = jnp.where(row == 0, cr, pltpu.roll(xr, 1, 0))
    hi = jnp.where(row == 0, ci, pltpu.roll(xi, 1, 0))
    h_in = jnp.concatenate([hr, hi], axis=1).astype(BF16)
    last = jnp.concatenate([xr[nc - 8:], xi[nc - 8:]], axis=1)
    carry_ref[...] = last
    hl_ref[...] = last
    for j in range(SSM_PAIRS):
        acc = lax.dot_general(h_in, wout_bd[j], (((1,), (1,)), ((), ())), preferred_element_type=F32)
        for i in range(j + 1):
            acc = acc + _dot(a[i], toep_ref[j - i])
        y_ref[pl.ds(2 * j, nc, stride=SSM_T), :] = acc[:, :LANES]
        y_ref[pl.ds(2 * j + 1, nc, stride=SSM_T), :] = acc[:, LANES:]


def _ssm_prompt(u, tables):
    toep, w_in, w_out, ldt = tables
    s = u.shape[0]
    tile = min(s, 8192)
    nc = tile // SSM_T
    nsteps = int(math.log2(nc))
    scale = (SSM_T * (2.0 ** jnp.arange(nsteps, dtype=F32)))[:, None, None]
    lp = jnp.exp(ldt[None] * scale)
    lp = jnp.concatenate([lp.real.reshape(nsteps, SSM_Q, SSM_SQ), lp.imag.reshape(nsteps, SSM_Q, SSM_SQ)],
                         axis=2).transpose(1, 0, 2)
    per_q = lambda *shape: pl.BlockSpec((None,) + shape, lambda q, t: (q,) + (0,) * len(shape))
    y, hl = pl.pallas_call(
        functools.partial(_ssm_prompt_kernel, nsteps=nsteps),
        grid=(SSM_Q, s // tile),
        in_specs=[pl.BlockSpec((tile, LANES), lambda q, t: (t, q)),
                  per_q(SSM_PAIRS, 2 * LANES, 2 * LANES), per_q(SSM_PAIRS, 2 * LANES, 2 * LANES),
                  per_q(SSM_PAIRS, 2 * LANES, 2 * LANES), per_q(nsteps, 2 * SSM_SQ)],
        out_specs=[pl.BlockSpec((tile, LANES), lambda q, t: (t, q)), per_q(8, 2 * SSM_SQ)],
        out_shape=[jax.ShapeDtypeStruct((s, SSM_WIDTH), F32),
                   jax.ShapeDtypeStruct((SSM_Q, 8, 2 * SSM_SQ), F32)],
        scratch_shapes=[pltpu.VMEM((8, 2 * SSM_SQ), F32),
                        pltpu.VMEM((SSM_PAIRS, 2 * LANES, 2 * SSM_SQ), BF16),
                        pltpu.VMEM((SSM_PAIRS, 2 * LANES, 2 * SSM_SQ), BF16)],
        compiler_params=_params(2, 56), name="ssm_prompt",
    )(u, toep, w_in, w_out, lp)
    hl = hl[:, 7, :]
    return y, hl[:, :SSM_SQ].reshape(SSM_G, SSM_N), hl[:, SSM_SQ:].reshape(SSM_G, SSM_N)


def _ssm_sample_kernel(u_ref, hr_ref, hi_ref, lr_ref, li_ref, br_ref, bi_ref, cr_ref, ci_ref,
                       y_ref, or_ref, oi_ref):
    u = u_ref[...].astype(BF16)
    hr, hi = hr_ref[...], hi_ref[...]
    lr, li = lr_ref[...], li_ref[...]
    nr = lr * hr - li * hi + _dot(u, br_ref[...])
    ni = lr * hi + li * hr + _dot(u, bi_ref[...])
    or_ref[...] = nr
    oi_ref[...] = ni
    y_ref[...] = _dot(nr.astype(BF16), cr_ref[...]) + _dot(ni.astype(BF16), ci_ref[...])


def _ssm_sample(u, h_re, h_im, ssm_p):
    nb = u.shape[0]
    _, lam_bar, b_bar, c = _ssm_discretise(*ssm_p)
    eye = jnp.eye(SSM_G, dtype=F32)
    gn = SSM_G * SSM_N
    bd = lambda part, eq, shape: jnp.einsum(eq, part, eye).reshape(shape).astype(BF16)
    args = (u, h_re, h_im, lam_bar.real.reshape(1, gn), lam_bar.imag.reshape(1, gn),
            bd(b_bar.real, 'gnp,gh->gphn', (SSM_WIDTH, gn)), bd(b_bar.imag, 'gnp,gh->gphn', (SSM_WIDTH, gn)),
            bd(c.real, 'gpn,gh->gnhp', (gn, SSM_WIDTH)), bd(-c.imag, 'gpn,gh->gnhp', (gn, SSM_WIDTH)))
    return pl.pallas_call(
        _ssm_sample_kernel,
        grid=(1,),
        in_specs=[_const_spec(a.shape) for a in args],
        out_specs=[_const_spec((nb, SSM_WIDTH)), _const_spec((nb, gn)), _const_spec((nb, gn))],
        out_shape=[jax.ShapeDtypeStruct((nb, SSM_WIDTH), F32),
                   jax.ShapeDtypeStruct((nb, gn), F32), jax.ShapeDtypeStruct((nb, gn), F32)],
        compiler_params=_params(1, 32), name="ssm_sample",
    )(*args)


def _post_kernel(*refs, dils, n_phase, jobs):
    (x_ref, o1_ref, o2_ref, o3_ref, l1_ref, l2_ref, l3_ref, y_ref, u_ref, gs_ref, ga_ref,
     d_ref, wglu_ref, bglu_ref, wbs_ref, wba_ref, wout_ref, nm_ref, wup_ref, wdn_ref, nf_ref) = refs[:21]
    n_in = 21 + JOB_IN * len(jobs)
    out_ref = refs[n_in]
    n_out = 1 + JOB_OUT * len(jobs)
    scr = refs[n_in + n_out:n_in + n_out + 6]
    x1_scr, hm_scr = refs[n_in + n_out + 6:n_in + n_out + 8]
    tm = x_ref.shape[0]
    _, in_phase = _phase_runner(n_phase)

    def natural(ref, dil, buf):
        if dil == 1:
            return ref[0]
        for r in range(dil):
            for t in range(N_TILES):
                buf[t, pl.ds(r, tm // dil, stride=dil), :] = ref[r, :, t * LANES:(t + 1) * LANES]
        return jnp.concatenate([buf[t] for t in range(N_TILES)], axis=1)

    @in_phase(0)
    def _():
        o1, o2, o3 = (natural(r, d, scr[k]) for k, (r, d) in enumerate(zip((o1_ref, o2_ref, o3_ref), dils)))
        l1, l2, l3 = (natural(r, d, scr[3 + k])
                      for k, (r, d) in enumerate(zip((l1_ref, l2_ref, l3_ref), dils)))
        m = jnp.maximum(jnp.maximum(l1, l2), l3)
        e1, e2, e3 = jnp.exp(l1 - m), jnp.exp(l2 - m), jnp.exp(l3 - m)
        den = e1 + e2 + e3
        attn = (e1 / den) * o1 + (e2 / den) * o2 + (e3 / den) * o3
        y = jax.nn.gelu(y_ref[...] + d_ref[...] * u_ref[...])
        ssm_y = y * _sigmoid(_dot(y.astype(BF16), wglu_ref[...]) + bglu_ref[...])
        z = (_sigmoid(gs_ref[...]) * _dot(ssm_y.astype(BF16), wbs_ref[...])
             + _sigmoid(ga_ref[...]) * _dot(attn.astype(BF16), wba_ref[...]))
        x1 = x_ref[...] + _dot(z.astype(BF16), wout_ref[...])
        x1_scr[...] = x1
        hm_scr[...] = _rms(x1, nm_ref[...]).astype(BF16)

    @in_phase(1 % n_phase)
    def _():
        up = jnp.maximum(_dot(hm_scr[...], wup_ref[...]), 0.0)
        x2 = x1_scr[...] + _dot((up * up).astype(BF16), wdn_ref[...])
        out_ref[...] = _rms(x2, nf_ref[...])

    _run_cache_jobs(jobs, refs[21:n_in], refs[n_in + 1:n_in + n_out], refs[n_in + n_out + 8:])


def _post(x, os_, ls_, y_ssm, u, gs, ga, consts, tm, n_phase=1, jobs=(), job_args=()):
    s = x.shape[0]
    ntiles = s // tm
    row = lambda st: (st // n_phase, 0)
    res = lambda st: (0, st // n_phase, 0)
    dils = tuple(o.shape[0] for o in os_)
    grouped = [*os_, *ls_]
    flat = [y_ssm, u, gs, ga]
    nb = job_args[0][1].shape[0] if jobs else 0
    job_in, job_out, job_shape, job_scr = _cache_job_specs(jobs, nb, ntiles * n_phase)
    return pl.pallas_call(
        functools.partial(_post_kernel, dils=dils, n_phase=n_phase, jobs=jobs),
        grid=(ntiles * n_phase,),
        in_specs=([pl.BlockSpec((tm, D_MODEL), row)]
                  + [pl.BlockSpec((a.shape[0], tm // a.shape[0], GROUP_W), res) for a in grouped]
                  + [pl.BlockSpec((tm, a.shape[1]), row) for a in flat]
                  + [_const_spec(c.shape) for c in consts] + job_in),
        out_specs=[pl.BlockSpec((tm, D_MODEL), row)] + job_out,
        out_shape=[jax.ShapeDtypeStruct((s, D_MODEL), F32)] + job_shape,
        scratch_shapes=([pltpu.VMEM((N_TILES, tm, LANES), F32)] * 6
                        + [pltpu.VMEM((tm, D_MODEL), F32), pltpu.VMEM((tm, D_MODEL), BF16)] + job_scr),
        compiler_params=_params(1, 58), name="post",
    )(x, *grouped, *flat, *consts, *[a for args in job_args for a in _cache_job_operands(*args)])


def kernel(x_prompt, x_sample, cache_kv_g1, cache_kv_g2, cache_kv_g3, state_ssm, norm_mix, w_in,
           ssm_lambda_re, ssm_lambda_im, ssm_log_dt, ssm_b_re, ssm_b_im, ssm_c_re, ssm_c_im, ssm_d,
           w_glu, b_glu, w_branch_ssm, w_branch_attn, w_out, norm_mlp, w_up, w_down, norm_final):
    depth = w_in.shape[0]
    assert depth == 1, "one decoder layer"
    seq = x_prompt.shape[1]
    nb = x_sample.shape[0]
    assert x_prompt.shape[0] == 1 and x_sample.shape[1] == 1
    caches = (cache_kv_g1, cache_kv_g2, cache_kv_g3)
    windows = tuple(w for w, _ in ATTN_GROUPS)
    dils = tuple(d for _, d in ATTN_GROUPS)
    for c, w in zip(caches, windows):
        assert c.shape[2] == w, "cache holds exactly one window"

    w_in_bf = w_in[0].astype(BF16)
    gain_mix = norm_mix[0][None, :]
    consts = (ssm_d[0][None, :], w_glu[0].astype(BF16), b_glu[0][None, :],
              w_branch_ssm[0].astype(BF16), w_branch_attn[0].astype(BF16), w_out[0].astype(BF16),
              norm_mlp[0][None, :], w_up[0].astype(BF16), w_down[0].astype(BF16), norm_final[None, :])
    ssm_p = (ssm_lambda_re[0], ssm_lambda_im[0], ssm_log_dt[0], ssm_b_re[0], ssm_b_im[0],
             ssm_c_re[0], ssm_c_im[0])

    xs = x_sample[:, 0]
    res = _inproj(xs, gain_mix, w_in_bf, jnp.full((1,), PAST_LEN), jnp.zeros((nb,)), nb,
                  (nb,) * N_GROUPS, (1,) * N_GROUPS)
    qs_s, tails_s = res[0:3], res[9:12]
    u_s, gs_s, ga_s = res[12:15]

    def cache_job(g, halves):
        lc = caches[g].shape[2]
        cache_t = caches[g][0].transpose(0, 2, 3, 4, 1).reshape(nb * 2 * GROUP_W, lc)
        return (_CacheJob(dils[g], lc, halves),
                (cache_t, qs_s[g][0], tails_s[g][:, :GROUP_W], tails_s[g][:, GROUP_W:]))

    xp = x_prompt[0]
    tm_p = 256
    ride_in = [cache_job(2, True)]
    res = _inproj(xp, gain_mix, w_in_bf, jnp.arange(seq // tm_p) * tm_p, jnp.arange(tm_p), tm_p,
                  windows, dils, n_phase=4, jobs=tuple(j for j, _ in ride_in),
                  job_args=tuple(a for _, a in ride_in))
    qs, ks, vs, tails = res[0:3], res[3:6], res[6:9], res[9:12]
    u_p, gs_p, ga_p = res[12:15]
    sample_attn = {2: _cache_job_results(*res[15:18])}
    os_p, ls_p = zip(*[_attn_prompt(qs[g], ks[g], vs[g]) for g in range(N_GROUPS)])
    y_p, hp_re, hp_im = _ssm_prompt(u_p, _ssm_tables(ssm_p))
    ride_post = [cache_job(1, False), cache_job(0, False)]
    res = _post(xp, os_p, ls_p, y_p, u_p, gs_p, ga_p, consts, tm_p, n_phase=2,
                jobs=tuple(j for j, _ in ride_post), job_args=tuple(a for _, a in ride_post))
    y_prompt = res[0][None]
    sample_attn[1] = _cache_job_results(*res[1:4])
    sample_attn[0] = _cache_job_results(*res[4:7])
    kv_prompt = [t.reshape(1, 1, w, 2, HEADS, HEAD_DIM) for t, w in zip(tails, windows)]
    ssm_prompt = jnp.stack([hp_re, hp_im], axis=-1)[None, None].astype(state_ssm.dtype)

    os_s, ls_s, kv_sample = [], [], []
    for g in range(N_GROUPS):
        new_t, o, lse = sample_attn[g]
        lc = caches[g].shape[2]
        kv_sample.append(new_t.reshape(nb, 2, HEADS, HEAD_DIM, lc).transpose(0, 4, 1, 2, 3)[None])
        os_s.append(o[None])
        ls_s.append(jnp.repeat(lse, HEAD_DIM, axis=1)[None])
    st = state_ssm[0].astype(F32)
    gn = SSM_G * SSM_N
    y_s, hs_re, hs_im = _ssm_sample(u_s, st[..., 0].reshape(nb, gn), st[..., 1].reshape(nb, gn), ssm_p)
    y_sample = _post(xs, os_s, ls_s, y_s, u_s, gs_s, ga_s, consts, nb)[0][:, None]
    ssm_sample = jnp.stack([hs_re.reshape(nb, SSM_G, SSM_N), hs_im.reshape(nb, SSM_G, SSM_N)],
                           axis=-1)[None].astype(state_ssm.dtype)

    return (y_prompt, y_sample, kv_prompt[0], kv_prompt[1], kv_prompt[2], ssm_prompt,
            kv_sample[0], kv_sample[1], kv_sample[2], ssm_sample)
```

```python
import functools
import math
from typing import NamedTuple

import jax
import jax.numpy as jnp
from jax import lax
from jax.experimental import pallas as pl
from jax.experimental.pallas import tpu as pltpu

F32 = jnp.float32
BF16 = jnp.bfloat16

D_MODEL = 1024
HEAD_DIM = 64
HEADS = 8
GROUP_W = HEADS * HEAD_DIM
ATTN_GROUPS = ((128, 1), (512, 4), (2048, 16))
N_GROUPS = len(ATTN_GROUPS)
QKV_W = N_GROUPS * GROUP_W
BAND = 128
Q_BLOCK = 512
ROT_DIM = HEAD_DIM // 4
ROT_HALF = ROT_DIM // 2
ROPE_THETA = 500000.0
PAST_LEN = 8192
SSM_WIDTH = 512
SSM_P = 16
SSM_G = SSM_WIDTH // SSM_P
SSM_N = 64
SSM_T = 16
EPS = 1e-6
NEG = -1e30
LN2 = math.log(2.0)
Q_SCALE = HEAD_DIM ** -0.5 * math.log2(math.e)
LANES = 128
N_TILES = GROUP_W // LANES
MIB = 1024 * 1024

SSM_Q = SSM_WIDTH // LANES
SSM_GQ = SSM_G // SSM_Q
SSM_SQ = SSM_GQ * SSM_N
SSM_PAIRS = SSM_T // 2

OFF_Q, OFF_K, OFF_V = 0, QKV_W, 2 * QKV_W
OFF_U = 3 * QKV_W
OFF_GS = OFF_U + SSM_WIDTH
OFF_GA = OFF_GS + D_MODEL
IN_W = OFF_GA + D_MODEL


def _const_spec(shape):
    nd = len(shape)
    return pl.BlockSpec(shape, lambda *_: (0,) * nd, pipeline_mode=pl.Buffered(1))


def _params(n_axes, vmem_mib):
    return pltpu.CompilerParams(dimension_semantics=("arbitrary",) * n_axes,
                                vmem_limit_bytes=vmem_mib * MIB)


def _sigmoid(x):
    return 1.0 / (1.0 + jnp.exp(-x))


def _rms(x, g):
    return x * lax.rsqrt(jnp.mean(x * x, axis=-1, keepdims=True) + EPS) * g


def _dot(a, b):
    return jnp.dot(a, b, preferred_element_type=F32)


def _phase_runner(n_phase):
    step = pl.program_id(0)
    if n_phase == 1:
        return step, lambda ph: (lambda fn: fn())
    phase = step & (n_phase - 1)
    return lax.shift_right_logical(step, int(math.log2(n_phase))), lambda ph: pl.when(phase == ph)


def _inproj_kernel(*refs, tm, dils, tail_rows, first_tail, n_phase, jobs):
    n_in = 7 + sum(JOB_ARITY[j.part][0] for j in jobs)
    x_ref, g_ref, w_ref, ca_ref, sa_ref, cb_ref, sb_ref = refs[:7]
    outs = refs[n_in:n_in + 15]
    q_refs, k_refs, v_refs, tail_refs = outs[0:3], outs[3:6], outs[6:9], outs[9:12]
    u_ref, gs_ref, ga_ref = outs[12:15]
    n_out = 15 + sum(JOB_ARITY[j.part][1] for j in jobs)
    scr, h_scr, rot_scr = refs[n_in + n_out:n_in + n_out + 3]
    i, in_phase = _phase_runner(n_phase)

    def rot(a):
        cos, s1, s2 = rot_scr[0], rot_scr[1], rot_scr[2]
        parts = []
        for j in range(GROUP_W // LANES):
            seg = a[:, j * LANES:(j + 1) * LANES]
            parts.append(seg * cos + pltpu.roll(seg, LANES - ROT_HALF, 1) * s1
                         + pltpu.roll(seg, ROT_HALF, 1) * s2)
        return jnp.concatenate(parts, axis=1)

    def put(out_ref, val, dil):
        if dil == 1:
            out_ref[0] = val.astype(BF16)
            return
        for t in range(N_TILES):
            scr[t] = val[:, t * LANES:(t + 1) * LANES]
        for r in range(dil):
            rows = [scr[t, pl.ds(r, tm // dil, stride=dil), :] for t in range(N_TILES)]
            out_ref[r] = jnp.concatenate(rows, axis=1).astype(BF16)

    def proj(off, gi):
        c0 = off + gi * GROUP_W
        return _dot(h_scr[...], w_ref[:, c0:c0 + GROUP_W])

    def put_tail(gi, half, val):
        tb = tail_rows[gi]

        @pl.when(i >= first_tail[gi])
        def _():
            tail_refs[gi][:, half * GROUP_W:(half + 1) * GROUP_W] = val[tm - tb:, :]

    @in_phase(0)
    def _():
        h_scr[...] = _rms(x_ref[...], g_ref[...]).astype(BF16)
        ca, sa = ca_ref[pl.ds(i, 1), :], sa_ref[pl.ds(i, 1), :]
        cb, sb = cb_ref[...], sb_ref[...]
        cos_f = ca * cb - sa * sb
        sin_f = sa * cb + ca * sb
        hd = lax.broadcasted_iota(jnp.int32, (tm, LANES), 1) & (HEAD_DIM - 1)
        rot_scr[0] = jnp.where(hd < ROT_DIM, cos_f, 1.0)
        rot_scr[1] = jnp.where(hd < ROT_HALF, -sin_f, 0.0)
        rot_scr[2] = jnp.where((hd >= ROT_HALF) & (hd < ROT_DIM), sin_f, 0.0)
        for gi in range(N_GROUPS):
            put(q_refs[gi], rot(proj(OFF_Q, gi)) * Q_SCALE, dils[gi])

    @in_phase(1 % n_phase)
    def _():
        for gi in range(N_GROUPS):
            ka = rot(proj(OFF_K, gi))
            put(k_refs[gi], ka, dils[gi])
            put_tail(gi, 0, ka)

    @in_phase(2 % n_phase)
    def _():
        for gi in range(N_GROUPS):
            va = proj(OFF_V, gi)
            put(v_refs[gi], va, dils[gi])
            put_tail(gi, 1, va)

    @in_phase(3 % n_phase)
    def _():
        h = h_scr[...]
        u_ref[...] = _dot(h, w_ref[:, OFF_U:OFF_GS])
        gs_ref[...] = _dot(h, w_ref[:, OFF_GS:OFF_GA])
        ga_ref[...] = _dot(h, w_ref[:, OFF_GA:IN_W])

    _run_cache_jobs(jobs, refs[7:n_in], refs[n_in + 15:n_in + n_out], refs[n_in + n_out + 3:])


def _inproj(x, gain, w_bf, tile_pos, row_pos, tm, windows, dils, n_phase=1, jobs=(), job_args=()):
    s = x.shape[0]
    ntiles = s // tm
    inv_freq = jnp.exp(-math.log(ROPE_THETA) * jnp.arange(ROT_HALF, dtype=F32) * (2.0 / ROT_DIM))
    lane_freq = jnp.tile(inv_freq, LANES // ROT_HALF)[None, :]
    ang_a = tile_pos.astype(F32)[:, None] * lane_freq
    ang_b = row_pos.astype(F32)[:, None] * lane_freq
    tail_rows = tuple(min(w, tm) for w in windows)
    first_tail = tuple(ntiles - w // tb for w, tb in zip(windows, tail_rows))
    row = lambda st: (st // n_phase, 0)
    res = lambda st: (0, st // n_phase, 0)
    qkv_shapes = [jax.ShapeDtypeStruct((d, s // d, GROUP_W), BF16) for d in dils]
    qkv_specs = [pl.BlockSpec((d, tm // d, GROUP_W), res) for d in dils]
    out_shape = (qkv_shapes * 3
                 + [jax.ShapeDtypeStruct((w, 2 * GROUP_W), F32) for w in windows]
                 + [jax.ShapeDtypeStruct((s, SSM_WIDTH), F32),
                    jax.ShapeDtypeStruct((s, D_MODEL), F32),
                    jax.ShapeDtypeStruct((s, D_MODEL), F32)])
    out_specs = (qkv_specs * 3
                 + [pl.BlockSpec((tb, 2 * GROUP_W), functools.partial(
                     lambda st, ft: (jnp.maximum(st // n_phase - ft, 0), 0), ft=ft))
                    for tb, ft in zip(tail_rows, first_tail)]
                 + [pl.BlockSpec((tm, SSM_WIDTH), row),
                    pl.BlockSpec((tm, D_MODEL), row),
                    pl.BlockSpec((tm, D_MODEL), row)])
    nb = job_args[0][1].shape[0] if jobs else 0
    job_in, job_out, job_shape, job_scr, job_alias = _cache_job_specs(jobs, nb, ntiles * n_phase)
    kern = functools.partial(_inproj_kernel, tm=tm, dils=dils, tail_rows=tail_rows,
                             first_tail=first_tail, n_phase=n_phase, jobs=jobs)
    return pl.pallas_call(
        kern,
        grid=(ntiles * n_phase,),
        in_specs=[pl.BlockSpec((tm, D_MODEL), row), _const_spec((1, D_MODEL)),
                  _const_spec((D_MODEL, IN_W)),
                  _const_spec((ntiles, LANES)), _const_spec((ntiles, LANES)),
                  _const_spec((tm, LANES)), _const_spec((tm, LANES))] + job_in,
        out_specs=out_specs + job_out, out_shape=out_shape + job_shape,
        scratch_shapes=[pltpu.VMEM((N_TILES, tm, LANES), F32), pltpu.VMEM((tm, D_MODEL), BF16),
                        pltpu.VMEM((3, tm, LANES), F32)] + job_scr,
        input_output_aliases={7 + i: 15 + o for i, o in job_alias},
        compiler_params=_params(1, 58), name="inproj",
    )(x, gain, w_bf, jnp.cos(ang_a), jnp.sin(ang_a), jnp.cos(ang_b), jnp.sin(ang_b),
      *[a for args in job_args for a in args])


def _attn_prompt_kernel(q_ref, kp_ref, kc_ref, vp_ref, vc_ref, o_ref, l_ref):
    j = pl.program_id(1)
    lane = lax.broadcasted_iota(jnp.int32, (BAND, LANES), 1)
    head_a = lane < HEAD_DIM
    qi = lax.broadcasted_iota(jnp.int32, (BAND, 2 * BAND), 0)
    kj = lax.broadcasted_iota(jnp.int32, (BAND, 2 * BAND), 1)
    band = (kj >= qi) & (kj <= qi + BAND)
    bias = jnp.where(band, 0.0, NEG)
    bias_first = jnp.where(band & (kj >= BAND), 0.0, NEG)
    for t in range(q_ref.shape[0] // BAND):
        rows = slice(t * BAND, (t + 1) * BAND)
        b = bias if t > 0 else jnp.where(j > 0, bias, bias_first)
        for hp in range(N_TILES):
            sl = slice(hp * LANES, (hp + 1) * LANES)
            q2 = q_ref[rows, sl]
            if t == 0:
                kk = jnp.concatenate([kp_ref[:, sl], kc_ref[0:BAND, sl]], axis=0)
                vv = jnp.concatenate([vp_ref[:, sl], vc_ref[0:BAND, sl]], axis=0)
            else:
                kk = kc_ref[(t - 1) * BAND:(t + 1) * BAND, sl]
                vv = vc_ref[(t - 1) * BAND:(t + 1) * BAND, sl]
            zero = jnp.zeros_like(q2)
            res = []
            for qm in (jnp.where(head_a, q2, zero), jnp.where(head_a, zero, q2)):
                s = lax.dot_general(qm, kk, (((1,), (1,)), ((), ())), preferred_element_type=F32) + b
                m = jnp.max(s, axis=1, keepdims=True)
                p = jnp.exp2(s - m)
                l = jnp.sum(p, axis=1, keepdims=True)
                o = _dot(p.astype(BF16), vv) * (1.0 / l)
                res.append((o, (m + jnp.log2(l)) * LN2))
            o_ref[rows, sl] = jnp.where(head_a, res[0][0], res[1][0])
            l_ref[rows, sl] = jnp.where(head_a, res[0][1], res[1][1])


def _attn_prompt(q, k, v):
    dil, n, _ = q.shape
    qb = min(Q_BLOCK, n)
    cur = lambda r, j: (r, j, 0)
    prev = lambda r, j: (r, jnp.maximum(j * (qb // BAND) - 1, 0), 0)
    blk = (None, qb, GROUP_W)
    hist = (None, BAND, GROUP_W)
    return pl.pallas_call(
        _attn_prompt_kernel,
        grid=(dil, n // qb),
        in_specs=[pl.BlockSpec(blk, cur), pl.BlockSpec(hist, prev), pl.BlockSpec(blk, cur),
                  pl.BlockSpec(hist, prev), pl.BlockSpec(blk, cur)],
        out_specs=[pl.BlockSpec(blk, cur), pl.BlockSpec(blk, cur)],
        out_shape=[jax.ShapeDtypeStruct((dil, n, GROUP_W), F32)] * 2,
        compiler_params=_params(2, 32), name=f"attn_prompt_d{dil}",
    )(q, k, k, v, v)


class _CacheJob(NamedTuple):
    dil: int
    lc: int
    part: str


JOB_ARITY = {"kv": (6, 3), "k": (4, 4), "v": (6, 2)}


def _cache_step(job, ins, outs, scr):
    dil, lc = job.dil, job.lc
    seq = pl.program_id(0)
    lane = seq & (LANES - 1)
    lane_h = lax.broadcasted_iota(jnp.int32, (HEADS, LANES), 1)
    last = lax.broadcasted_iota(jnp.int32, (GROUP_W, LANES), 1) == LANES - 1
    own = (lax.broadcasted_iota(jnp.int32, (HEADS, GROUP_W), 1) >> int(math.log2(HEAD_DIM))
           == lax.broadcasted_iota(jnp.int32, (HEADS, GROUP_W), 0))
    row = lambda ref: ref[pl.ds(seq, 1), :]
    new_col = lambda ref: pltpu.roll(ref[...], LANES - 1 - lane, 1)
    c_ref, out_ref = ins[0], outs[0]
    if dil > 1:
        sel_scr = scr[0]

        @pl.when(seq == 0)
        def _():
            pos = lax.broadcasted_iota(jnp.int32, (lc, BAND), 0)
            j = lax.broadcasted_iota(jnp.int32, (lc, BAND), 1)
            sel_scr[...] = jnp.where(pos == j * dil, 1.0, 0.0).astype(BF16)

    def picked(cin):
        if dil == 1:
            return cin.astype(BF16)
        return _dot(cin.astype(BF16), sel_scr[...]).astype(BF16)

    def shift_into(row0, cin, col127):
        nt = lc // LANES
        rot = [pltpu.roll(cin[:, t * LANES:(t + 1) * LANES], LANES - 1, 1) for t in range(nt)]
        for t in range(nt):
            nxt = rot[t + 1] if t + 1 < nt else col127
            out_ref[row0:row0 + GROUP_W, t * LANES:(t + 1) * LANES] = jnp.where(last, nxt, rot[t])

    def scores(q_ref, k_ref, kt_ref, l_ref):
        @pl.when(seq == 0)
        def _():
            l_ref[...] = jnp.zeros_like(l_ref)

        kin = c_ref[0:GROUP_W, :]
        shift_into(0, kin, new_col(kt_ref))
        q_heads = jnp.where(own, row(q_ref), 0.0)
        s = _dot(q_heads.astype(BF16), picked(kin))
        s_new = jnp.sum(q_heads * row(k_ref), axis=1, keepdims=True)
        m = jnp.maximum(jnp.max(s, axis=1, keepdims=True), s_new)
        p = jnp.exp2(s - m)
        pn = jnp.exp2(s_new - m)
        l = jnp.sum(p, axis=1, keepdims=True) + pn
        l_ref[...] = jnp.where(lane_h == lane, (m + jnp.log2(l)) * LN2, l_ref[...])
        return p / l, pn / l

    def attend(row0, p, pn, v_ref, vt_ref, o_ref):
        vin = c_ref[row0:row0 + GROUP_W, :]
        shift_into(row0, vin, new_col(vt_ref))
        o_all = lax.dot_general(p.astype(BF16), picked(vin), (((1,), (1,)), ((), ())),
                                preferred_element_type=F32)
        o_all = o_all + pn * row(v_ref)
        o_ref[pl.ds(seq, 1), :] = jnp.sum(jnp.where(own, o_all, 0.0), axis=0, keepdims=True)

    if job.part == "kv":
        _, q_ref, k_ref, v_ref, kt_ref, vt_ref = ins
        _, o_ref, l_ref = outs
        p, pn = scores(q_ref, k_ref, kt_ref, l_ref)
        attend(GROUP_W, p, pn, v_ref, vt_ref, o_ref)
    elif job.part == "k":
        _, q_ref, k_ref, kt_ref = ins
        _, p_ref, pn_ref, l_ref = outs
        p, pn = scores(q_ref, k_ref, kt_ref, l_ref)
        p_ref[...] = p
        pn_ref[...] = jnp.broadcast_to(pn, pn_ref.shape)
    else:
        _, v_ref, vt_ref, p_ref, pn_ref, _ = ins
        attend(0, p_ref[...], pn_ref[:, 0:1], v_ref, vt_ref, outs[1])


def _run_cache_jobs(jobs, in_refs, out_refs, scr_refs):
    i = o = k = 0
    for job in jobs:
        ni, no = JOB_ARITY[job.part]
        ns = 1 if job.dil > 1 else 0
        _cache_step(job, in_refs[i:i + ni], out_refs[o:o + no], scr_refs[k:k + ns])
        i, o, k = i + ni, o + no, k + ns


def _cache_job_specs(jobs, nb, nsteps):
    in_specs, out_specs, out_shape, scratch, aliases = [], [], [], [], []
    for job in jobs:
        assert nsteps == nb, "one cache unit per grid step"
        half = {"kv": None, "k": 0, "v": 1}[job.part]
        if half is None:
            blk = pl.BlockSpec((2 * GROUP_W, job.lc), lambda st: (st, 0))
        else:
            blk = pl.BlockSpec((GROUP_W, job.lc), functools.partial(lambda st, h: (2 * st + h, 0), h=half))
        rows = _const_spec((nb, GROUP_W))
        nbp = -(-nb // LANES) * LANES
        lane_blk = lambda st: (0, st // LANES)
        cols = pl.BlockSpec((GROUP_W, LANES), lane_blk)
        lse = pl.BlockSpec((HEADS, LANES), lane_blk)
        per_seq = pl.BlockSpec((None, HEADS, BAND), lambda st: (st, 0, 0))
        cache_shape = jax.ShapeDtypeStruct((nb * 2 * GROUP_W, job.lc), F32)
        rows_shape = jax.ShapeDtypeStruct((nb, GROUP_W), F32)
        per_seq_shape = jax.ShapeDtypeStruct((nb, HEADS, BAND), F32)
        if job.part == "kv":
            in_specs += [blk, rows, rows, rows, cols, cols]
            out_specs += [blk, rows, lse]
            out_shape += [cache_shape, rows_shape, jax.ShapeDtypeStruct((HEADS, nbp), F32)]
        elif job.part == "k":
            in_specs += [blk, rows, rows, cols]
            out_specs += [blk, per_seq, per_seq, lse]
            out_shape += [cache_shape, per_seq_shape, per_seq_shape,
                          jax.ShapeDtypeStruct((HEADS, nbp), F32)]
        else:
            aliases.append((len(in_specs) + 5, len(out_specs)))
            in_specs += [blk, rows, cols, per_seq, per_seq, pl.BlockSpec(memory_space=pl.ANY)]
            out_specs += [blk, rows]
            out_shape += [cache_shape, rows_shape]
        if job.dil > 1:
            scratch += [pltpu.VMEM((job.lc, BAND), BF16)]
    return in_specs, out_specs, out_shape, scratch, aliases


def _columns(t):
    return jnp.pad(t.T, ((0, 0), (0, -t.shape[0] % LANES)))


def _ssm_discretise(lam_re, lam_im, log_dt, b_re, b_im, c_re, c_im):
    lam = lax.complex(lam_re.astype(F32), lam_im.astype(F32))
    dt = jnp.exp(log_dt.astype(F32))[:, None]
    ldt = lam * dt
    lam_bar = jnp.exp(ldt)
    b_bar = ((lam_bar - 1.0) / lam)[..., None] * lax.complex(b_re.astype(F32), b_im.astype(F32))
    c = lax.complex(c_re.astype(F32), c_im.astype(F32))
    return ldt, lam_bar, b_bar, c


def _ssm_tables(ssm_p):
    ldt, _, b_bar, c = _ssm_discretise(*ssm_p)
    t, q, gq, pr = SSM_T, SSM_Q, SSM_GQ, SSM_PAIRS
    steps = jnp.arange(t + 1, dtype=F32)
    pw = jnp.exp(ldt[None] * steps[:, None, None])
    hi = lax.Precision.HIGHEST
    eye = jnp.eye(gq, dtype=F32)
    kern = jnp.einsum('gqn,kgn,gnp->gkpq', c, pw[:t], b_bar, precision=hi).real
    kern = kern.reshape(q, gq, t, SSM_P, SSM_P)
    d_, a_, b_ = jnp.meshgrid(jnp.arange(pr), jnp.arange(2), jnp.arange(2), indexing='ij')
    lag = 2 * d_ + b_ - a_
    kl = jnp.where((lag >= 0)[None, None, :, :, :, None, None], kern[:, :, jnp.maximum(lag, 0)], 0.0)
    toep = jnp.einsum('qgdabpr,gh->qdagpbhr', kl, eye).reshape(q, pr, 2 * LANES, 2 * LANES)
    win = pw[:t][::-1][:, :, :, None] * b_bar[None]
    win = win.reshape(pr, 2, q, gq, SSM_N, SSM_P).transpose(2, 0, 1, 3, 5, 4)
    w_in = jnp.concatenate([win.real, win.real, win.imag, win.imag], axis=-1)
    w_in = w_in.reshape(q, pr, 2 * LANES, 2 * LANES)
    cl = c[None] * pw[1:, :, None, :]
    cl = cl.reshape(pr, 2, q, gq, SSM_P, SSM_N).transpose(2, 0, 1, 3, 4, 5)
    w_out = jnp.concatenate([cl.real, cl.real, -cl.imag, -cl.imag], axis=-1)
    w_out = w_out.reshape(q, pr, 2 * LANES, 2 * LANES)
    return toep.astype(BF16), w_in.astype(BF16), w_out.astype(BF16), ldt


def _ssm_prompt_kernel(u_ref, toep_ref, win_ref, wout_ref, lp_ref, y_ref, hl_ref, carry_ref, win_bd,
                       wout_bd, *, nsteps):
    tt = pl.program_id(1)
    nc = u_ref.shape[0] // SSM_T
    sq = SSM_SQ

    @pl.when(tt == 0)
    def _():
        carry_ref[...] = jnp.zeros_like(carry_ref)
        r = lax.broadcasted_iota(jnp.int32, (2 * LANES, 2 * sq), 0)
        col = lax.broadcasted_iota(jnp.int32, (2 * LANES, 2 * sq), 1)
        own = (((r & (LANES - 1)) >> int(math.log2(SSM_P)))
               == ((col & (sq - 1)) >> int(math.log2(SSM_N))))
        for i in range(SSM_PAIRS):
            for src, dst in ((win_ref, win_bd), (wout_ref, wout_bd)):
                c = src[i]
                full = jnp.concatenate([c[:, :LANES]] * (sq // LANES) + [c[:, LANES:]] * (sq // LANES), axis=1)
                dst[i] = jnp.where(own, full, jnp.zeros_like(full))

    step = lambda t: u_ref[pl.ds(t, nc, stride=SSM_T), :].astype(BF16)
    a = [jnp.concatenate([step(2 * i), step(2 * i + 1)], axis=1) for i in range(SSM_PAIRS)]
    s = _dot(a[0], win_bd[0])
    for i in range(1, SSM_PAIRS):
        s = s + _dot(a[i], win_bd[i])
    xr, xi = s[:, :sq], s[:, sq:]
    row = lax.broadcasted_iota(jnp.int32, (nc, sq), 0)
    cr, ci = carry_ref[7:8, :sq], carry_ref[7:8, sq:]
    lr, li = lp_ref[0:1, :sq], lp_ref[0:1, sq:]
    xr = xr + jnp.where(row == 0, lr * cr - li * ci, 0.0)
    xi = xi + jnp.where(row == 0, lr * ci + li * cr, 0.0)

    def down(v, d):
        return jnp.where(row >= d, pltpu.roll(v, d, 0), 0.0)

    for k in range(nsteps):
        d = 1 << k
        lr, li = lp_ref[k:k + 1, :sq], lp_ref[k:k + 1, sq:]
        sr, si = down(xr, d), down(xi, d)
        xr, xi = xr + lr * sr - li * si, xi + lr * si + li * sr
    hr = jnp.where(row == 0, cr, pltpu.roll(xr, 1, 0))
    hi = jnp.where(row == 0, ci, pltpu.roll(xi, 1, 0))
    h_in = jnp.concatenate([hr, hi], axis=1).astype(BF16)
    last = jnp.concatenate([xr[nc - 8:], xi[nc - 8:]], axis=1)
    carry_ref[...] = last
    hl_ref[...] = last
    for j in range(SSM_PAIRS):
        acc = lax.dot_general(h_in, wout_bd[j], (((1,), (1,)), ((), ())), preferred_element_type=F32)
        for i in range(j + 1):
            acc = acc + _dot(a[i], toep_ref[j - i])
        y_ref[pl.ds(2 * j, nc, stride=SSM_T), :] = acc[:, :LANES]
        y_ref[pl.ds(2 * j + 1, nc, stride=SSM_T), :] = acc[:, LANES:]


def _ssm_prompt(u, tables):
    toep, w_in, w_out, ldt = tables
    s = u.shape[0]
    tile = min(s, 8192)
    nc = tile // SSM_T
    nsteps = int(math.log2(nc))
    scale = (SSM_T * (2.0 ** jnp.arange(nsteps, dtype=F32)))[:, None, None]
    lp = jnp.exp(ldt[None] * scale)
    lp = jnp.concatenate([lp.real.reshape(nsteps, SSM_Q, SSM_SQ), lp.imag.reshape(nsteps, SSM_Q, SSM_SQ)],
                         axis=2).transpose(1, 0, 2)
    per_q = lambda *shape: pl.BlockSpec((None,) + shape, lambda q, t: (q,) + (0,) * len(shape))
    y, hl = pl.pallas_call(
        functools.partial(_ssm_prompt_kernel, nsteps=nsteps),
        grid=(SSM_Q, s // tile),
        in_specs=[pl.BlockSpec((tile, LANES), lambda q, t: (t, q)),
                  per_q(SSM_PAIRS, 2 * LANES, 2 * LANES), per_q(SSM_PAIRS, 2 * LANES, 2 * LANES),
                  per_q(SSM_PAIRS, 2 * LANES, 2 * LANES), per_q(nsteps, 2 * SSM_SQ)],
        out_specs=[pl.BlockSpec((tile, LANES), lambda q, t: (t, q)), per_q(8, 2 * SSM_SQ)],
        out_shape=[jax.ShapeDtypeStruct((s, SSM_WIDTH), F32),
                   jax.ShapeDtypeStruct((SSM_Q, 8, 2 * SSM_SQ), F32)],
        scratch_shapes=[pltpu.VMEM((8, 2 * SSM_SQ), F32),
                        pltpu.VMEM((SSM_PAIRS, 2 * LANES, 2 * SSM_SQ), BF16),
                        pltpu.VMEM((SSM_PAIRS, 2 * LANES, 2 * SSM_SQ), BF16)],
        compiler_params=_params(2, 56), name="ssm_prompt",
    )(u, toep, w_in, w_out, lp)
    hl = hl[:, 7, :]
    return y, hl[:, :SSM_SQ].reshape(SSM_G, SSM_N), hl[:, SSM_SQ:].reshape(SSM_G, SSM_N)


def _ssm_sample_kernel(u_ref, hr_ref, hi_ref, lr_ref, li_ref, br_ref, bi_ref, cr_ref, ci_ref,
                       y_ref, or_ref, oi_ref):
    u = u_ref[...].astype(BF16)
    hr, hi = hr_ref[...], hi_ref[...]
    lr, li = lr_ref[...], li_ref[...]
    nr = lr * hr - li * hi + _dot(u, br_ref[...])
    ni = lr * hi + li * hr + _dot(u, bi_ref[...])
    or_ref[...] = nr
    oi_ref[...] = ni
    y_ref[...] = _dot(nr.astype(BF16), cr_ref[...]) + _dot(ni.astype(BF16), ci_ref[...])


def _ssm_sample(u, h_re, h_im, ssm_p):
    nb = u.shape[0]
    _, lam_bar, b_bar, c = _ssm_discretise(*ssm_p)
    eye = jnp.eye(SSM_G, dtype=F32)
    gn = SSM_G * SSM_N
    bd = lambda part, eq, shape: jnp.einsum(eq, part, eye).reshape(shape).astype(BF16)
    args = (u, h_re, h_im, lam_bar.real.reshape(1, gn), lam_bar.imag.reshape(1, gn),
            bd(b_bar.real, 'gnp,gh->gphn', (SSM_WIDTH, gn)), bd(b_bar.imag, 'gnp,gh->gphn', (SSM_WIDTH, gn)),
            bd(c.real, 'gpn,gh->gnhp', (gn, SSM_WIDTH)), bd(-c.imag, 'gpn,gh->gnhp', (gn, SSM_WIDTH)))
    return pl.pallas_call(
        _ssm_sample_kernel,
        grid=(1,),
        in_specs=[_const_spec(a.shape) for a in args],
        out_specs=[_const_spec((nb, SSM_WIDTH)), _const_spec((nb, gn)), _const_spec((nb, gn))],
        out_shape=[jax.ShapeDtypeStruct((nb, SSM_WIDTH), F32),
                   jax.ShapeDtypeStruct((nb, gn), F32), jax.ShapeDtypeStruct((nb, gn), F32)],
        compiler_params=_params(1, 32), name="ssm_sample",
    )(*args)


def _mix_kernel(x_ref, o1_ref, o2_ref, o3_ref, l1_ref, l2_ref, l3_ref, y_ref, u_ref, gs_ref, ga_ref,
                d_ref, wglu_ref, bglu_ref, wbs_ref, wba_ref, wout_ref, x1_ref, *scr, dils):
    tm = x_ref.shape[0]

    def natural(ref, dil, buf):
        if dil == 1:
            return ref[0]
        for r in range(dil):
            for t in range(N_TILES):
                buf[t, pl.ds(r, tm // dil, stride=dil), :] = ref[r, :, t * LANES:(t + 1) * LANES]
        return jnp.concatenate([buf[t] for t in range(N_TILES)], axis=1)

    o1, o2, o3 = (natural(r, d, scr[k]) for k, (r, d) in enumerate(zip((o1_ref, o2_ref, o3_ref), dils)))
    l1, l2, l3 = (natural(r, d, scr[3 + k]) for k, (r, d) in enumerate(zip((l1_ref, l2_ref, l3_ref), dils)))
    m = jnp.maximum(jnp.maximum(l1, l2), l3)
    e1, e2, e3 = jnp.exp(l1 - m), jnp.exp(l2 - m), jnp.exp(l3 - m)
    den = e1 + e2 + e3
    attn = (e1 / den) * o1 + (e2 / den) * o2 + (e3 / den) * o3
    y = jax.nn.gelu(y_ref[...] + d_ref[...] * u_ref[...])
    ssm_y = y * _sigmoid(_dot(y.astype(BF16), wglu_ref[...]) + bglu_ref[...])
    z = (_sigmoid(gs_ref[...]) * _dot(ssm_y.astype(BF16), wbs_ref[...])
         + _sigmoid(ga_ref[...]) * _dot(attn.astype(BF16), wba_ref[...]))
    x1_ref[...] = x_ref[...] + _dot(z.astype(BF16), wout_ref[...])


def _mix(x, os_, ls_, y_ssm, u, gs, ga, consts, tm):
    s = x.shape[0]
    row = lambda i: (i, 0)
    res = lambda i: (0, i, 0)
    dils = tuple(o.shape[0] for o in os_)
    grouped = [*os_, *ls_]
    flat = [y_ssm, u, gs, ga]
    return pl.pallas_call(
        functools.partial(_mix_kernel, dils=dils),
        grid=(s // tm,),
        in_specs=([pl.BlockSpec((tm, D_MODEL), row)]
                  + [pl.BlockSpec((a.shape[0], tm // a.shape[0], GROUP_W), res) for a in grouped]
                  + [pl.BlockSpec((tm, a.shape[1]), row) for a in flat]
                  + [_const_spec(c.shape) for c in consts]),
        out_specs=pl.BlockSpec((tm, D_MODEL), row),
        out_shape=jax.ShapeDtypeStruct((s, D_MODEL), F32),
        scratch_shapes=[pltpu.VMEM((N_TILES, tm, LANES), F32)] * 6,
        compiler_params=_params(1, 48), name="mix",
    )(x, *grouped, *flat, *consts)


def _mlp_kernel(*refs, n_phase, jobs):
    x1_ref, nm_ref, wup_ref, wdn_ref, nf_ref = refs[:5]
    n_in = 5 + sum(JOB_ARITY[j.part][0] for j in jobs)
    out_ref = refs[n_in]
    n_out = 1 + sum(JOB_ARITY[j.part][1] for j in jobs)
    up_scr = refs[n_in + n_out]
    _, in_phase = _phase_runner(n_phase)

    @in_phase(0)
    def _():
        hm = _rms(x1_ref[...], nm_ref[...]).astype(BF16)
        up = jnp.maximum(_dot(hm, wup_ref[...]), 0.0)
        up_scr[...] = (up * up).astype(BF16)

    @in_phase(1 % n_phase)
    def _():
        x2 = x1_ref[...] + _dot(up_scr[...], wdn_ref[...])
        out_ref[...] = _rms(x2, nf_ref[...])

    _run_cache_jobs(jobs, refs[5:n_in], refs[n_in + 1:n_in + n_out], refs[n_in + n_out + 1:])


def _mlp(x1, consts, tm, n_phase=1, jobs=(), job_args=()):
    s = x1.shape[0]
    ntiles = s // tm
    row = lambda st: (st // n_phase, 0)
    nb = job_args[0][1].shape[0] if jobs else 0
    job_in, job_out, job_shape, job_scr, job_alias = _cache_job_specs(jobs, nb, ntiles * n_phase)
    return pl.pallas_call(
        functools.partial(_mlp_kernel, n_phase=n_phase, jobs=jobs),
        grid=(ntiles * n_phase,),
        in_specs=[pl.BlockSpec((tm, D_MODEL), row)] + [_const_spec(c.shape) for c in consts] + job_in,
        out_specs=[pl.BlockSpec((tm, D_MODEL), row)] + job_out,
        out_shape=[jax.ShapeDtypeStruct((s, D_MODEL), F32)] + job_shape,
        scratch_shapes=[pltpu.VMEM((tm, consts[1].shape[1]), BF16)] + job_scr,
        input_output_aliases={5 + i: 1 + o for i, o in job_alias},
        compiler_params=_params(1, 58), name="mlp",
    )(x1, *consts, *[a for args in job_args for a in args])


def kernel(x_prompt, x_sample, cache_kv_g1, cache_kv_g2, cache_kv_g3, state_ssm, norm_mix, w_in,
           ssm_lambda_re, ssm_lambda_im, ssm_log_dt, ssm_b_re, ssm_b_im, ssm_c_re, ssm_c_im, ssm_d,
           w_glu, b_glu, w_branch_ssm, w_branch_attn, w_out, norm_mlp, w_up, w_down, norm_final):
    depth = w_in.shape[0]
    assert depth == 1, "one decoder layer"
    seq = x_prompt.shape[1]
    nb = x_sample.shape[0]
    assert x_prompt.shape[0] == 1 and x_sample.shape[1] == 1
    caches = (cache_kv_g1, cache_kv_g2, cache_kv_g3)
    windows = tuple(w for w, _ in ATTN_GROUPS)
    dils = tuple(d for _, d in ATTN_GROUPS)
    for c, w in zip(caches, windows):
        assert c.shape[2] == w, "cache holds exactly one window"

    w_in_bf = w_in[0].astype(BF16)
    gain_mix = norm_mix[0][None, :]
    mix_consts = (ssm_d[0][None, :], w_glu[0].astype(BF16), b_glu[0][None, :],
                  w_branch_ssm[0].astype(BF16), w_branch_attn[0].astype(BF16), w_out[0].astype(BF16))
    mlp_consts = (norm_mlp[0][None, :], w_up[0].astype(BF16), w_down[0].astype(BF16), norm_final[None, :])
    ssm_p = (ssm_lambda_re[0], ssm_lambda_im[0], ssm_log_dt[0], ssm_b_re[0], ssm_b_im[0],
             ssm_c_re[0], ssm_c_im[0])

    xs = x_sample[:, 0]
    res = _inproj(xs, gain_mix, w_in_bf, jnp.full((1,), PAST_LEN), jnp.zeros((nb,)), nb,
                  (nb,) * N_GROUPS, (1,) * N_GROUPS)
    qs_s, tails_s = res[0:3], res[9:12]
    u_s, gs_s, ga_s = res[12:15]

    cache_t = [c[0].transpose(0, 2, 3, 4, 1).reshape(nb * 2 * GROUP_W, c.shape[2]) for c in caches]
    q_s = [q[0].astype(F32) for q in qs_s]
    k_s = [t[:, :GROUP_W] for t in tails_s]
    v_s = [t[:, GROUP_W:] for t in tails_s]
    job = lambda g, part: _CacheJob(dils[g], caches[g].shape[2], part)

    xp = x_prompt[0]
    tm_p = 256
    res = _inproj(xp, gain_mix, w_in_bf, jnp.arange(seq // tm_p) * tm_p, jnp.arange(tm_p), tm_p,
                  windows, dils, n_phase=2, jobs=(job(2, "k"),),
                  job_args=((cache_t[2], q_s[2], k_s[2], _columns(k_s[2])),))
    qs, ks, vs, tails = res[0:3], res[3:6], res[6:9], res[9:12]
    u_p, gs_p, ga_p = res[12:15]
    cache3_k, p3, pn3, lse3 = res[15:19]
    os_p, ls_p = zip(*[_attn_prompt(qs[g], ks[g], vs[g]) for g in range(N_GROUPS)])
    y_p, hp_re, hp_im = _ssm_prompt(u_p, _ssm_tables(ssm_p))
    x1_p = _mix(xp, os_p, ls_p, y_p, u_p, gs_p, ga_p, mix_consts, tm_p)
    kv_args = lambda g: (cache_t[g], q_s[g], k_s[g], v_s[g], _columns(k_s[g]), _columns(v_s[g]))
    res = _mlp(x1_p, mlp_consts, tm_p, n_phase=2, jobs=(job(2, "v"), job(1, "kv"), job(0, "kv")),
               job_args=((cache_t[2], v_s[2], _columns(v_s[2]), p3, pn3, cache3_k), kv_args(1), kv_args(0)))
    y_prompt = res[0][None]
    new_caches = (res[6], res[3], res[1])
    os_s = (res[7], res[4], res[2])
    lses = (res[8], res[5], lse3)
    kv_prompt = [t.reshape(1, 1, w, 2, HEADS, HEAD_DIM) for t, w in zip(tails, windows)]
    ssm_prompt = jnp.stack([hp_re, hp_im], axis=-1)[None, None].astype(state_ssm.dtype)

    kv_sample = [t.reshape(nb, 2, HEADS, HEAD_DIM, t.shape[1]).transpose(0, 4, 1, 2, 3)[None]
                 for t in new_caches]
    os_s = [o[None] for o in os_s]
    ls_s = [jnp.repeat(l[:, :nb].T, HEAD_DIM, axis=1)[None] for l in lses]
    st = state_ssm[0].astype(F32)
    gn = SSM_G * SSM_N
    y_s, hs_re, hs_im = _ssm_sample(u_s, st[..., 0].reshape(nb, gn), st[..., 1].reshape(nb, gn), ssm_p)
    x1_s = _mix(xs, os_s, ls_s, y_s, u_s, gs_s, ga_s, mix_consts, nb)
    y_sample = _mlp(x1_s, mlp_consts, nb)[0][:, None]
    ssm_sample = jnp.stack([hs_re.reshape(nb, SSM_G, SSM_N), hs_im.reshape(nb, SSM_G, SSM_N)],
                           axis=-1)[None].astype(state_ssm.dtype)

    return (y_prompt, y_sample, kv_prompt[0], kv_prompt[1], kv_prompt[2], ssm_prompt,
            kv_sample[0], kv_sample[1], kv_sample[2], ssm_sample)
```

```python
import functools
import math
from typing import NamedTuple

import jax
import jax.numpy as jnp
from jax import lax
from jax.experimental import pallas as pl
from jax.experimental.pallas import tpu as pltpu

F32 = jnp.float32
BF16 = jnp.bfloat16

D_MODEL = 1024
HEAD_DIM = 64
HEADS = 8
GROUP_W = HEADS * HEAD_DIM
ATTN_GROUPS = ((128, 1), (512, 4), (2048, 16))
N_GROUPS = len(ATTN_GROUPS)
QKV_W = N_GROUPS * GROUP_W
BAND = 128
Q_BLOCK = 512
ROT_DIM = HEAD_DIM // 4
ROT_HALF = ROT_DIM // 2
ROPE_THETA = 500000.0
PAST_LEN = 8192
SSM_WIDTH = 512
SSM_P = 16
SSM_G = SSM_WIDTH // SSM_P
SSM_N = 64
SSM_T = 16
EPS = 1e-6
NEG = -1e30
LN2 = math.log(2.0)
Q_SCALE = HEAD_DIM ** -0.5 * math.log2(math.e)
LANES = 128
N_TILES = GROUP_W // LANES
MIB = 1024 * 1024

SSM_Q = SSM_WIDTH // LANES
SSM_GQ = SSM_G // SSM_Q
SSM_SQ = SSM_GQ * SSM_N
SSM_PAIRS = SSM_T // 2

OFF_Q, OFF_K, OFF_V = 0, QKV_W, 2 * QKV_W
OFF_U = 3 * QKV_W
OFF_GS = OFF_U + SSM_WIDTH
OFF_GA = OFF_GS + D_MODEL
IN_W = OFF_GA + D_MODEL


def _const_spec(shape):
    nd = len(shape)
    return pl.BlockSpec(shape, lambda *_: (0,) * nd, pipeline_mode=pl.Buffered(1))


def _params(n_axes, vmem_mib):
    return pltpu.CompilerParams(dimension_semantics=("arbitrary",) * n_axes,
                                vmem_limit_bytes=vmem_mib * MIB)


def _sigmoid(x):
    return 1.0 / (1.0 + jnp.exp(-x))


def _rms(x, g):
    return x * lax.rsqrt(jnp.mean(x * x, axis=-1, keepdims=True) + EPS) * g


def _dot(a, b):
    return jnp.dot(a, b, preferred_element_type=F32)


def _phase_runner(n_phase):
    step = pl.program_id(0)
    if n_phase == 1:
        return step, lambda ph: (lambda fn: fn())
    phase = step & (n_phase - 1)
    return lax.shift_right_logical(step, int(math.log2(n_phase))), lambda ph: pl.when(phase == ph)


def _inproj_kernel(*refs, tm, dils, tail_rows, first_tail, n_phase, jobs):
    n_in = 7 + sum(JOB_ARITY[j.part][0] for j in jobs)
    x_ref, g_ref, w_ref, ca_ref, sa_ref, cb_ref, sb_ref = refs[:7]
    outs = refs[n_in:n_in + 15]
    q_refs, k_refs, v_refs, tail_refs = outs[0:3], outs[3:6], outs[6:9], outs[9:12]
    u_ref, gs_ref, ga_ref = outs[12:15]
    n_out = 15 + sum(JOB_ARITY[j.part][1] for j in jobs)
    scr, h_scr, rot_scr = refs[n_in + n_out:n_in + n_out + 3]
    i, in_phase = _phase_runner(n_phase)

    def rot(a):
        cos, s1, s2 = rot_scr[0], rot_scr[1], rot_scr[2]
        parts = []
        for j in range(GROUP_W // LANES):
            seg = a[:, j * LANES:(j + 1) * LANES]
            parts.append(seg * cos + pltpu.roll(seg, LANES - ROT_HALF, 1) * s1
                         + pltpu.roll(seg, ROT_HALF, 1) * s2)
        return jnp.concatenate(parts, axis=1)

    def put(out_ref, val, dil):
        if dil == 1:
            out_ref[0] = val.astype(BF16)
            return
        for t in range(N_TILES):
            scr[t] = val[:, t * LANES:(t + 1) * LANES]
        for r in range(dil):
            rows = [scr[t, pl.ds(r, tm // dil, stride=dil), :] for t in range(N_TILES)]
            out_ref[r] = jnp.concatenate(rows, axis=1).astype(BF16)

    def proj(off, gi):
        c0 = off + gi * GROUP_W
        return _dot(h_scr[...], w_ref[:, c0:c0 + GROUP_W])

    def put_tail(gi, half, val):
        tb = tail_rows[gi]

        @pl.when(i >= first_tail[gi])
        def _():
            tail_refs[gi][:, half * GROUP_W:(half + 1) * GROUP_W] = val[tm - tb:, :]

    job_refs = (refs[7:n_in], refs[n_in + 15:n_in + n_out], refs[n_in + n_out + 3:])

    @pl.when(pl.program_id(0) == 0)
    def _():
        _run_cache_jobs(jobs, *job_refs, init=True)

    def ride(block):
        if block < n_phase:
            _run_cache_jobs(jobs, *job_refs)

    @in_phase(0)
    def _():
        h_scr[...] = _rms(x_ref[...], g_ref[...]).astype(BF16)
        ca, sa = ca_ref[pl.ds(i, 1), :], sa_ref[pl.ds(i, 1), :]
        cb, sb = cb_ref[...], sb_ref[...]
        cos_f = ca * cb - sa * sb
        sin_f = sa * cb + ca * sb
        hd = lax.broadcasted_iota(jnp.int32, (tm, LANES), 1) & (HEAD_DIM - 1)
        rot_scr[0] = jnp.where(hd < ROT_DIM, cos_f, 1.0)
        rot_scr[1] = jnp.where(hd < ROT_HALF, -sin_f, 0.0)
        rot_scr[2] = jnp.where((hd >= ROT_HALF) & (hd < ROT_DIM), sin_f, 0.0)
        for gi in range(N_GROUPS):
            put(q_refs[gi], rot(proj(OFF_Q, gi)) * Q_SCALE, dils[gi])
        ride(0)

    @in_phase(1 % n_phase)
    def _():
        for gi in range(N_GROUPS):
            ka = rot(proj(OFF_K, gi))
            put(k_refs[gi], ka, dils[gi])
            put_tail(gi, 0, ka)
        ride(1)

    @in_phase(2 % n_phase)
    def _():
        for gi in range(N_GROUPS):
            va = proj(OFF_V, gi)
            put(v_refs[gi], va, dils[gi])
            put_tail(gi, 1, va)
        ride(2)

    @in_phase(3 % n_phase)
    def _():
        h = h_scr[...]
        u_ref[...] = _dot(h, w_ref[:, OFF_U:OFF_GS])
        gs_ref[...] = _dot(h, w_ref[:, OFF_GS:OFF_GA]).astype(BF16)
        ga_ref[...] = _dot(h, w_ref[:, OFF_GA:IN_W]).astype(BF16)
        ride(3)


def _inproj(x, gain, w_bf, tile_pos, row_pos, tm, windows, dils, n_phase=1, jobs=(), job_args=()):
    s = x.shape[0]
    ntiles = s // tm
    inv_freq = jnp.exp(-math.log(ROPE_THETA) * jnp.arange(ROT_HALF, dtype=F32) * (2.0 / ROT_DIM))
    lane_freq = jnp.tile(inv_freq, LANES // ROT_HALF)[None, :]
    ang_a = tile_pos.astype(F32)[:, None] * lane_freq
    ang_b = row_pos.astype(F32)[:, None] * lane_freq
    tail_rows = tuple(min(w, tm) for w in windows)
    first_tail = tuple(ntiles - w // tb for w, tb in zip(windows, tail_rows))
    row = lambda st: (st // n_phase, 0)
    res = lambda st: (0, st // n_phase, 0)
    qkv_shapes = [jax.ShapeDtypeStruct((d, s // d, GROUP_W), BF16) for d in dils]
    qkv_specs = [pl.BlockSpec((d, tm // d, GROUP_W), res) for d in dils]
    out_shape = (qkv_shapes * 3
                 + [jax.ShapeDtypeStruct((w, 2 * GROUP_W), F32) for w in windows]
                 + [jax.ShapeDtypeStruct((s, SSM_WIDTH), F32),
                    jax.ShapeDtypeStruct((s, D_MODEL), BF16),
                    jax.ShapeDtypeStruct((s, D_MODEL), BF16)])
    out_specs = (qkv_specs * 3
                 + [pl.BlockSpec((tb, 2 * GROUP_W), functools.partial(
                     lambda st, ft: (jnp.maximum(st // n_phase - ft, 0), 0), ft=ft))
                    for tb, ft in zip(tail_rows, first_tail)]
                 + [pl.BlockSpec((tm, SSM_WIDTH), row),
                    pl.BlockSpec((tm, D_MODEL), row),
                    pl.BlockSpec((tm, D_MODEL), row)])
    nb = job_args[0][1].shape[0] if jobs else 0
    job_in, job_out, job_shape, job_scr, job_alias = _cache_job_specs(jobs, nb, ntiles * n_phase)
    kern = functools.partial(_inproj_kernel, tm=tm, dils=dils, tail_rows=tail_rows,
                             first_tail=first_tail, n_phase=n_phase, jobs=jobs)
    return pl.pallas_call(
        kern,
        grid=(ntiles * n_phase,),
        in_specs=[pl.BlockSpec((tm, D_MODEL), row), _const_spec((1, D_MODEL)),
                  _const_spec((D_MODEL, IN_W)),
                  _const_spec((ntiles, LANES)), _const_spec((ntiles, LANES)),
                  _const_spec((tm, LANES)), _const_spec((tm, LANES))] + job_in,
        out_specs=out_specs + job_out, out_shape=out_shape + job_shape,
        scratch_shapes=[pltpu.VMEM((N_TILES, tm, LANES), F32), pltpu.VMEM((tm, D_MODEL), BF16),
                        pltpu.VMEM((3, tm, LANES), F32)] + job_scr,
        input_output_aliases={7 + i: 15 + o for i, o in job_alias},
        compiler_params=_params(1, 58), name="inproj",
    )(x, gain, w_bf, jnp.cos(ang_a), jnp.sin(ang_a), jnp.cos(ang_b), jnp.sin(ang_b),
      *[a for args in job_args for a in args])


def _attn_prompt_kernel(q_ref, kp_ref, kc_ref, vp_ref, vc_ref, o_ref, l_ref):
    j = pl.program_id(1)
    lane = lax.broadcasted_iota(jnp.int32, (BAND, LANES), 1)
    head_a = lane < HEAD_DIM
    qi = lax.broadcasted_iota(jnp.int32, (BAND, 2 * BAND), 0)
    kj = lax.broadcasted_iota(jnp.int32, (BAND, 2 * BAND), 1)
    band = (kj >= qi) & (kj <= qi + BAND)
    bias = jnp.where(band, 0.0, NEG)
    bias_first = jnp.where(band & (kj >= BAND), 0.0, NEG)
    for t in range(q_ref.shape[0] // BAND):
        rows = slice(t * BAND, (t + 1) * BAND)
        b = bias if t > 0 else jnp.where(j > 0, bias, bias_first)
        for hp in range(N_TILES):
            sl = slice(hp * LANES, (hp + 1) * LANES)
            q2 = q_ref[rows, sl]
            if t == 0:
                kk = jnp.concatenate([kp_ref[:, sl], kc_ref[0:BAND, sl]], axis=0)
                vv = jnp.concatenate([vp_ref[:, sl], vc_ref[0:BAND, sl]], axis=0)
            else:
                kk = kc_ref[(t - 1) * BAND:(t + 1) * BAND, sl]
                vv = vc_ref[(t - 1) * BAND:(t + 1) * BAND, sl]
            zero = jnp.zeros_like(q2)
            res = []
            for qm in (jnp.where(head_a, q2, zero), jnp.where(head_a, zero, q2)):
                s = lax.dot_general(qm, kk, (((1,), (1,)), ((), ())), preferred_element_type=F32) + b
                m = jnp.max(s, axis=1, keepdims=True)
                p = jnp.exp2(s - m)
                l = jnp.sum(p, axis=1, keepdims=True)
                o = _dot(p.astype(BF16), vv) * (1.0 / l)
                res.append((o, (m + jnp.log2(l)) * LN2))
            o_ref[rows, sl] = jnp.where(head_a, res[0][0], res[1][0]).astype(BF16)
            l_ref[rows, sl] = jnp.where(head_a, res[0][1], res[1][1])


def _attn_prompt(q, k, v):
    dil, n, _ = q.shape
    qb = min(Q_BLOCK, n)
    cur = lambda r, j: (r, j, 0)
    prev = lambda r, j: (r, jnp.maximum(j * (qb // BAND) - 1, 0), 0)
    blk = (None, qb, GROUP_W)
    hist = (None, BAND, GROUP_W)
    return pl.pallas_call(
        _attn_prompt_kernel,
        grid=(dil, n // qb),
        in_specs=[pl.BlockSpec(blk, cur), pl.BlockSpec(hist, prev), pl.BlockSpec(blk, cur),
                  pl.BlockSpec(hist, prev), pl.BlockSpec(blk, cur)],
        out_specs=[pl.BlockSpec(blk, cur), pl.BlockSpec(blk, cur)],
        out_shape=[jax.ShapeDtypeStruct((dil, n, GROUP_W), BF16),
                   jax.ShapeDtypeStruct((dil, n, GROUP_W), F32)],
        compiler_params=_params(2, 32), name=f"attn_prompt_d{dil}",
    )(q, k, k, v, v)


class _CacheJob(NamedTuple):
    dil: int
    lc: int
    part: str


JOB_ARITY = {"kv": (6, 3), "k": (4, 4), "v": (6, 2)}


def _cache_step(job, ins, outs, scr, init):
    dil, lc = job.dil, job.lc
    seq = pl.program_id(0)
    lane = seq & (LANES - 1)
    lane_h = lax.broadcasted_iota(jnp.int32, (HEADS, LANES), 1)
    last = lax.broadcasted_iota(jnp.int32, (GROUP_W, LANES), 1) == LANES - 1
    own = (lax.broadcasted_iota(jnp.int32, (HEADS, GROUP_W), 1) >> int(math.log2(HEAD_DIM))
           == lax.broadcasted_iota(jnp.int32, (HEADS, GROUP_W), 0))
    row = lambda ref: ref[pl.ds(seq, 1), :]
    new_col = lambda ref: pltpu.roll(ref[...], LANES - 1 - lane, 1)
    c_ref, out_ref = ins[0], outs[0]
    sel_scr = scr[0] if dil > 1 else None
    if init:
        if dil > 1:
            pos = lax.broadcasted_iota(jnp.int32, (lc, BAND), 0)
            j = lax.broadcasted_iota(jnp.int32, (lc, BAND), 1)
            sel_scr[...] = jnp.where(pos == j * dil, 1.0, 0.0).astype(BF16)
        if job.part != "v":
            outs[-1][...] = jnp.zeros_like(outs[-1])
        return

    def picked(cin):
        if dil == 1:
            return cin.astype(BF16)
        return _dot(cin.astype(BF16), sel_scr[...]).astype(BF16)

    def shift_into(row0, cin, col127):
        nt = lc // LANES
        rot = [pltpu.roll(cin[:, t * LANES:(t + 1) * LANES], LANES - 1, 1) for t in range(nt)]
        for t in range(nt):
            nxt = rot[t + 1] if t + 1 < nt else col127
            out_ref[row0:row0 + GROUP_W, t * LANES:(t + 1) * LANES] = jnp.where(last, nxt, rot[t])

    def scores(q_ref, k_ref, kt_ref, l_ref):
        kin = c_ref[0:GROUP_W, :]
        shift_into(0, kin, new_col(kt_ref))
        q_heads = jnp.where(own, row(q_ref), 0.0)
        s = _dot(q_heads.astype(BF16), picked(kin))
        s_new = jnp.sum(q_heads * row(k_ref), axis=1, keepdims=True)
        m = jnp.maximum(jnp.max(s, axis=1, keepdims=True), s_new)
        p = jnp.exp2(s - m)
        pn = jnp.exp2(s_new - m)
        l = jnp.sum(p, axis=1, keepdims=True) + pn
        l_ref[...] = jnp.where(lane_h == lane, (m + jnp.log2(l)) * LN2, l_ref[...])
        return p / l, pn / l

    def attend(row0, p, pn, v_ref, vt_ref, o_ref):
        vin = c_ref[row0:row0 + GROUP_W, :]
        shift_into(row0, vin, new_col(vt_ref))
        o_all = lax.dot_general(p.astype(BF16), picked(vin), (((1,), (1,)), ((), ())),
                                preferred_element_type=F32)
        o_all = o_all + pn * row(v_ref)
        o_ref[pl.ds(seq, 1), :] = jnp.sum(jnp.where(own, o_all, 0.0), axis=0, keepdims=True)

    if job.part == "kv":
        _, q_ref, k_ref, v_ref, kt_ref, vt_ref = ins
        _, o_ref, l_ref = outs
        p, pn = scores(q_ref, k_ref, kt_ref, l_ref)
        attend(GROUP_W, p, pn, v_ref, vt_ref, o_ref)
    elif job.part == "k":
        _, q_ref, k_ref, kt_ref = ins
        _, p_ref, pn_ref, l_ref = outs
        p, pn = scores(q_ref, k_ref, kt_ref, l_ref)
        p_ref[...] = p
        pn_ref[...] = jnp.broadcast_to(pn, pn_ref.shape)
    else:
        _, v_ref, vt_ref, p_ref, pn_ref, _ = ins
        attend(0, p_ref[...], pn_ref[:, 0:1], v_ref, vt_ref, outs[1])


def _run_cache_jobs(jobs, in_refs, out_refs, scr_refs, init=False):
    i = o = k = 0
    for job in jobs:
        ni, no = JOB_ARITY[job.part]
        ns = 1 if job.dil > 1 else 0
        _cache_step(job, in_refs[i:i + ni], out_refs[o:o + no], scr_refs[k:k + ns], init)
        i, o, k = i + ni, o + no, k + ns


def _cache_job_specs(jobs, nb, nsteps):
    in_specs, out_specs, out_shape, scratch, aliases = [], [], [], [], []
    for job in jobs:
        assert nsteps == nb, "one cache unit per grid step"
        half = {"kv": None, "k": 0, "v": 1}[job.part]
        if half is None:
            blk = pl.BlockSpec((2 * GROUP_W, job.lc), lambda st: (st, 0))
        else:
            blk = pl.BlockSpec((GROUP_W, job.lc), functools.partial(lambda st, h: (2 * st + h, 0), h=half))
        rows = _const_spec((nb, GROUP_W))
        nbp = -(-nb // LANES) * LANES
        lane_blk = lambda st: (0, st // LANES)
        cols = pl.BlockSpec((GROUP_W, LANES), lane_blk)
        lse = pl.BlockSpec((HEADS, LANES), lane_blk)
        per_seq = pl.BlockSpec((None, HEADS, BAND), lambda st: (st, 0, 0))
        cache_shape = jax.ShapeDtypeStruct((nb * 2 * GROUP_W, job.lc), F32)
        rows_shape = jax.ShapeDtypeStruct((nb, GROUP_W), F32)
        per_seq_shape = jax.ShapeDtypeStruct((nb, HEADS, BAND), F32)
        if job.part == "kv":
            in_specs += [blk, rows, rows, rows, cols, cols]
            out_specs += [blk, rows, lse]
            out_shape += [cache_shape, rows_shape, jax.ShapeDtypeStruct((HEADS, nbp), F32)]
        elif job.part == "k":
            in_specs += [blk, rows, rows, cols]
            out_specs += [blk, per_seq, per_seq, lse]
            out_shape += [cache_shape, per_seq_shape, per_seq_shape,
                          jax.ShapeDtypeStruct((HEADS, nbp), F32)]
        else:
            aliases.append((len(in_specs) + 5, len(out_specs)))
            in_specs += [blk, rows, cols, per_seq, per_seq, pl.BlockSpec(memory_space=pl.ANY)]
            out_specs += [blk, rows]
            out_shape += [cache_shape, rows_shape]
        if job.dil > 1:
            scratch += [pltpu.VMEM((job.lc, BAND), BF16)]
    return in_specs, out_specs, out_shape, scratch, aliases


def _columns(t):
    return jnp.pad(t.T, ((0, 0), (0, -t.shape[0] % LANES)))


def _ssm_discretise(lam_re, lam_im, log_dt, b_re, b_im, c_re, c_im):
    lam = lax.complex(lam_re.astype(F32), lam_im.astype(F32))
    dt = jnp.exp(log_dt.astype(F32))[:, None]
    ldt = lam * dt
    lam_bar = jnp.exp(ldt)
    b_bar = ((lam_bar - 1.0) / lam)[..., None] * lax.complex(b_re.astype(F32), b_im.astype(F32))
    c = lax.complex(c_re.astype(F32), c_im.astype(F32))
    return ldt, lam_bar, b_bar, c


def _ssm_tables(ssm_p):
    ldt, _, b_bar, c = _ssm_discretise(*ssm_p)
    t, q, gq, pr = SSM_T, SSM_Q, SSM_GQ, SSM_PAIRS
    steps = jnp.arange(t + 1, dtype=F32)
    pw = jnp.exp(ldt[None] * steps[:, None, None])
    hi = lax.Precision.HIGHEST
    eye = jnp.eye(gq, dtype=F32)
    kern = jnp.einsum('gqn,kgn,gnp->gkpq', c, pw[:t], b_bar, precision=hi).real
    kern = kern.reshape(q, gq, t, SSM_P, SSM_P)
    d_, a_, b_ = jnp.meshgrid(jnp.arange(pr), jnp.arange(2), jnp.arange(2), indexing='ij')
    lag = 2 * d_ + b_ - a_
    kl = jnp.where((lag >= 0)[None, None, :, :, :, None, None], kern[:, :, jnp.maximum(lag, 0)], 0.0)
    toep = kl.transpose(0, 2, 3, 1, 5, 4, 6).reshape(q, pr, 2 * LANES, 2 * SSM_P)
    win = pw[:t][::-1][:, :, :, None] * b_bar[None]
    win = win.reshape(pr, 2, q, gq, SSM_N, SSM_P).transpose(2, 0, 1, 3, 5, 4)
    w_in = jnp.concatenate([win.real, win.real, win.imag, win.imag], axis=-1)
    w_in = w_in.reshape(q, pr, 2 * LANES, 2 * LANES)
    cl = c[None] * pw[1:, :, None, :]
    cl = cl.reshape(pr, 2, q, gq, SSM_P, SSM_N).transpose(2, 0, 1, 3, 4, 5)
    w_out = jnp.concatenate([cl.real, cl.real, -cl.imag, -cl.imag], axis=-1)
    w_out = w_out.reshape(q, pr, 2 * LANES, 2 * LANES)
    return toep.astype(BF16), w_in.astype(BF16), w_out.astype(BF16), ldt


def _ssm_prompt_kernel(u_ref, toep_ref, win_ref, wout_ref, lp_ref, y_ref, hl_ref, carry_ref, win_bd,
                       wout_bd, toep_bd, *, nsteps):
    tt = pl.program_id(1)
    nc = u_ref.shape[0] // SSM_T
    sq = SSM_SQ

    @pl.when(tt == 0)
    def _():
        carry_ref[...] = jnp.zeros_like(carry_ref)
        r = lax.broadcasted_iota(jnp.int32, (2 * LANES, 2 * sq), 0)
        col = lax.broadcasted_iota(jnp.int32, (2 * LANES, 2 * sq), 1)
        own = (((r & (LANES - 1)) >> int(math.log2(SSM_P)))
               == ((col & (sq - 1)) >> int(math.log2(SSM_N))))
        for i in range(SSM_PAIRS):
            for src, dst in ((win_ref, win_bd), (wout_ref, wout_bd)):
                c = src[i]
                full = jnp.concatenate([c[:, :LANES]] * (sq // LANES) + [c[:, LANES:]] * (sq // LANES), axis=1)
                dst[i] = jnp.where(own, full, jnp.zeros_like(full))
        er = lax.broadcasted_iota(jnp.int32, (2 * SSM_P, 2 * LANES), 0)
        ec = lax.broadcasted_iota(jnp.int32, (2 * SSM_P, 2 * LANES), 1)
        lg_p, lg_l = int(math.log2(SSM_P)), int(math.log2(LANES))
        spread = jnp.where(((er >> lg_p) == (ec >> lg_l)) & ((er & (SSM_P - 1)) == (ec & (SSM_P - 1))),
                           1.0, 0.0).astype(BF16)
        r2 = lax.broadcasted_iota(jnp.int32, (2 * LANES, 2 * LANES), 0)
        c2 = lax.broadcasted_iota(jnp.int32, (2 * LANES, 2 * LANES), 1)
        own2 = ((r2 & (LANES - 1)) >> lg_p) == ((c2 & (LANES - 1)) >> lg_p)
        for d in range(SSM_PAIRS):
            toep_bd[d] = jnp.where(own2, _dot(toep_ref[d], spread), 0.0).astype(BF16)

    step = lambda t: u_ref[pl.ds(t, nc, stride=SSM_T), :].astype(BF16)
    a = [jnp.concatenate([step(2 * i), step(2 * i + 1)], axis=1) for i in range(SSM_PAIRS)]
    s = _dot(a[0], win_bd[0])
    for i in range(1, SSM_PAIRS):
        s = s + _dot(a[i], win_bd[i])
    xr, xi = s[:, :sq], s[:, sq:]
    row = lax.broadcasted_iota(jnp.int32, (nc, sq), 0)
    cr, ci = carry_ref[7:8, :sq], carry_ref[7:8, sq:]
    lr, li = lp_ref[0:1, :sq], lp_ref[0:1, sq:]
    xr = xr + jnp.where(row == 0, lr * cr - li * ci, 0.0)
    xi = xi + jnp.where(row == 0, lr * ci + li * cr, 0.0)

    def down(v, d):
        return jnp.where(row >= d, pltpu.roll(v, d, 0), 0.0)

    for k in range(nsteps):
        d = 1 << k
        lr, li = lp_ref[k:k + 1, :sq], lp_ref[k:k + 1, sq:]
        sr, si = down(xr, d), down(xi, d)
        xr, xi = xr + lr * sr - li * si, xi + lr * si + li * sr
    hr = jnp.where(row == 0, cr, pltpu.roll(xr, 1, 0))
    hi = jnp.where(row == 0, ci, pltpu.roll(xi, 1, 0))
    h_in = jnp.concatenate([hr, hi], axis=1).astype(BF16)
    last = jnp.concatenate([xr[nc - 8:], xi[nc - 8:]], axis=1)
    carry_ref[...] = last
    hl_ref[...] = last
    for j in range(SSM_PAIRS):
        acc = lax.dot_general(h_in, wout_bd[j], (((1,), (1,)), ((), ())), preferred_element_type=F32)
        for i in range(j + 1):
            acc = acc + _dot(a[i], toep_bd[j - i])
        y_ref[pl.ds(2 * j, nc, stride=SSM_T), :] = acc[:, :LANES]
        y_ref[pl.ds(2 * j + 1, nc, stride=SSM_T), :] = acc[:, LANES:]


def _ssm_prompt(u, tables):
    toep, w_in, w_out, ldt = tables
    s = u.shape[0]
    tile = min(s, 8192)
    nc = tile // SSM_T
    nsteps = int(math.log2(nc))
    scale = (SSM_T * (2.0 ** jnp.arange(nsteps, dtype=F32)))[:, None, None]
    lp = jnp.exp(ldt[None] * scale)
    lp = jnp.concatenate([lp.real.reshape(nsteps, SSM_Q, SSM_SQ), lp.imag.reshape(nsteps, SSM_Q, SSM_SQ)],
                         axis=2).transpose(1, 0, 2)
    per_q = lambda *shape: pl.BlockSpec((None,) + shape, lambda q, t: (q,) + (0,) * len(shape))
    y, hl = pl.pallas_call(
        functools.partial(_ssm_prompt_kernel, nsteps=nsteps),
        grid=(SSM_Q, s // tile),
        in_specs=[pl.BlockSpec((tile, LANES), lambda q, t: (t, q)),
                  per_q(SSM_PAIRS, 2 * LANES, 2 * SSM_P), per_q(SSM_PAIRS, 2 * LANES, 2 * LANES),
                  per_q(SSM_PAIRS, 2 * LANES, 2 * LANES), per_q(nsteps, 2 * SSM_SQ)],
        out_specs=[pl.BlockSpec((tile, LANES), lambda q, t: (t, q)), per_q(8, 2 * SSM_SQ)],
        out_shape=[jax.ShapeDtypeStruct((s, SSM_WIDTH), F32),
                   jax.ShapeDtypeStruct((SSM_Q, 8, 2 * SSM_SQ), F32)],
        scratch_shapes=[pltpu.VMEM((8, 2 * SSM_SQ), F32),
                        pltpu.VMEM((SSM_PAIRS, 2 * LANES, 2 * SSM_SQ), BF16),
                        pltpu.VMEM((SSM_PAIRS, 2 * LANES, 2 * SSM_SQ), BF16),
                        pltpu.VMEM((SSM_PAIRS, 2 * LANES, 2 * LANES), BF16)],
        compiler_params=_params(2, 56), name="ssm_prompt",
    )(u, toep, w_in, w_out, lp)
    hl = hl[:, 7, :]
    return y, hl[:, :SSM_SQ].reshape(SSM_G, SSM_N), hl[:, SSM_SQ:].reshape(SSM_G, SSM_N)


def _ssm_sample_kernel(u_ref, hr_ref, hi_ref, lr_ref, li_ref, br_ref, bi_ref, cr_ref, ci_ref,
                       y_ref, or_ref, oi_ref):
    u = u_ref[...].astype(BF16)
    hr, hi = hr_ref[...], hi_ref[...]
    lr, li = lr_ref[...], li_ref[...]
    nr = lr * hr - li * hi + _dot(u, br_ref[...])
    ni = lr * hi + li * hr + _dot(u, bi_ref[...])
    or_ref[...] = nr
    oi_ref[...] = ni
    y_ref[...] = _dot(nr.astype(BF16), cr_ref[...]) + _dot(ni.astype(BF16), ci_ref[...])


def _ssm_sample(u, h_re, h_im, ssm_p):
    nb = u.shape[0]
    _, lam_bar, b_bar, c = _ssm_discretise(*ssm_p)
    eye = jnp.eye(SSM_G, dtype=F32)
    gn = SSM_G * SSM_N
    bd = lambda part, eq, shape: jnp.einsum(eq, part, eye).reshape(shape).astype(BF16)
    args = (u, h_re, h_im, lam_bar.real.reshape(1, gn), lam_bar.imag.reshape(1, gn),
            bd(b_bar.real, 'gnp,gh->gphn', (SSM_WIDTH, gn)), bd(b_bar.imag, 'gnp,gh->gphn', (SSM_WIDTH, gn)),
            bd(c.real, 'gpn,gh->gnhp', (gn, SSM_WIDTH)), bd(-c.imag, 'gpn,gh->gnhp', (gn, SSM_WIDTH)))
    return pl.pallas_call(
        _ssm_sample_kernel,
        grid=(1,),
        in_specs=[_const_spec(a.shape) for a in args],
        out_specs=[_const_spec((nb, SSM_WIDTH)), _const_spec((nb, gn)), _const_spec((nb, gn))],
        out_shape=[jax.ShapeDtypeStruct((nb, SSM_WIDTH), F32),
                   jax.ShapeDtypeStruct((nb, gn), F32), jax.ShapeDtypeStruct((nb, gn), F32)],
        compiler_params=_params(1, 32), name="ssm_sample",
    )(*args)


def _mix_kernel(x_ref, o1_ref, o2_ref, o3_ref, l1_ref, l2_ref, l3_ref, y_ref, u_ref, gs_ref, ga_ref,
                d_ref, wglu_ref, bglu_ref, wbs_ref, wba_ref, wout_ref, x1_ref, *scr, dils):
    tm = x_ref.shape[0]

    def natural(ref, dil, buf):
        if dil == 1:
            return ref[0]
        for r in range(dil):
            for t in range(N_TILES):
                buf[t, pl.ds(r, tm // dil, stride=dil), :] = ref[r, :, t * LANES:(t + 1) * LANES].astype(F32)
        return jnp.concatenate([buf[t] for t in range(N_TILES)], axis=1)

    o1, o2, o3 = (natural(r, d, scr[k]) for k, (r, d) in enumerate(zip((o1_ref, o2_ref, o3_ref), dils)))
    l1, l2, l3 = (natural(r, d, scr[3 + k]) for k, (r, d) in enumerate(zip((l1_ref, l2_ref, l3_ref), dils)))
    m = jnp.maximum(jnp.maximum(l1, l2), l3)
    e1, e2, e3 = jnp.exp(l1 - m), jnp.exp(l2 - m), jnp.exp(l3 - m)
    den = e1 + e2 + e3
    attn = (e1 / den) * o1 + (e2 / den) * o2 + (e3 / den) * o3
    y = jax.nn.gelu(y_ref[...] + d_ref[...] * u_ref[...])
    ssm_y = y * _sigmoid(_dot(y.astype(BF16), wglu_ref[...]) + bglu_ref[...])
    z = (_sigmoid(gs_ref[...].astype(F32)) * _dot(ssm_y.astype(BF16), wbs_ref[...])
         + _sigmoid(ga_ref[...].astype(F32)) * _dot(attn.astype(BF16), wba_ref[...]))
    x1_ref[...] = x_ref[...] + _dot(z.astype(BF16), wout_ref[...])


def _mix(x, os_, ls_, y_ssm, u, gs, ga, consts, tm):
    s = x.shape[0]
    row = lambda i: (i, 0)
    res = lambda i: (0, i, 0)
    dils = tuple(o.shape[0] for o in os_)
    grouped = [*os_, *ls_]
    flat = [y_ssm, u, gs, ga]
    return pl.pallas_call(
        functools.partial(_mix_kernel, dils=dils),
        grid=(s // tm,),
        in_specs=([pl.BlockSpec((tm, D_MODEL), row)]
                  + [pl.BlockSpec((a.shape[0], tm // a.shape[0], GROUP_W), res) for a in grouped]
                  + [pl.BlockSpec((tm, a.shape[1]), row) for a in flat]
                  + [_const_spec(c.shape) for c in consts]),
        out_specs=pl.BlockSpec((tm, D_MODEL), row),
        out_shape=jax.ShapeDtypeStruct((s, D_MODEL), F32),
        scratch_shapes=[pltpu.VMEM((N_TILES, tm, LANES), F32)] * 6,
        compiler_params=_params(1, 48), name="mix",
    )(x, *grouped, *flat, *consts)


def _mlp_kernel(*refs, n_phase, jobs):
    x1_ref, nm_ref, wup_ref, wdn_ref, nf_ref = refs[:5]
    n_in = 5 + sum(JOB_ARITY[j.part][0] for j in jobs)
    out_ref = refs[n_in]
    n_out = 1 + sum(JOB_ARITY[j.part][1] for j in jobs)
    up_scr = refs[n_in + n_out]
    _, in_phase = _phase_runner(n_phase)

    job_refs = (refs[5:n_in], refs[n_in + 1:n_in + n_out], refs[n_in + n_out + 1:])

    @pl.when(pl.program_id(0) == 0)
    def _():
        _run_cache_jobs(jobs, *job_refs, init=True)

    def ride(block):
        if block < n_phase:
            _run_cache_jobs(jobs, *job_refs)

    @in_phase(0)
    def _():
        hm = _rms(x1_ref[...], nm_ref[...]).astype(BF16)
        up = jnp.maximum(_dot(hm, wup_ref[...]), 0.0)
        up_scr[...] = (up * up).astype(BF16)
        ride(0)

    @in_phase(1 % n_phase)
    def _():
        x2 = x1_ref[...] + _dot(up_scr[...], wdn_ref[...])
        out_ref[...] = _rms(x2, nf_ref[...])
        ride(1)


def _mlp(x1, consts, tm, n_phase=1, jobs=(), job_args=()):
    s = x1.shape[0]
    ntiles = s // tm
    row = lambda st: (st // n_phase, 0)
    nb = job_args[0][1].shape[0] if jobs else 0
    job_in, job_out, job_shape, job_scr, job_alias = _cache_job_specs(jobs, nb, ntiles * n_phase)
    return pl.pallas_call(
        functools.partial(_mlp_kernel, n_phase=n_phase, jobs=jobs),
        grid=(ntiles * n_phase,),
        in_specs=[pl.BlockSpec((tm, D_MODEL), row)] + [_const_spec(c.shape) for c in consts] + job_in,
        out_specs=[pl.BlockSpec((tm, D_MODEL), row)] + job_out,
        out_shape=[jax.ShapeDtypeStruct((s, D_MODEL), F32)] + job_shape,
        scratch_shapes=[pltpu.VMEM((tm, consts[1].shape[1]), BF16)] + job_scr,
        input_output_aliases={5 + i: 1 + o for i, o in job_alias},
        compiler_params=_params(1, 58), name="mlp",
    )(x1, *consts, *[a for args in job_args for a in args])


def kernel(x_prompt, x_sample, cache_kv_g1, cache_kv_g2, cache_kv_g3, state_ssm, norm_mix, w_in,
           ssm_lambda_re, ssm_lambda_im, ssm_log_dt, ssm_b_re, ssm_b_im, ssm_c_re, ssm_c_im, ssm_d,
           w_glu, b_glu, w_branch_ssm, w_branch_attn, w_out, norm_mlp, w_up, w_down, norm_final):
    depth = w_in.shape[0]
    assert depth == 1, "one decoder layer"
    seq = x_prompt.shape[1]
    nb = x_sample.shape[0]
    assert x_prompt.shape[0] == 1 and x_sample.shape[1] == 1
    caches = (cache_kv_g1, cache_kv_g2, cache_kv_g3)
    windows = tuple(w for w, _ in ATTN_GROUPS)
    dils = tuple(d for _, d in ATTN_GROUPS)
    for c, w in zip(caches, windows):
        assert c.shape[2] == w, "cache holds exactly one window"

    w_in_bf = w_in[0].astype(BF16)
    gain_mix = norm_mix[0][None, :]
    mix_consts = (ssm_d[0][None, :], w_glu[0].astype(BF16), b_glu[0][None, :],
                  w_branch_ssm[0].astype(BF16), w_branch_attn[0].astype(BF16), w_out[0].astype(BF16))
    mlp_consts = (norm_mlp[0][None, :], w_up[0].astype(BF16), w_down[0].astype(BF16), norm_final[None, :])
    ssm_p = (ssm_lambda_re[0], ssm_lambda_im[0], ssm_log_dt[0], ssm_b_re[0], ssm_b_im[0],
             ssm_c_re[0], ssm_c_im[0])

    xs = x_sample[:, 0]
    res = _inproj(xs, gain_mix, w_in_bf, jnp.full((1,), PAST_LEN), jnp.zeros((nb,)), nb,
                  (nb,) * N_GROUPS, (1,) * N_GROUPS)
    qs_s, tails_s = res[0:3], res[9:12]
    u_s, gs_s, ga_s = res[12:15]

    cache_t = [c[0].transpose(0, 2, 3, 4, 1).reshape(nb * 2 * GROUP_W, c.shape[2]) for c in caches]
    q_s = [q[0].astype(F32) for q in qs_s]
    k_s = [t[:, :GROUP_W] for t in tails_s]
    v_s = [t[:, GROUP_W:] for t in tails_s]
    job = lambda g, part: _CacheJob(dils[g], caches[g].shape[2], part)

    xp = x_prompt[0]
    tm_p = 256
    res = _inproj(xp, gain_mix, w_in_bf, jnp.arange(seq // tm_p) * tm_p, jnp.arange(tm_p), tm_p,
                  windows, dils, n_phase=2, jobs=(job(2, "k"),),
                  job_args=((cache_t[2], q_s[2], k_s[2], _columns(k_s[2])),))
    qs, ks, vs, tails = res[0:3], res[3:6], res[6:9], res[9:12]
    u_p, gs_p, ga_p = res[12:15]
    cache3_k, p3, pn3, lse3 = res[15:19]
    os_p, ls_p = zip(*[_attn_prompt(qs[g], ks[g], vs[g]) for g in range(N_GROUPS)])
    y_p, hp_re, hp_im = _ssm_prompt(u_p, _ssm_tables(ssm_p))
    x1_p = _mix(xp, os_p, ls_p, y_p, u_p, gs_p, ga_p, mix_consts, tm_p)
    kv_args = lambda g: (cache_t[g], q_s[g], k_s[g], v_s[g], _columns(k_s[g]), _columns(v_s[g]))
    res = _mlp(x1_p, mlp_consts, tm_p, n_phase=2, jobs=(job(2, "v"), job(1, "kv"), job(0, "kv")),
               job_args=((cache_t[2], v_s[2], _columns(v_s[2]), p3, pn3, cache3_k), kv_args(1), kv_args(0)))
    y_prompt = res[0][None]
    new_caches = (res[6], res[3], res[1])
    os_s = (res[7], res[4], res[2])
    lses = (res[8], res[5], lse3)
    kv_prompt = [t.reshape(1, 1, w, 2, HEADS, HEAD_DIM) for t, w in zip(tails, windows)]
    ssm_prompt = jnp.stack([hp_re, hp_im], axis=-1)[None, None].astype(state_ssm.dtype)

    kv_sample = [t.reshape(nb, 2, HEADS, HEAD_DIM, t.shape[1]).transpose(0, 4, 1, 2, 3)[None]
                 for t in new_caches]
    os_s = [o[None] for o in os_s]
    ls_s = [jnp.repeat(l[:, :nb].T, HEAD_DIM, axis=1)[None] for l in lses]
    st = state_ssm[0].astype(F32)
    gn = SSM_G * SSM_N
    y_s, hs_re, hs_im = _ssm_sample(u_s, st[..., 0].reshape(nb, gn), st[..., 1].reshape(nb, gn), ssm_p)
    x1_s = _mix(xs, os_s, ls_s, y_s, u_s, gs_s, ga_s, mix_consts, nb)
    y_sample = _mlp(x1_s, mlp_consts, nb)[0][:, None]
    ssm_sample = jnp.stack([hs_re.reshape(nb, SSM_G, SSM_N), hs_im.reshape(nb, SSM_G, SSM_N)],
                           axis=-1)[None].astype(state_ssm.dtype)

    return (y_prompt, y_sample, kv_prompt[0], kv_prompt[1], kv_prompt[2], ssm_prompt,
            kv_sample[0], kv_sample[1], kv_sample[2], ssm_sample)
```

```python
import functools
import math
from typing import NamedTuple

import jax
import jax.numpy as jnp
from jax import lax
from jax.experimental import pallas as pl
from jax.experimental.pallas import tpu as pltpu

F32 = jnp.float32
BF16 = jnp.bfloat16

D_MODEL = 1024
HEAD_DIM = 64
HEADS = 8
GROUP_W = HEADS * HEAD_DIM
ATTN_GROUPS = ((128, 1), (512, 4), (2048, 16))
N_GROUPS = len(ATTN_GROUPS)
QKV_W = N_GROUPS * GROUP_W
BAND = 128
Q_BLOCK = 1024
ROT_DIM = HEAD_DIM // 4
ROT_HALF = ROT_DIM // 2
ROPE_THETA = 500000.0
PAST_LEN = 8192
SSM_WIDTH = 512
SSM_P = 16
SSM_G = SSM_WIDTH // SSM_P
SSM_N = 64
SSM_T = 16
EPS = 1e-6
NEG = -1e30
LN2 = math.log(2.0)
Q_SCALE = HEAD_DIM ** -0.5 * math.log2(math.e)
LANES = 128
N_TILES = GROUP_W // LANES
MIB = 1024 * 1024

SSM_Q = SSM_WIDTH // LANES
SSM_GQ = SSM_G // SSM_Q
SSM_SQ = SSM_GQ * SSM_N
SSM_PAIRS = SSM_T // 2

OFF_Q, OFF_K, OFF_V = 0, QKV_W, 2 * QKV_W
OFF_U = 3 * QKV_W
OFF_GS = OFF_U + SSM_WIDTH
OFF_GA = OFF_GS + D_MODEL
IN_W = OFF_GA + D_MODEL


def _const_spec(shape):
    nd = len(shape)
    return pl.BlockSpec(shape, lambda *_: (0,) * nd, pipeline_mode=pl.Buffered(1))


def _params(n_axes, vmem_mib):
    return pltpu.CompilerParams(dimension_semantics=("arbitrary",) * n_axes,
                                vmem_limit_bytes=vmem_mib * MIB)


def _sigmoid(x):
    return 1.0 / (1.0 + jnp.exp(-x))


def _rms(x, g):
    return x * lax.rsqrt(jnp.mean(x * x, axis=-1, keepdims=True) + EPS) * g


def _dot(a, b):
    return jnp.dot(a, b, preferred_element_type=F32)


def _phase_runner(n_phase):
    step = pl.program_id(0)
    if n_phase == 1:
        return step, lambda ph: (lambda fn: fn())
    phase = step & (n_phase - 1)
    return lax.shift_right_logical(step, int(math.log2(n_phase))), lambda ph: pl.when(phase == ph)


def _inproj_kernel(*refs, tm, dils, tail_rows, first_tail, n_phase, jobs):
    n_in = 7 + sum(JOB_ARITY[j.part][0] for j in jobs)
    x_ref, g_ref, w_ref, ca_ref, sa_ref, cb_ref, sb_ref = refs[:7]
    outs = refs[n_in:n_in + 15]
    q_refs, k_refs, v_refs, tail_refs = outs[0:3], outs[3:6], outs[6:9], outs[9:12]
    u_ref, gs_ref, ga_ref = outs[12:15]
    n_out = 15 + sum(JOB_ARITY[j.part][1] for j in jobs)
    scr, h_scr, rot_scr = refs[n_in + n_out:n_in + n_out + 3]
    i, in_phase = _phase_runner(n_phase)

    def rot(a):
        cos, s1, s2 = rot_scr[0], rot_scr[1], rot_scr[2]
        parts = []
        for j in range(GROUP_W // LANES):
            seg = a[:, j * LANES:(j + 1) * LANES]
            parts.append(seg * cos + pltpu.roll(seg, LANES - ROT_HALF, 1) * s1
                         + pltpu.roll(seg, ROT_HALF, 1) * s2)
        return jnp.concatenate(parts, axis=1)

    def put(out_ref, val, dil):
        if dil == 1:
            out_ref[0] = val.astype(BF16)
            return
        for t in range(N_TILES):
            scr[t] = val[:, t * LANES:(t + 1) * LANES]
        for r in range(dil):
            rows = [scr[t, pl.ds(r, tm // dil, stride=dil), :] for t in range(N_TILES)]
            out_ref[r] = jnp.concatenate(rows, axis=1).astype(BF16)

    def proj(off, gi):
        c0 = off + gi * GROUP_W
        return _dot(h_scr[...], w_ref[:, c0:c0 + GROUP_W])

    def put_tail(gi, half, val):
        tb = tail_rows[gi]

        @pl.when(i >= first_tail[gi])
        def _():
            tail_refs[gi][:, half * GROUP_W:(half + 1) * GROUP_W] = val[tm - tb:, :]

    job_refs = (refs[7:n_in], refs[n_in + 15:n_in + n_out], refs[n_in + n_out + 3:])

    @pl.when(pl.program_id(0) == 0)
    def _():
        _run_cache_jobs(jobs, *job_refs, init=True)

    def ride(block):
        if block < n_phase:
            _run_cache_jobs(jobs, *job_refs)

    @in_phase(0)
    def _():
        h_scr[...] = _rms(x_ref[...], g_ref[...]).astype(BF16)
        ca, sa = ca_ref[pl.ds(i, 1), :], sa_ref[pl.ds(i, 1), :]
        cb, sb = cb_ref[...], sb_ref[...]
        cos_f = ca * cb - sa * sb
        sin_f = sa * cb + ca * sb
        hd = lax.broadcasted_iota(jnp.int32, (tm, LANES), 1) & (HEAD_DIM - 1)
        rot_scr[0] = jnp.where(hd < ROT_DIM, cos_f, 1.0)
        rot_scr[1] = jnp.where(hd < ROT_HALF, -sin_f, 0.0)
        rot_scr[2] = jnp.where((hd >= ROT_HALF) & (hd < ROT_DIM), sin_f, 0.0)
        for gi in range(N_GROUPS):
            put(q_refs[gi], rot(proj(OFF_Q, gi)) * Q_SCALE, dils[gi])
        ride(0)

    @in_phase(1 % n_phase)
    def _():
        for gi in range(N_GROUPS):
            ka = rot(proj(OFF_K, gi))
            put(k_refs[gi], ka, dils[gi])
            put_tail(gi, 0, ka)
        ride(1)

    @in_phase(2 % n_phase)
    def _():
        for gi in range(N_GROUPS):
            va = proj(OFF_V, gi)
            put(v_refs[gi], va, dils[gi])
            put_tail(gi, 1, va)
        ride(2)

    @in_phase(3 % n_phase)
    def _():
        h = h_scr[...]
        u_ref[...] = _dot(h, w_ref[:, OFF_U:OFF_GS])
        gs_ref[...] = _dot(h, w_ref[:, OFF_GS:OFF_GA]).astype(BF16)
        ga_ref[...] = _dot(h, w_ref[:, OFF_GA:IN_W]).astype(BF16)
        ride(3)


def _inproj(x, gain, w_bf, tile_pos, row_pos, tm, windows, dils, n_phase=1, jobs=(), job_args=()):
    s = x.shape[0]
    ntiles = s // tm
    inv_freq = jnp.exp(-math.log(ROPE_THETA) * jnp.arange(ROT_HALF, dtype=F32) * (2.0 / ROT_DIM))
    lane_freq = jnp.tile(inv_freq, LANES // ROT_HALF)[None, :]
    ang_a = tile_pos.astype(F32)[:, None] * lane_freq
    ang_b = row_pos.astype(F32)[:, None] * lane_freq
    tail_rows = tuple(min(w, tm) for w in windows)
    first_tail = tuple(ntiles - w // tb for w, tb in zip(windows, tail_rows))
    row = lambda st: (st // n_phase, 0)
    res = lambda st: (0, st // n_phase, 0)
    qkv_shapes = [jax.ShapeDtypeStruct((d, s // d, GROUP_W), BF16) for d in dils]
    qkv_specs = [pl.BlockSpec((d, tm // d, GROUP_W), res) for d in dils]
    out_shape = (qkv_shapes * 3
                 + [jax.ShapeDtypeStruct((w, 2 * GROUP_W), F32) for w in windows]
                 + [jax.ShapeDtypeStruct((s, SSM_WIDTH), F32),
                    jax.ShapeDtypeStruct((s, D_MODEL), BF16),
                    jax.ShapeDtypeStruct((s, D_MODEL), BF16)])
    out_specs = (qkv_specs * 3
                 + [pl.BlockSpec((tb, 2 * GROUP_W), functools.partial(
                     lambda st, ft: (jnp.maximum(st // n_phase - ft, 0), 0), ft=ft))
                    for tb, ft in zip(tail_rows, first_tail)]
                 + [pl.BlockSpec((tm, SSM_WIDTH), row),
                    pl.BlockSpec((tm, D_MODEL), row),
                    pl.BlockSpec((tm, D_MODEL), row)])
    nb = job_args[0][1].shape[0] if jobs else 0
    job_in, job_out, job_shape, job_scr, job_alias = _cache_job_specs(jobs, nb, ntiles * n_phase)
    kern = functools.partial(_inproj_kernel, tm=tm, dils=dils, tail_rows=tail_rows,
                             first_tail=first_tail, n_phase=n_phase, jobs=jobs)
    return pl.pallas_call(
        kern,
        grid=(ntiles * n_phase,),
        in_specs=[pl.BlockSpec((tm, D_MODEL), row), _const_spec((1, D_MODEL)),
                  _const_spec((D_MODEL, IN_W)),
                  _const_spec((ntiles, LANES)), _const_spec((ntiles, LANES)),
                  _const_spec((tm, LANES)), _const_spec((tm, LANES))] + job_in,
        out_specs=out_specs + job_out, out_shape=out_shape + job_shape,
        scratch_shapes=[pltpu.VMEM((N_TILES, tm, LANES), F32), pltpu.VMEM((tm, D_MODEL), BF16),
                        pltpu.VMEM((3, tm, LANES), F32)] + job_scr,
        input_output_aliases={7 + i: 15 + o for i, o in job_alias},
        compiler_params=_params(1, 58), name="inproj",
    )(x, gain, w_bf, jnp.cos(ang_a), jnp.sin(ang_a), jnp.cos(ang_b), jnp.sin(ang_b),
      *[a for args in job_args for a in args])


def _attn_prompt_kernel(q_ref, kp_ref, kc_ref, vp_ref, vc_ref, o_ref, l_ref):
    j = pl.program_id(1)
    lane = lax.broadcasted_iota(jnp.int32, (BAND, LANES), 1)
    head_a = lane < HEAD_DIM
    qi = lax.broadcasted_iota(jnp.int32, (BAND, 2 * BAND), 0)
    kj = lax.broadcasted_iota(jnp.int32, (BAND, 2 * BAND), 1)
    band = (kj >= qi) & (kj <= qi + BAND)
    bias = jnp.where(band, 0.0, NEG)
    bias_first = jnp.where(band & (kj >= BAND), 0.0, NEG)
    for t in range(q_ref.shape[0] // BAND):
        rows = slice(t * BAND, (t + 1) * BAND)
        b = bias if t > 0 else jnp.where(j > 0, bias, bias_first)
        for hp in range(N_TILES):
            sl = slice(hp * LANES, (hp + 1) * LANES)
            q2 = q_ref[rows, sl]
            if t == 0:
                kk = jnp.concatenate([kp_ref[:, sl], kc_ref[0:BAND, sl]], axis=0)
                vv = jnp.concatenate([vp_ref[:, sl], vc_ref[0:BAND, sl]], axis=0)
            else:
                kk = kc_ref[(t - 1) * BAND:(t + 1) * BAND, sl]
                vv = vc_ref[(t - 1) * BAND:(t + 1) * BAND, sl]
            zero = jnp.zeros_like(q2)
            res = []
            for qm in (jnp.where(head_a, q2, zero), jnp.where(head_a, zero, q2)):
                s = lax.dot_general(qm, kk, (((1,), (1,)), ((), ())), preferred_element_type=F32) + b
                m = jnp.max(s, axis=1, keepdims=True)
                p = jnp.exp2(s - m)
                l = jnp.sum(p, axis=1, keepdims=True)
                o = _dot(p.astype(BF16), vv) * (1.0 / l)
                res.append((o, (m + jnp.log2(l)) * LN2))
            o_ref[rows, sl] = jnp.where(head_a, res[0][0], res[1][0]).astype(BF16)
            l_ref[rows, sl] = jnp.where(head_a, res[0][1], res[1][1])


def _attn_prompt(q, k, v):
    dil, n, _ = q.shape
    qb = min(Q_BLOCK, n)
    cur = lambda r, j: (r, j, 0)
    prev = lambda r, j: (r, jnp.maximum(j * (qb // BAND) - 1, 0), 0)
    blk = (None, qb, GROUP_W)
    hist = (None, BAND, GROUP_W)
    return pl.pallas_call(
        _attn_prompt_kernel,
        grid=(dil, n // qb),
        in_specs=[pl.BlockSpec(blk, cur), pl.BlockSpec(hist, prev), pl.BlockSpec(blk, cur),
                  pl.BlockSpec(hist, prev), pl.BlockSpec(blk, cur)],
        out_specs=[pl.BlockSpec(blk, cur), pl.BlockSpec(blk, cur)],
        out_shape=[jax.ShapeDtypeStruct((dil, n, GROUP_W), BF16),
                   jax.ShapeDtypeStruct((dil, n, GROUP_W), F32)],
        compiler_params=_params(2, 32), name=f"attn_prompt_d{dil}",
    )(q, k, k, v, v)


class _CacheJob(NamedTuple):
    dil: int
    lc: int
    part: str


JOB_ARITY = {"kv": (6, 3), "k": (4, 4), "v": (6, 2)}


def _cache_step(job, ins, outs, scr, init):
    dil, lc = job.dil, job.lc
    seq = pl.program_id(0)
    lane = seq & (LANES - 1)
    lane_h = lax.broadcasted_iota(jnp.int32, (HEADS, LANES), 1)
    last = lax.broadcasted_iota(jnp.int32, (GROUP_W, LANES), 1) == LANES - 1
    own = (lax.broadcasted_iota(jnp.int32, (HEADS, GROUP_W), 1) >> int(math.log2(HEAD_DIM))
           == lax.broadcasted_iota(jnp.int32, (HEADS, GROUP_W), 0))
    row = lambda ref: ref[pl.ds(seq, 1), :]
    new_col = lambda ref: pltpu.roll(ref[...], LANES - 1 - lane, 1)
    c_ref, out_ref = ins[0], outs[0]
    sel_scr = scr[0] if dil > 1 else None
    if init:
        if dil > 1:
            pos = lax.broadcasted_iota(jnp.int32, (lc, BAND), 0)
            j = lax.broadcasted_iota(jnp.int32, (lc, BAND), 1)
            sel_scr[...] = jnp.where(pos == j * dil, 1.0, 0.0).astype(BF16)
        if job.part != "v":
            outs[-1][...] = jnp.zeros_like(outs[-1])
        return

    def picked(cin):
        if dil == 1:
            return cin.astype(BF16)
        return _dot(cin.astype(BF16), sel_scr[...]).astype(BF16)

    def shift_into(row0, cin, col127):
        nt = lc // LANES
        rot = [pltpu.roll(cin[:, t * LANES:(t + 1) * LANES], LANES - 1, 1) for t in range(nt)]
        for t in range(nt):
            nxt = rot[t + 1] if t + 1 < nt else col127
            out_ref[row0:row0 + GROUP_W, t * LANES:(t + 1) * LANES] = jnp.where(last, nxt, rot[t])

    def scores(q_ref, k_ref, kt_ref, l_ref):
        kin = c_ref[0:GROUP_W, :]
        shift_into(0, kin, new_col(kt_ref))
        q_heads = jnp.where(own, row(q_ref), 0.0)
        s = _dot(q_heads.astype(BF16), picked(kin))
        s_new = jnp.sum(q_heads * row(k_ref), axis=1, keepdims=True)
        m = jnp.maximum(jnp.max(s, axis=1, keepdims=True), s_new)
        p = jnp.exp2(s - m)
        pn = jnp.exp2(s_new - m)
        l = jnp.sum(p, axis=1, keepdims=True) + pn
        l_ref[...] = jnp.where(lane_h == lane, (m + jnp.log2(l)) * LN2, l_ref[...])
        return p / l, pn / l

    def attend(row0, p, pn, v_ref, vt_ref, o_ref):
        vin = c_ref[row0:row0 + GROUP_W, :]
        shift_into(row0, vin, new_col(vt_ref))
        o_all = lax.dot_general(p.astype(BF16), picked(vin), (((1,), (1,)), ((), ())),
                                preferred_element_type=F32)
        o_all = o_all + pn * row(v_ref)
        o_ref[pl.ds(seq, 1), :] = jnp.sum(jnp.where(own, o_all, 0.0), axis=0, keepdims=True)

    if job.part == "kv":
        _, q_ref, k_ref, v_ref, kt_ref, vt_ref = ins
        _, o_ref, l_ref = outs
        p, pn = scores(q_ref, k_ref, kt_ref, l_ref)
        attend(GROUP_W, p, pn, v_ref, vt_ref, o_ref)
    elif job.part == "k":
        _, q_ref, k_ref, kt_ref = ins
        _, p_ref, pn_ref, l_ref = outs
        p, pn = scores(q_ref, k_ref, kt_ref, l_ref)
        p_ref[...] = p
        pn_ref[...] = jnp.broadcast_to(pn, pn_ref.shape)
    else:
        _, v_ref, vt_ref, p_ref, pn_ref, _ = ins
        attend(0, p_ref[...], pn_ref[:, 0:1], v_ref, vt_ref, outs[1])


def _run_cache_jobs(jobs, in_refs, out_refs, scr_refs, init=False):
    i = o = k = 0
    for job in jobs:
        ni, no = JOB_ARITY[job.part]
        ns = 1 if job.dil > 1 else 0
        _cache_step(job, in_refs[i:i + ni], out_refs[o:o + no], scr_refs[k:k + ns], init)
        i, o, k = i + ni, o + no, k + ns


def _cache_job_specs(jobs, nb, nsteps):
    in_specs, out_specs, out_shape, scratch, aliases = [], [], [], [], []
    for job in jobs:
        assert nsteps == nb, "one cache unit per grid step"
        half = {"kv": None, "k": 0, "v": 1}[job.part]
        if half is None:
            blk = pl.BlockSpec((2 * GROUP_W, job.lc), lambda st: (st, 0))
        else:
            blk = pl.BlockSpec((GROUP_W, job.lc), functools.partial(lambda st, h: (2 * st + h, 0), h=half))
        rows = _const_spec((nb, GROUP_W))
        nbp = -(-nb // LANES) * LANES
        lane_blk = lambda st: (0, st // LANES)
        cols = pl.BlockSpec((GROUP_W, LANES), lane_blk)
        lse = pl.BlockSpec((HEADS, LANES), lane_blk)
        per_seq = pl.BlockSpec((None, HEADS, BAND), lambda st: (st, 0, 0))
        cache_shape = jax.ShapeDtypeStruct((nb * 2 * GROUP_W, job.lc), F32)
        rows_shape = jax.ShapeDtypeStruct((nb, GROUP_W), F32)
        per_seq_shape = jax.ShapeDtypeStruct((nb, HEADS, BAND), F32)
        if job.part == "kv":
            in_specs += [blk, rows, rows, rows, cols, cols]
            out_specs += [blk, rows, lse]
            out_shape += [cache_shape, rows_shape, jax.ShapeDtypeStruct((HEADS, nbp), F32)]
        elif job.part == "k":
            in_specs += [blk, rows, rows, cols]
            out_specs += [blk, per_seq, per_seq, lse]
            out_shape += [cache_shape, per_seq_shape, per_seq_shape,
                          jax.ShapeDtypeStruct((HEADS, nbp), F32)]
        else:
            aliases.append((len(in_specs) + 5, len(out_specs)))
            in_specs += [blk, rows, cols, per_seq, per_seq, pl.BlockSpec(memory_space=pl.ANY)]
            out_specs += [blk, rows]
            out_shape += [cache_shape, rows_shape]
        if job.dil > 1:
            scratch += [pltpu.VMEM((job.lc, BAND), BF16)]
    return in_specs, out_specs, out_shape, scratch, aliases


def _columns(t):
    return jnp.pad(t.T, ((0, 0), (0, -t.shape[0] % LANES)))


def _ssm_discretise(lam_re, lam_im, log_dt, b_re, b_im, c_re, c_im):
    lam = lax.complex(lam_re.astype(F32), lam_im.astype(F32))
    dt = jnp.exp(log_dt.astype(F32))[:, None]
    ldt = lam * dt
    lam_bar = jnp.exp(ldt)
    b_bar = ((lam_bar - 1.0) / lam)[..., None] * lax.complex(b_re.astype(F32), b_im.astype(F32))
    c = lax.complex(c_re.astype(F32), c_im.astype(F32))
    return ldt, lam_bar, b_bar, c


def _ssm_tables(ssm_p):
    ldt, _, b_bar, c = _ssm_discretise(*ssm_p)
    t, q, gq, pr = SSM_T, SSM_Q, SSM_GQ, SSM_PAIRS
    steps = jnp.arange(t + 1, dtype=F32)
    pw = jnp.exp(ldt[None] * steps[:, None, None])
    hi = lax.Precision.HIGHEST
    eye = jnp.eye(gq, dtype=F32)
    kern = jnp.einsum('gqn,kgn,gnp->gkpq', c, pw[:t], b_bar, precision=hi).real
    kern = kern.reshape(q, gq, t, SSM_P, SSM_P)
    d_, a_, b_ = jnp.meshgrid(jnp.arange(pr), jnp.arange(2), jnp.arange(2), indexing='ij')
    lag = 2 * d_ + b_ - a_
    kl = jnp.where((lag >= 0)[None, None, :, :, :, None, None], kern[:, :, jnp.maximum(lag, 0)], 0.0)
    toep = kl.transpose(0, 2, 3, 1, 5, 4, 6).reshape(q, pr, 2 * LANES, 2 * SSM_P)
    win = pw[:t][::-1][:, :, :, None] * b_bar[None]
    win = win.reshape(pr, 2, q, gq, SSM_N, SSM_P).transpose(2, 0, 1, 3, 5, 4)
    w_in = jnp.concatenate([win.real, win.real, win.imag, win.imag], axis=-1)
    w_in = w_in.reshape(q, pr, 2 * LANES, 2 * LANES)
    cl = c[None] * pw[1:, :, None, :]
    cl = cl.reshape(pr, 2, q, gq, SSM_P, SSM_N).transpose(2, 0, 1, 3, 4, 5)
    w_out = jnp.concatenate([cl.real, cl.real, -cl.imag, -cl.imag], axis=-1)
    w_out = w_out.reshape(q, pr, 2 * LANES, 2 * LANES)
    return toep.astype(BF16), w_in.astype(BF16), w_out.astype(BF16), ldt


def _ssm_prompt_kernel(u_ref, toep_ref, win_ref, wout_ref, lp_ref, y_ref, hl_ref, carry_ref, win_bd,
                       wout_bd, toep_bd, *, nsteps):
    tt = pl.program_id(1)
    nc = u_ref.shape[0] // SSM_T
    sq = SSM_SQ

    @pl.when(tt == 0)
    def _():
        carry_ref[...] = jnp.zeros_like(carry_ref)
        r = lax.broadcasted_iota(jnp.int32, (2 * LANES, 2 * sq), 0)
        col = lax.broadcasted_iota(jnp.int32, (2 * LANES, 2 * sq), 1)
        own = (((r & (LANES - 1)) >> int(math.log2(SSM_P)))
               == ((col & (sq - 1)) >> int(math.log2(SSM_N))))
        for i in range(SSM_PAIRS):
            for src, dst in ((win_ref, win_bd), (wout_ref, wout_bd)):
                c = src[i]
                full = jnp.concatenate([c[:, :LANES]] * (sq // LANES) + [c[:, LANES:]] * (sq // LANES), axis=1)
                dst[i] = jnp.where(own, full, jnp.zeros_like(full))
        er = lax.broadcasted_iota(jnp.int32, (2 * SSM_P, 2 * LANES), 0)
        ec = lax.broadcasted_iota(jnp.int32, (2 * SSM_P, 2 * LANES), 1)
        lg_p, lg_l = int(math.log2(SSM_P)), int(math.log2(LANES))
        spread = jnp.where(((er >> lg_p) == (ec >> lg_l)) & ((er & (SSM_P - 1)) == (ec & (SSM_P - 1))),
                           1.0, 0.0).astype(BF16)
        r2 = lax.broadcasted_iota(jnp.int32, (2 * LANES, 2 * LANES), 0)
        c2 = lax.broadcasted_iota(jnp.int32, (2 * LANES, 2 * LANES), 1)
        own2 = ((r2 & (LANES - 1)) >> lg_p) == ((c2 & (LANES - 1)) >> lg_p)
        for d in range(SSM_PAIRS):
            toep_bd[d] = jnp.where(own2, _dot(toep_ref[d], spread), 0.0).astype(BF16)

    step = lambda t: u_ref[pl.ds(t, nc, stride=SSM_T), :].astype(BF16)
    a = [jnp.concatenate([step(2 * i), step(2 * i + 1)], axis=1) for i in range(SSM_PAIRS)]
    s = _dot(a[0], win_bd[0])
    for i in range(1, SSM_PAIRS):
        s = s + _dot(a[i], win_bd[i])
    xr, xi = s[:, :sq], s[:, sq:]
    row = lax.broadcasted_iota(jnp.int32, (nc, sq), 0)
    cr, ci = carry_ref[7:8, :sq], carry_ref[7:8, sq:]
    lr, li = lp_ref[0:1, :sq], lp_ref[0:1, sq:]
    xr = xr + jnp.where(row == 0, lr * cr - li * ci, 0.0)
    xi = xi + jnp.where(row == 0, lr * ci + li * cr, 0.0)

    def down(v, d):
        return jnp.where(row >= d, pltpu.roll(v, d, 0), 0.0)

    for k in range(nsteps):
        d = 1 << k
        lr, li = lp_ref[k:k + 1, :sq], lp_ref[k:k + 1, sq:]
        sr, si = down(xr, d), down(xi, d)
        xr, xi = xr + lr * sr - li * si, xi + lr * si + li * sr
    hr = jnp.where(row == 0, cr, pltpu.roll(xr, 1, 0))
    hi = jnp.where(row == 0, ci, pltpu.roll(xi, 1, 0))
    h_in = jnp.concatenate([hr, hi], axis=1).astype(BF16)
    last = jnp.concatenate([xr[nc - 8:], xi[nc - 8:]], axis=1)
    carry_ref[...] = last
    hl_ref[...] = last
    for j in range(SSM_PAIRS):
        acc = lax.dot_general(h_in, wout_bd[j], (((1,), (1,)), ((), ())), preferred_element_type=F32)
        for i in range(j + 1):
            acc = acc + _dot(a[i], toep_bd[j - i])
        y_ref[pl.ds(2 * j, nc, stride=SSM_T), :] = acc[:, :LANES]
        y_ref[pl.ds(2 * j + 1, nc, stride=SSM_T), :] = acc[:, LANES:]


def _ssm_prompt(u, tables):
    toep, w_in, w_out, ldt = tables
    s = u.shape[0]
    tile = min(s, 8192)
    nc = tile // SSM_T
    nsteps = int(math.log2(nc))
    scale = (SSM_T * (2.0 ** jnp.arange(nsteps, dtype=F32)))[:, None, None]
    lp = jnp.exp(ldt[None] * scale)
    lp = jnp.concatenate([lp.real.reshape(nsteps, SSM_Q, SSM_SQ), lp.imag.reshape(nsteps, SSM_Q, SSM_SQ)],
                         axis=2).transpose(1, 0, 2)
    per_q = lambda *shape: pl.BlockSpec((None,) + shape, lambda q, t: (q,) + (0,) * len(shape))
    y, hl = pl.pallas_call(
        functools.partial(_ssm_prompt_kernel, nsteps=nsteps),
        grid=(SSM_Q, s // tile),
        in_specs=[pl.BlockSpec((tile, LANES), lambda q, t: (t, q)),
                  per_q(SSM_PAIRS, 2 * LANES, 2 * SSM_P), per_q(SSM_PAIRS, 2 * LANES, 2 * LANES),
                  per_q(SSM_PAIRS, 2 * LANES, 2 * LANES), per_q(nsteps, 2 * SSM_SQ)],
        out_specs=[pl.BlockSpec((tile, LANES), lambda q, t: (t, q)), per_q(8, 2 * SSM_SQ)],
        out_shape=[jax.ShapeDtypeStruct((s, SSM_WIDTH), F32),
                   jax.ShapeDtypeStruct((SSM_Q, 8, 2 * SSM_SQ), F32)],
        scratch_shapes=[pltpu.VMEM((8, 2 * SSM_SQ), F32),
                        pltpu.VMEM((SSM_PAIRS, 2 * LANES, 2 * SSM_SQ), BF16),
                        pltpu.VMEM((SSM_PAIRS, 2 * LANES, 2 * SSM_SQ), BF16),
                        pltpu.VMEM((SSM_PAIRS, 2 * LANES, 2 * LANES), BF16)],
        compiler_params=_params(2, 56), name="ssm_prompt",
    )(u, toep, w_in, w_out, lp)
    hl = hl[:, 7, :]
    return y, hl[:, :SSM_SQ].reshape(SSM_G, SSM_N), hl[:, SSM_SQ:].reshape(SSM_G, SSM_N)


def _ssm_sample_kernel(u_ref, hr_ref, hi_ref, lr_ref, li_ref, br_ref, bi_ref, cr_ref, ci_ref,
                       y_ref, or_ref, oi_ref):
    u = u_ref[...].astype(BF16)
    hr, hi = hr_ref[...], hi_ref[...]
    lr, li = lr_ref[...], li_ref[...]
    nr = lr * hr - li * hi + _dot(u, br_ref[...])
    ni = lr * hi + li * hr + _dot(u, bi_ref[...])
    or_ref[...] = nr
    oi_ref[...] = ni
    y_ref[...] = _dot(nr.astype(BF16), cr_ref[...]) + _dot(ni.astype(BF16), ci_ref[...])


def _ssm_sample(u, h_re, h_im, ssm_p):
    nb = u.shape[0]
    _, lam_bar, b_bar, c = _ssm_discretise(*ssm_p)
    eye = jnp.eye(SSM_G, dtype=F32)
    gn = SSM_G * SSM_N
    bd = lambda part, eq, shape: jnp.einsum(eq, part, eye).reshape(shape).astype(BF16)
    args = (u, h_re, h_im, lam_bar.real.reshape(1, gn), lam_bar.imag.reshape(1, gn),
            bd(b_bar.real, 'gnp,gh->gphn', (SSM_WIDTH, gn)), bd(b_bar.imag, 'gnp,gh->gphn', (SSM_WIDTH, gn)),
            bd(c.real, 'gpn,gh->gnhp', (gn, SSM_WIDTH)), bd(-c.imag, 'gpn,gh->gnhp', (gn, SSM_WIDTH)))
    return pl.pallas_call(
        _ssm_sample_kernel,
        grid=(1,),
        in_specs=[_const_spec(a.shape) for a in args],
        out_specs=[_const_spec((nb, SSM_WIDTH)), _const_spec((nb, gn)), _const_spec((nb, gn))],
        out_shape=[jax.ShapeDtypeStruct((nb, SSM_WIDTH), F32),
                   jax.ShapeDtypeStruct((nb, gn), F32), jax.ShapeDtypeStruct((nb, gn), F32)],
        compiler_params=_params(1, 32), name="ssm_sample",
    )(*args)


def _mix_kernel(x_ref, o1_ref, o2_ref, o3_ref, l1_ref, l2_ref, l3_ref, y_ref, u_ref, gs_ref, ga_ref,
                d_ref, wglu_ref, bglu_ref, wbs_ref, wba_ref, wout_ref, x1_ref, *scr, dils):
    tm = x_ref.shape[0]

    def natural(ref, dil, buf):
        if dil == 1:
            return ref[0]
        for r in range(dil):
            for t in range(N_TILES):
                buf[t, pl.ds(r, tm // dil, stride=dil), :] = ref[r, :, t * LANES:(t + 1) * LANES].astype(F32)
        return jnp.concatenate([buf[t] for t in range(N_TILES)], axis=1)

    o1, o2, o3 = (natural(r, d, scr[k]) for k, (r, d) in enumerate(zip((o1_ref, o2_ref, o3_ref), dils)))
    l1, l2, l3 = (natural(r, d, scr[3 + k]) for k, (r, d) in enumerate(zip((l1_ref, l2_ref, l3_ref), dils)))
    m = jnp.maximum(jnp.maximum(l1, l2), l3)
    e1, e2, e3 = jnp.exp(l1 - m), jnp.exp(l2 - m), jnp.exp(l3 - m)
    den = e1 + e2 + e3
    attn = (e1 / den) * o1 + (e2 / den) * o2 + (e3 / den) * o3
    y = jax.nn.gelu(y_ref[...] + d_ref[...] * u_ref[...])
    ssm_y = y * _sigmoid(_dot(y.astype(BF16), wglu_ref[...]) + bglu_ref[...])
    z = (_sigmoid(gs_ref[...].astype(F32)) * _dot(ssm_y.astype(BF16), wbs_ref[...])
         + _sigmoid(ga_ref[...].astype(F32)) * _dot(attn.astype(BF16), wba_ref[...]))
    x1_ref[...] = x_ref[...] + _dot(z.astype(BF16), wout_ref[...])


def _mix(x, os_, ls_, y_ssm, u, gs, ga, consts, tm):
    s = x.shape[0]
    row = lambda i: (i, 0)
    res = lambda i: (0, i, 0)
    dils = tuple(o.shape[0] for o in os_)
    grouped = [*os_, *ls_]
    flat = [y_ssm, u, gs, ga]
    return pl.pallas_call(
        functools.partial(_mix_kernel, dils=dils),
        grid=(s // tm,),
        in_specs=([pl.BlockSpec((tm, D_MODEL), row)]
                  + [pl.BlockSpec((a.shape[0], tm // a.shape[0], GROUP_W), res) for a in grouped]
                  + [pl.BlockSpec((tm, a.shape[1]), row) for a in flat]
                  + [_const_spec(c.shape) for c in consts]),
        out_specs=pl.BlockSpec((tm, D_MODEL), row),
        out_shape=jax.ShapeDtypeStruct((s, D_MODEL), F32),
        scratch_shapes=[pltpu.VMEM((N_TILES, tm, LANES), F32)] * 6,
        compiler_params=_params(1, 48), name="mix",
    )(x, *grouped, *flat, *consts)


def _mlp_kernel(*refs, n_phase, jobs):
    x1_ref, nm_ref, wup_ref, wdn_ref, nf_ref = refs[:5]
    n_in = 5 + sum(JOB_ARITY[j.part][0] for j in jobs)
    out_ref = refs[n_in]
    n_out = 1 + sum(JOB_ARITY[j.part][1] for j in jobs)
    up_scr = refs[n_in + n_out]
    _, in_phase = _phase_runner(n_phase)

    job_refs = (refs[5:n_in], refs[n_in + 1:n_in + n_out], refs[n_in + n_out + 1:])

    @pl.when(pl.program_id(0) == 0)
    def _():
        _run_cache_jobs(jobs, *job_refs, init=True)

    def ride(block):
        if block < n_phase:
            _run_cache_jobs(jobs, *job_refs)

    @in_phase(0)
    def _():
        hm = _rms(x1_ref[...], nm_ref[...]).astype(BF16)
        up = jnp.maximum(_dot(hm, wup_ref[...]), 0.0)
        up_scr[...] = (up * up).astype(BF16)
        ride(0)

    @in_phase(1 % n_phase)
    def _():
        x2 = x1_ref[...] + _dot(up_scr[...], wdn_ref[...])
        out_ref[...] = _rms(x2, nf_ref[...])
        ride(1)


def _mlp(x1, consts, tm, n_phase=1, jobs=(), job_args=()):
    s = x1.shape[0]
    ntiles = s // tm
    row = lambda st: (st // n_phase, 0)
    nb = job_args[0][1].shape[0] if jobs else 0
    job_in, job_out, job_shape, job_scr, job_alias = _cache_job_specs(jobs, nb, ntiles * n_phase)
    return pl.pallas_call(
        functools.partial(_mlp_kernel, n_phase=n_phase, jobs=jobs),
        grid=(ntiles * n_phase,),
        in_specs=[pl.BlockSpec((tm, D_MODEL), row)] + [_const_spec(c.shape) for c in consts] + job_in,
        out_specs=[pl.BlockSpec((tm, D_MODEL), row)] + job_out,
        out_shape=[jax.ShapeDtypeStruct((s, D_MODEL), F32)] + job_shape,
        scratch_shapes=[pltpu.VMEM((tm, consts[1].shape[1]), BF16)] + job_scr,
        input_output_aliases={5 + i: 1 + o for i, o in job_alias},
        compiler_params=_params(1, 58), name="mlp",
    )(x1, *consts, *[a for args in job_args for a in args])


def kernel(x_prompt, x_sample, cache_kv_g1, cache_kv_g2, cache_kv_g3, state_ssm, norm_mix, w_in,
           ssm_lambda_re, ssm_lambda_im, ssm_log_dt, ssm_b_re, ssm_b_im, ssm_c_re, ssm_c_im, ssm_d,
           w_glu, b_glu, w_branch_ssm, w_branch_attn, w_out, norm_mlp, w_up, w_down, norm_final):
    depth = w_in.shape[0]
    assert depth == 1, "one decoder layer"
    seq = x_prompt.shape[1]
    nb = x_sample.shape[0]
    assert x_prompt.shape[0] == 1 and x_sample.shape[1] == 1
    caches = (cache_kv_g1, cache_kv_g2, cache_kv_g3)
    windows = tuple(w for w, _ in ATTN_GROUPS)
    dils = tuple(d for _, d in ATTN_GROUPS)
    for c, w in zip(caches, windows):
        assert c.shape[2] == w, "cache holds exactly one window"

    w_in_bf = w_in[0].astype(BF16)
    gain_mix = norm_mix[0][None, :]
    mix_consts = (ssm_d[0][None, :], w_glu[0].astype(BF16), b_glu[0][None, :],
                  w_branch_ssm[0].astype(BF16), w_branch_attn[0].astype(BF16), w_out[0].astype(BF16))
    mlp_consts = (norm_mlp[0][None, :], w_up[0].astype(BF16), w_down[0].astype(BF16), norm_final[None, :])
    ssm_p = (ssm_lambda_re[0], ssm_lambda_im[0], ssm_log_dt[0], ssm_b_re[0], ssm_b_im[0],
             ssm_c_re[0], ssm_c_im[0])

    xs = x_sample[:, 0]
    res = _inproj(xs, gain_mix, w_in_bf, jnp.full((1,), PAST_LEN), jnp.zeros((nb,)), nb,
                  (nb,) * N_GROUPS, (1,) * N_GROUPS)
    qs_s, tails_s = res[0:3], res[9:12]
    u_s, gs_s, ga_s = res[12:15]

    cache_t = [c[0].transpose(0, 2, 3, 4, 1).reshape(nb * 2 * GROUP_W, c.shape[2]) for c in caches]
    q_s = [q[0].astype(F32) for q in qs_s]
    k_s = [t[:, :GROUP_W] for t in tails_s]
    v_s = [t[:, GROUP_W:] for t in tails_s]
    job = lambda g, part: _CacheJob(dils[g], caches[g].shape[2], part)

    xp = x_prompt[0]
    tm_p = 256
    res = _inproj(xp, gain_mix, w_in_bf, jnp.arange(seq // tm_p) * tm_p, jnp.arange(tm_p), tm_p,
                  windows, dils, n_phase=2, jobs=(job(2, "k"),),
                  job_args=((cache_t[2], q_s[2], k_s[2], _columns(k_s[2])),))
    qs, ks, vs, tails = res[0:3], res[3:6], res[6:9], res[9:12]
    u_p, gs_p, ga_p = res[12:15]
    cache3_k, p3, pn3, lse3 = res[15:19]
    os_p, ls_p = zip(*[_attn_prompt(qs[g], ks[g], vs[g]) for g in range(N_GROUPS)])
    y_p, hp_re, hp_im = _ssm_prompt(u_p, _ssm_tables(ssm_p))
    x1_p = _mix(xp, os_p, ls_p, y_p, u_p, gs_p, ga_p, mix_consts, 2 * tm_p)
    kv_args = lambda g: (cache_t[g], q_s[g], k_s[g], v_s[g], _columns(k_s[g]), _columns(v_s[g]))
    res = _mlp(x1_p, mlp_consts, tm_p, n_phase=2, jobs=(job(2, "v"), job(1, "kv"), job(0, "kv")),
               job_args=((cache_t[2], v_s[2], _columns(v_s[2]), p3, pn3, cache3_k), kv_args(1), kv_args(0)))
    y_prompt = res[0][None]
    new_caches = (res[6], res[3], res[1])
    os_s = (res[7], res[4], res[2])
    lses = (res[8], res[5], lse3)
    kv_prompt = [t.reshape(1, 1, w, 2, HEADS, HEAD_DIM) for t, w in zip(tails, windows)]
    ssm_prompt = jnp.stack([hp_re, hp_im], axis=-1)[None, None].astype(state_ssm.dtype)

    kv_sample = [t.reshape(nb, 2, HEADS, HEAD_DIM, t.shape[1]).transpose(0, 4, 1, 2, 3)[None]
                 for t in new_caches]
    os_s = [o[None] for o in os_s]
    ls_s = [jnp.repeat(l[:, :nb].T, HEAD_DIM, axis=1)[None] for l in lses]
    st = state_ssm[0].astype(F32)
    gn = SSM_G * SSM_N
    y_s, hs_re, hs_im = _ssm_sample(u_s, st[..., 0].reshape(nb, gn), st[..., 1].reshape(nb, gn), ssm_p)
    x1_s = _mix(xs, os_s, ls_s, y_s, u_s, gs_s, ga_s, mix_consts, nb)
    y_sample = _mlp(x1_s, mlp_consts, nb)[0][:, None]
    ssm_sample = jnp.stack([hs_re.reshape(nb, SSM_G, SSM_N), hs_im.reshape(nb, SSM_G, SSM_N)],
                           axis=-1)[None].astype(state_ssm.dtype)

    return (y_prompt, y_sample, kv_prompt[0], kv_prompt[1], kv_prompt[2], ssm_prompt,
            kv_sample[0], kv_sample[1], kv_sample[2], ssm_sample)
```

```python
import functools
import math
from typing import NamedTuple

import jax
import jax.numpy as jnp
from jax import lax
from jax.experimental import pallas as pl
from jax.experimental.pallas import tpu as pltpu

F32 = jnp.float32
BF16 = jnp.bfloat16

D_MODEL = 1024
HEAD_DIM = 64
HEADS = 8
GROUP_W = HEADS * HEAD_DIM
ATTN_GROUPS = ((128, 1), (512, 4), (2048, 16))
N_GROUPS = len(ATTN_GROUPS)
QKV_W = N_GROUPS * GROUP_W
BAND = 128
Q_BLOCK = 1024
ROT_DIM = HEAD_DIM // 4
ROT_HALF = ROT_DIM // 2
ROPE_THETA = 500000.0
PAST_LEN = 8192
SSM_WIDTH = 512
SSM_P = 16
SSM_G = SSM_WIDTH // SSM_P
SSM_N = 64
SSM_T = 16
EPS = 1e-6
NEG = -1e30
LN2 = math.log(2.0)
Q_SCALE = HEAD_DIM ** -0.5 * math.log2(math.e)
LANES = 128
N_TILES = GROUP_W // LANES
MIB = 1024 * 1024

SSM_Q = SSM_WIDTH // LANES
SSM_GQ = SSM_G // SSM_Q
SSM_SQ = SSM_GQ * SSM_N
SSM_PAIRS = SSM_T // 2

OFF_Q, OFF_K, OFF_V = 0, QKV_W, 2 * QKV_W
OFF_U = 3 * QKV_W
OFF_GS = OFF_U + SSM_WIDTH
OFF_GA = OFF_GS + D_MODEL
IN_W = OFF_GA + D_MODEL


def _const_spec(shape):
    nd = len(shape)
    return pl.BlockSpec(shape, lambda *_: (0,) * nd, pipeline_mode=pl.Buffered(1))


def _params(n_axes, vmem_mib, flags=None):
    return pltpu.CompilerParams(dimension_semantics=("arbitrary",) * n_axes,
                                vmem_limit_bytes=vmem_mib * MIB, flags=flags)


def _sigmoid(x):
    return 1.0 / (1.0 + jnp.exp(-x))


def _rms(x, g):
    return x * lax.rsqrt(jnp.mean(x * x, axis=-1, keepdims=True) + EPS) * g


def _dot(a, b):
    return jnp.dot(a, b, preferred_element_type=F32)


def _phase_runner(n_phase):
    step = pl.program_id(0)
    if n_phase == 1:
        return step, lambda ph: (lambda fn: fn())
    phase = step & (n_phase - 1)
    return lax.shift_right_logical(step, int(math.log2(n_phase))), lambda ph: pl.when(phase == ph)


def _inproj_kernel(*refs, tm, dils, tail_rows, first_tail, n_phase, jobs):
    n_in = 7 + sum(JOB_ARITY[j.part][0] for j in jobs)
    x_ref, g_ref, w_ref, ca_ref, sa_ref, cb_ref, sb_ref = refs[:7]
    outs = refs[n_in:n_in + 15]
    q_refs, k_refs, v_refs, tail_refs = outs[0:3], outs[3:6], outs[6:9], outs[9:12]
    u_ref, gs_ref, ga_ref = outs[12:15]
    n_out = 15 + sum(JOB_ARITY[j.part][1] for j in jobs)
    scr, h_scr, rot_scr = refs[n_in + n_out:n_in + n_out + 3]
    i, in_phase = _phase_runner(n_phase)

    def rot(a):
        cos, s1, s2 = rot_scr[0], rot_scr[1], rot_scr[2]
        parts = []
        for j in range(GROUP_W // LANES):
            seg = a[:, j * LANES:(j + 1) * LANES]
            parts.append(seg * cos + pltpu.roll(seg, LANES - ROT_HALF, 1) * s1
                         + pltpu.roll(seg, ROT_HALF, 1) * s2)
        return jnp.concatenate(parts, axis=1)

    def put(out_ref, val, dil):
        if dil == 1:
            out_ref[0] = val.astype(BF16)
            return
        for t in range(N_TILES):
            scr[t] = val[:, t * LANES:(t + 1) * LANES]
        for r in range(dil):
            rows = [scr[t, pl.ds(r, tm // dil, stride=dil), :] for t in range(N_TILES)]
            out_ref[r] = jnp.concatenate(rows, axis=1).astype(BF16)


    def put_tail(gi, half, val):
        tb = tail_rows[gi]

        @pl.when(i >= first_tail[gi])
        def _():
            tail_refs[gi][:, half * GROUP_W:(half + 1) * GROUP_W] = val[tm - tb:, :]

    job_refs = (refs[7:n_in], refs[n_in + 15:n_in + n_out], refs[n_in + n_out + 3:])

    @pl.when(pl.program_id(0) == 0)
    def _():
        _run_cache_jobs(jobs, *job_refs, init=True)

    def ride(block):
        if block < n_phase:
            _run_cache_jobs(jobs, *job_refs)

    @in_phase(0)
    def _():
        h_scr[...] = _rms(x_ref[...], g_ref[...]).astype(BF16)
        ca, sa = ca_ref[pl.ds(i, 1), :], sa_ref[pl.ds(i, 1), :]
        cb, sb = cb_ref[...], sb_ref[...]
        cos_f = ca * cb - sa * sb
        sin_f = sa * cb + ca * sb
        hd = lax.broadcasted_iota(jnp.int32, (tm, LANES), 1) & (HEAD_DIM - 1)
        rot_scr[0] = jnp.where(hd < ROT_DIM, cos_f, 1.0)
        rot_scr[1] = jnp.where(hd < ROT_HALF, -sin_f, 0.0)
        rot_scr[2] = jnp.where((hd >= ROT_HALF) & (hd < ROT_DIM), sin_f, 0.0)
        qk = _dot(h_scr[...], w_ref[:, OFF_Q:OFF_V])
        for gi in range(N_GROUPS):
            c0 = gi * GROUP_W
            put(q_refs[gi], rot(qk[:, OFF_Q + c0:OFF_Q + c0 + GROUP_W]) * Q_SCALE, dils[gi])
            ka = rot(qk[:, OFF_K + c0:OFF_K + c0 + GROUP_W])
            put(k_refs[gi], ka, dils[gi])
            put_tail(gi, 0, ka)
        ride(0)

    @in_phase(1 % n_phase)
    def _():
        rest = _dot(h_scr[...], w_ref[:, OFF_V:IN_W])
        for gi in range(N_GROUPS):
            va = rest[:, gi * GROUP_W:(gi + 1) * GROUP_W]
            put(v_refs[gi], va, dils[gi])
            put_tail(gi, 1, va)
        u_ref[...] = rest[:, OFF_U - OFF_V:OFF_GS - OFF_V]
        gs_ref[...] = rest[:, OFF_GS - OFF_V:OFF_GA - OFF_V].astype(BF16)
        ga_ref[...] = rest[:, OFF_GA - OFF_V:].astype(BF16)
        ride(1)


def _inproj(x, gain, w_bf, tile_pos, row_pos, tm, windows, dils, n_phase=1, jobs=(), job_args=()):
    s = x.shape[0]
    ntiles = s // tm
    inv_freq = jnp.exp(-math.log(ROPE_THETA) * jnp.arange(ROT_HALF, dtype=F32) * (2.0 / ROT_DIM))
    lane_freq = jnp.tile(inv_freq, LANES // ROT_HALF)[None, :]
    ang_a = tile_pos.astype(F32)[:, None] * lane_freq
    ang_b = row_pos.astype(F32)[:, None] * lane_freq
    tail_rows = tuple(min(w, tm) for w in windows)
    first_tail = tuple(ntiles - w // tb for w, tb in zip(windows, tail_rows))
    row = lambda st: (st // n_phase, 0)
    res = lambda st: (0, st // n_phase, 0)
    qkv_shapes = [jax.ShapeDtypeStruct((d, s // d, GROUP_W), BF16) for d in dils]
    qkv_specs = [pl.BlockSpec((d, tm // d, GROUP_W), res) for d in dils]
    out_shape = (qkv_shapes * 3
                 + [jax.ShapeDtypeStruct((w, 2 * GROUP_W), F32) for w in windows]
                 + [jax.ShapeDtypeStruct((s, SSM_WIDTH), F32),
                    jax.ShapeDtypeStruct((s, D_MODEL), BF16),
                    jax.ShapeDtypeStruct((s, D_MODEL), BF16)])
    out_specs = (qkv_specs * 3
                 + [pl.BlockSpec((tb, 2 * GROUP_W), functools.partial(
                     lambda st, ft: (jnp.maximum(st // n_phase - ft, 0), 0), ft=ft))
                    for tb, ft in zip(tail_rows, first_tail)]
                 + [pl.BlockSpec((tm, SSM_WIDTH), row),
                    pl.BlockSpec((tm, D_MODEL), row),
                    pl.BlockSpec((tm, D_MODEL), row)])
    nb = job_args[0][1].shape[0] if jobs else 0
    job_in, job_out, job_shape, job_scr, job_alias = _cache_job_specs(jobs, nb, ntiles * n_phase)
    kern = functools.partial(_inproj_kernel, tm=tm, dils=dils, tail_rows=tail_rows,
                             first_tail=first_tail, n_phase=n_phase, jobs=jobs)
    return pl.pallas_call(
        kern,
        grid=(ntiles * n_phase,),
        in_specs=[pl.BlockSpec((tm, D_MODEL), row), _const_spec((1, D_MODEL)),
                  _const_spec((D_MODEL, IN_W)),
                  _const_spec((ntiles, LANES)), _const_spec((ntiles, LANES)),
                  _const_spec((tm, LANES)), _const_spec((tm, LANES))] + job_in,
        out_specs=out_specs + job_out, out_shape=out_shape + job_shape,
        scratch_shapes=[pltpu.VMEM((N_TILES, tm, LANES), F32), pltpu.VMEM((tm, D_MODEL), BF16),
                        pltpu.VMEM((3, tm, LANES), F32)] + job_scr,
        input_output_aliases={7 + i: 15 + o for i, o in job_alias},
        compiler_params=_params(1, 58), name="inproj",
    )(x, gain, w_bf, jnp.cos(ang_a), jnp.sin(ang_a), jnp.cos(ang_b), jnp.sin(ang_b),
      *[a for args in job_args for a in args])


def _attn_prompt_kernel(q_ref, kp_ref, kc_ref, vp_ref, vc_ref, o_ref, l_ref):
    j = pl.program_id(1)
    lane = lax.broadcasted_iota(jnp.int32, (BAND, LANES), 1)
    head_a = lane < HEAD_DIM
    qi = lax.broadcasted_iota(jnp.int32, (BAND, 2 * BAND), 0)
    kj = lax.broadcasted_iota(jnp.int32, (BAND, 2 * BAND), 1)
    band = (kj >= qi) & (kj <= qi + BAND)
    bias = jnp.where(band, 0.0, NEG)
    bias_first = jnp.where(band & (kj >= BAND), 0.0, NEG)
    for t in range(q_ref.shape[0] // BAND):
        rows = slice(t * BAND, (t + 1) * BAND)
        b = bias if t > 0 else jnp.where(j > 0, bias, bias_first)
        for hp in range(N_TILES):
            sl = slice(hp * LANES, (hp + 1) * LANES)
            q2 = q_ref[rows, sl]
            if t == 0:
                kk = jnp.concatenate([kp_ref[:, sl], kc_ref[0:BAND, sl]], axis=0)
                vv = jnp.concatenate([vp_ref[:, sl], vc_ref[0:BAND, sl]], axis=0)
            else:
                kk = kc_ref[(t - 1) * BAND:(t + 1) * BAND, sl]
                vv = vc_ref[(t - 1) * BAND:(t + 1) * BAND, sl]
            zero = jnp.zeros_like(q2)
            res = []
            for qm in (jnp.where(head_a, q2, zero), jnp.where(head_a, zero, q2)):
                s = lax.dot_general(qm, kk, (((1,), (1,)), ((), ())), preferred_element_type=F32) + b
                m = jnp.max(s, axis=1, keepdims=True)
                p = jnp.exp2(s - m)
                l = jnp.sum(p, axis=1, keepdims=True)
                o = _dot(p.astype(BF16), vv) * (1.0 / l)
                res.append((o, (m + jnp.log2(l)) * LN2))
            o_ref[rows, sl] = jnp.where(head_a, res[0][0], res[1][0]).astype(BF16)
            l_ref[rows, sl] = jnp.where(head_a, res[0][1], res[1][1])


def _attn_prompt(q, k, v):
    dil, n, _ = q.shape
    qb = min(Q_BLOCK, n)
    cur = lambda r, j: (r, j, 0)
    prev = lambda r, j: (r, jnp.maximum(j * (qb // BAND) - 1, 0), 0)
    blk = (None, qb, GROUP_W)
    hist = (None, BAND, GROUP_W)
    return pl.pallas_call(
        _attn_prompt_kernel,
        grid=(dil, n // qb),
        in_specs=[pl.BlockSpec(blk, cur), pl.BlockSpec(hist, prev), pl.BlockSpec(blk, cur),
                  pl.BlockSpec(hist, prev), pl.BlockSpec(blk, cur)],
        out_specs=[pl.BlockSpec(blk, cur), pl.BlockSpec(blk, cur)],
        out_shape=[jax.ShapeDtypeStruct((dil, n, GROUP_W), BF16),
                   jax.ShapeDtypeStruct((dil, n, GROUP_W), F32)],
        compiler_params=_params(2, 32), name=f"attn_prompt_d{dil}",
    )(q, k, k, v, v)


class _CacheJob(NamedTuple):
    dil: int
    lc: int
    part: str


JOB_ARITY = {"kv": (6, 3), "k": (4, 4), "v": (6, 2)}


def _cache_step(job, ins, outs, scr, init):
    dil, lc = job.dil, job.lc
    seq = pl.program_id(0)
    lane = seq & (LANES - 1)
    lane_h = lax.broadcasted_iota(jnp.int32, (HEADS, LANES), 1)
    last = lax.broadcasted_iota(jnp.int32, (GROUP_W, LANES), 1) == LANES - 1
    own = (lax.broadcasted_iota(jnp.int32, (HEADS, GROUP_W), 1) >> int(math.log2(HEAD_DIM))
           == lax.broadcasted_iota(jnp.int32, (HEADS, GROUP_W), 0))
    row = lambda ref: ref[pl.ds(seq, 1), :]
    new_col = lambda ref: pltpu.roll(ref[...], LANES - 1 - lane, 1)
    c_ref, out_ref = ins[0], outs[0]
    sel_scr = scr[0] if dil > 1 else None
    if init:
        if dil > 1:
            pos = lax.broadcasted_iota(jnp.int32, (lc, BAND), 0)
            j = lax.broadcasted_iota(jnp.int32, (lc, BAND), 1)
            sel_scr[...] = jnp.where(pos == j * dil, 1.0, 0.0).astype(BF16)
        if job.part != "v":
            outs[-1][...] = jnp.zeros_like(outs[-1])
        return

    def picked(cin):
        if dil == 1:
            return cin.astype(BF16)
        return _dot(cin.astype(BF16), sel_scr[...]).astype(BF16)

    def shift_into(row0, cin, col127):
        nt = lc // LANES
        rot = [pltpu.roll(cin[:, t * LANES:(t + 1) * LANES], LANES - 1, 1) for t in range(nt)]
        for t in range(nt):
            nxt = rot[t + 1] if t + 1 < nt else col127
            out_ref[row0:row0 + GROUP_W, t * LANES:(t + 1) * LANES] = jnp.where(last, nxt, rot[t])

    def scores(q_ref, k_ref, kt_ref, l_ref):
        kin = c_ref[0:GROUP_W, :]
        shift_into(0, kin, new_col(kt_ref))
        q_heads = jnp.where(own, row(q_ref), 0.0)
        s = _dot(q_heads.astype(BF16), picked(kin))
        s_new = jnp.sum(q_heads * row(k_ref), axis=1, keepdims=True)
        m = jnp.maximum(jnp.max(s, axis=1, keepdims=True), s_new)
        p = jnp.exp2(s - m)
        pn = jnp.exp2(s_new - m)
        l = jnp.sum(p, axis=1, keepdims=True) + pn
        l_ref[...] = jnp.where(lane_h == lane, (m + jnp.log2(l)) * LN2, l_ref[...])
        return p / l, pn / l

    def attend(row0, p, pn, v_ref, vt_ref, o_ref):
        vin = c_ref[row0:row0 + GROUP_W, :]
        shift_into(row0, vin, new_col(vt_ref))
        o_all = lax.dot_general(p.astype(BF16), picked(vin), (((1,), (1,)), ((), ())),
                                preferred_element_type=F32)
        o_all = o_all + pn * row(v_ref)
        o_ref[pl.ds(seq, 1), :] = jnp.sum(jnp.where(own, o_all, 0.0), axis=0, keepdims=True)

    if job.part == "kv":
        _, q_ref, k_ref, v_ref, kt_ref, vt_ref = ins
        _, o_ref, l_ref = outs
        p, pn = scores(q_ref, k_ref, kt_ref, l_ref)
        attend(GROUP_W, p, pn, v_ref, vt_ref, o_ref)
    elif job.part == "k":
        _, q_ref, k_ref, kt_ref = ins
        _, p_ref, pn_ref, l_ref = outs
        p, pn = scores(q_ref, k_ref, kt_ref, l_ref)
        p_ref[...] = p
        pn_ref[...] = jnp.broadcast_to(pn, pn_ref.shape)
    else:
        _, v_ref, vt_ref, p_ref, pn_ref, _ = ins
        attend(0, p_ref[...], pn_ref[:, 0:1], v_ref, vt_ref, outs[1])


def _run_cache_jobs(jobs, in_refs, out_refs, scr_refs, init=False):
    i = o = k = 0
    for job in jobs:
        ni, no = JOB_ARITY[job.part]
        ns = 1 if job.dil > 1 else 0
        _cache_step(job, in_refs[i:i + ni], out_refs[o:o + no], scr_refs[k:k + ns], init)
        i, o, k = i + ni, o + no, k + ns


def _cache_job_specs(jobs, nb, nsteps):
    in_specs, out_specs, out_shape, scratch, aliases = [], [], [], [], []
    for job in jobs:
        assert nsteps == nb, "one cache unit per grid step"
        half = {"kv": None, "k": 0, "v": 1}[job.part]
        if half is None:
            blk = pl.BlockSpec((2 * GROUP_W, job.lc), lambda st: (st, 0))
        else:
            blk = pl.BlockSpec((GROUP_W, job.lc), functools.partial(lambda st, h: (2 * st + h, 0), h=half))
        rows = _const_spec((nb, GROUP_W))
        nbp = -(-nb // LANES) * LANES
        lane_blk = lambda st: (0, st // LANES)
        cols = pl.BlockSpec((GROUP_W, LANES), lane_blk)
        lse = pl.BlockSpec((HEADS, LANES), lane_blk)
        per_seq = pl.BlockSpec((None, HEADS, BAND), lambda st: (st, 0, 0))
        cache_shape = jax.ShapeDtypeStruct((nb * 2 * GROUP_W, job.lc), F32)
        rows_shape = jax.ShapeDtypeStruct((nb, GROUP_W), F32)
        per_seq_shape = jax.ShapeDtypeStruct((nb, HEADS, BAND), F32)
        if job.part == "kv":
            in_specs += [blk, rows, rows, rows, cols, cols]
            out_specs += [blk, rows, lse]
            out_shape += [cache_shape, rows_shape, jax.ShapeDtypeStruct((HEADS, nbp), F32)]
        elif job.part == "k":
            in_specs += [blk, rows, rows, cols]
            out_specs += [blk, per_seq, per_seq, lse]
            out_shape += [cache_shape, per_seq_shape, per_seq_shape,
                          jax.ShapeDtypeStruct((HEADS, nbp), F32)]
        else:
            aliases.append((len(in_specs) + 5, len(out_specs)))
            in_specs += [blk, rows, cols, per_seq, per_seq, pl.BlockSpec(memory_space=pl.ANY)]
            out_specs += [blk, rows]
            out_shape += [cache_shape, rows_shape]
        if job.dil > 1:
            scratch += [pltpu.VMEM((job.lc, BAND), BF16)]
    return in_specs, out_specs, out_shape, scratch, aliases


def _columns(t):
    return jnp.pad(t.T, ((0, 0), (0, -t.shape[0] % LANES)))


def _ssm_discretise(lam_re, lam_im, log_dt, b_re, b_im, c_re, c_im):
    lam = lax.complex(lam_re.astype(F32), lam_im.astype(F32))
    dt = jnp.exp(log_dt.astype(F32))[:, None]
    ldt = lam * dt
    lam_bar = jnp.exp(ldt)
    b_bar = ((lam_bar - 1.0) / lam)[..., None] * lax.complex(b_re.astype(F32), b_im.astype(F32))
    c = lax.complex(c_re.astype(F32), c_im.astype(F32))
    return ldt, lam_bar, b_bar, c


def _ssm_tables(ssm_p):
    ldt, _, b_bar, c = _ssm_discretise(*ssm_p)
    t, q, gq, pr = SSM_T, SSM_Q, SSM_GQ, SSM_PAIRS
    steps = jnp.arange(t + 1, dtype=F32)
    pw = jnp.exp(ldt[None] * steps[:, None, None])
    hi = lax.Precision.HIGHEST
    eye = jnp.eye(gq, dtype=F32)
    kern = jnp.einsum('gqn,kgn,gnp->gkpq', c, pw[:t], b_bar, precision=hi).real
    kern = kern.reshape(q, gq, t, SSM_P, SSM_P)
    d_, a_, b_ = jnp.meshgrid(jnp.arange(pr), jnp.arange(2), jnp.arange(2), indexing='ij')
    lag = 2 * d_ + b_ - a_
    kl = jnp.where((lag >= 0)[None, None, :, :, :, None, None], kern[:, :, jnp.maximum(lag, 0)], 0.0)
    toep = kl.transpose(0, 2, 3, 1, 5, 4, 6).reshape(q, pr, 2 * LANES, 2 * SSM_P)
    win = pw[:t][::-1][:, :, :, None] * b_bar[None]
    win = win.reshape(pr, 2, q, gq, SSM_N, SSM_P).transpose(2, 0, 1, 3, 5, 4)
    w_in = jnp.concatenate([win.real, win.real, win.imag, win.imag], axis=-1)
    w_in = w_in.reshape(q, pr, 2 * LANES, 2 * LANES)
    cl = c[None] * pw[1:, :, None, :]
    cl = cl.reshape(pr, 2, q, gq, SSM_P, SSM_N).transpose(2, 0, 1, 3, 4, 5)
    w_out = jnp.concatenate([cl.real, cl.real, -cl.imag, -cl.imag], axis=-1)
    w_out = w_out.reshape(q, pr, 2 * LANES, 2 * LANES)
    return toep.astype(BF16), w_in.astype(BF16), w_out.astype(BF16), ldt


def _ssm_prompt_kernel(u_ref, toep_ref, win_ref, wout_ref, lp_ref, y_ref, hl_ref, carry_ref, win_bd,
                       wout_bd, toep_bd, *, nsteps):
    tt = pl.program_id(1)
    nc = u_ref.shape[0] // SSM_T
    sq = SSM_SQ

    @pl.when(tt == 0)
    def _():
        carry_ref[...] = jnp.zeros_like(carry_ref)
        r = lax.broadcasted_iota(jnp.int32, (2 * LANES, 2 * sq), 0)
        col = lax.broadcasted_iota(jnp.int32, (2 * LANES, 2 * sq), 1)
        own = (((r & (LANES - 1)) >> int(math.log2(SSM_P)))
               == ((col & (sq - 1)) >> int(math.log2(SSM_N))))
        for i in range(SSM_PAIRS):
            for src, dst in ((win_ref, win_bd), (wout_ref, wout_bd)):
                c = src[i]
                full = jnp.concatenate([c[:, :LANES]] * (sq // LANES) + [c[:, LANES:]] * (sq // LANES), axis=1)
                dst[i] = jnp.where(own, full, jnp.zeros_like(full))
        er = lax.broadcasted_iota(jnp.int32, (2 * SSM_P, 2 * LANES), 0)
        ec = lax.broadcasted_iota(jnp.int32, (2 * SSM_P, 2 * LANES), 1)
        lg_p, lg_l = int(math.log2(SSM_P)), int(math.log2(LANES))
        spread = jnp.where(((er >> lg_p) == (ec >> lg_l)) & ((er & (SSM_P - 1)) == (ec & (SSM_P - 1))),
                           1.0, 0.0).astype(BF16)
        r2 = lax.broadcasted_iota(jnp.int32, (2 * LANES, 2 * LANES), 0)
        c2 = lax.broadcasted_iota(jnp.int32, (2 * LANES, 2 * LANES), 1)
        own2 = ((r2 & (LANES - 1)) >> lg_p) == ((c2 & (LANES - 1)) >> lg_p)
        for d in range(SSM_PAIRS):
            toep_bd[d] = jnp.where(own2, _dot(toep_ref[d], spread), 0.0).astype(BF16)

    step = lambda t: u_ref[pl.ds(t, nc, stride=SSM_T), :].astype(BF16)
    a = [jnp.concatenate([step(2 * i), step(2 * i + 1)], axis=1) for i in range(SSM_PAIRS)]
    s = _dot(a[0], win_bd[0])
    for i in range(1, SSM_PAIRS):
        s = s + _dot(a[i], win_bd[i])
    xr, xi = s[:, :sq], s[:, sq:]
    row = lax.broadcasted_iota(jnp.int32, (nc, sq), 0)
    cr, ci = carry_ref[7:8, :sq], carry_ref[7:8, sq:]
    lr, li = lp_ref[0:1, :sq], lp_ref[0:1, sq:]
    xr = xr + jnp.where(row == 0, lr * cr - li * ci, 0.0)
    xi = xi + jnp.where(row == 0, lr * ci + li * cr, 0.0)

    def down(v, d):
        return jnp.where(row >= d, pltpu.roll(v, d, 0), 0.0)

    for k in range(nsteps):
        d = 1 << k
        lr, li = lp_ref[k:k + 1, :sq], lp_ref[k:k + 1, sq:]
        sr, si = down(xr, d), down(xi, d)
        xr, xi = xr + lr * sr - li * si, xi + lr * si + li * sr
    hr = jnp.where(row == 0, cr, pltpu.roll(xr, 1, 0))
    hi = jnp.where(row == 0, ci, pltpu.roll(xi, 1, 0))
    h_in = jnp.concatenate([hr, hi], axis=1).astype(BF16)
    last = jnp.concatenate([xr[nc - 8:], xi[nc - 8:]], axis=1)
    carry_ref[...] = last
    hl_ref[...] = last
    for j in range(SSM_PAIRS):
        acc = lax.dot_general(h_in, wout_bd[j], (((1,), (1,)), ((), ())), preferred_element_type=F32)
        for i in range(j + 1):
            acc = acc + _dot(a[i], toep_bd[j - i])
        y_ref[pl.ds(2 * j, nc, stride=SSM_T), :] = acc[:, :LANES]
        y_ref[pl.ds(2 * j + 1, nc, stride=SSM_T), :] = acc[:, LANES:]


def _ssm_prompt(u, tables):
    toep, w_in, w_out, ldt = tables
    s = u.shape[0]
    tile = min(s, 8192)
    nc = tile // SSM_T
    nsteps = int(math.log2(nc))
    scale = (SSM_T * (2.0 ** jnp.arange(nsteps, dtype=F32)))[:, None, None]
    lp = jnp.exp(ldt[None] * scale)
    lp = jnp.concatenate([lp.real.reshape(nsteps, SSM_Q, SSM_SQ), lp.imag.reshape(nsteps, SSM_Q, SSM_SQ)],
                         axis=2).transpose(1, 0, 2)
    per_q = lambda *shape: pl.BlockSpec((None,) + shape, lambda q, t: (q,) + (0,) * len(shape))
    y, hl = pl.pallas_call(
        functools.partial(_ssm_prompt_kernel, nsteps=nsteps),
        grid=(SSM_Q, s // tile),
        in_specs=[pl.BlockSpec((tile, LANES), lambda q, t: (t, q)),
                  per_q(SSM_PAIRS, 2 * LANES, 2 * SSM_P), per_q(SSM_PAIRS, 2 * LANES, 2 * LANES),
                  per_q(SSM_PAIRS, 2 * LANES, 2 * LANES), per_q(nsteps, 2 * SSM_SQ)],
        out_specs=[pl.BlockSpec((tile, LANES), lambda q, t: (t, q)), per_q(8, 2 * SSM_SQ)],
        out_shape=[jax.ShapeDtypeStruct((s, SSM_WIDTH), F32),
                   jax.ShapeDtypeStruct((SSM_Q, 8, 2 * SSM_SQ), F32)],
        scratch_shapes=[pltpu.VMEM((8, 2 * SSM_SQ), F32),
                        pltpu.VMEM((SSM_PAIRS, 2 * LANES, 2 * SSM_SQ), BF16),
                        pltpu.VMEM((SSM_PAIRS, 2 * LANES, 2 * SSM_SQ), BF16),
                        pltpu.VMEM((SSM_PAIRS, 2 * LANES, 2 * LANES), BF16)],
        compiler_params=_params(2, 56), name="ssm_prompt",
    )(u, toep, w_in, w_out, lp)
    hl = hl[:, 7, :]
    return y, hl[:, :SSM_SQ].reshape(SSM_G, SSM_N), hl[:, SSM_SQ:].reshape(SSM_G, SSM_N)


def _ssm_sample_kernel(u_ref, hr_ref, hi_ref, lr_ref, li_ref, br_ref, bi_ref, cr_ref, ci_ref,
                       y_ref, or_ref, oi_ref):
    u = u_ref[...].astype(BF16)
    hr, hi = hr_ref[...], hi_ref[...]
    lr, li = lr_ref[...], li_ref[...]
    nr = lr * hr - li * hi + _dot(u, br_ref[...])
    ni = lr * hi + li * hr + _dot(u, bi_ref[...])
    or_ref[...] = nr
    oi_ref[...] = ni
    y_ref[...] = _dot(nr.astype(BF16), cr_ref[...]) + _dot(ni.astype(BF16), ci_ref[...])


def _ssm_sample(u, h_re, h_im, ssm_p):
    nb = u.shape[0]
    _, lam_bar, b_bar, c = _ssm_discretise(*ssm_p)
    eye = jnp.eye(SSM_G, dtype=F32)
    gn = SSM_G * SSM_N
    bd = lambda part, eq, shape: jnp.einsum(eq, part, eye).reshape(shape).astype(BF16)
    args = (u, h_re, h_im, lam_bar.real.reshape(1, gn), lam_bar.imag.reshape(1, gn),
            bd(b_bar.real, 'gnp,gh->gphn', (SSM_WIDTH, gn)), bd(b_bar.imag, 'gnp,gh->gphn', (SSM_WIDTH, gn)),
            bd(c.real, 'gpn,gh->gnhp', (gn, SSM_WIDTH)), bd(-c.imag, 'gpn,gh->gnhp', (gn, SSM_WIDTH)))
    return pl.pallas_call(
        _ssm_sample_kernel,
        grid=(1,),
        in_specs=[_const_spec(a.shape) for a in args],
        out_specs=[_const_spec((nb, SSM_WIDTH)), _const_spec((nb, gn)), _const_spec((nb, gn))],
        out_shape=[jax.ShapeDtypeStruct((nb, SSM_WIDTH), F32),
                   jax.ShapeDtypeStruct((nb, gn), F32), jax.ShapeDtypeStruct((nb, gn), F32)],
        compiler_params=_params(1, 32), name="ssm_sample",
    )(*args)


def _mix_kernel(x_ref, o1_ref, o2_ref, o3_ref, l1_ref, l2_ref, l3_ref, y_ref, u_ref, gs_ref, ga_ref,
                d_ref, wglu_ref, bglu_ref, wbs_ref, wba_ref, wout_ref, x1_ref, *scr, dils):
    tm = x_ref.shape[0]

    def natural(ref, dil, buf):
        if dil == 1:
            return ref[0]
        for r in range(dil):
            for t in range(N_TILES):
                buf[t, pl.ds(r, tm // dil, stride=dil), :] = ref[r, :, t * LANES:(t + 1) * LANES].astype(F32)
        return jnp.concatenate([buf[t] for t in range(N_TILES)], axis=1)

    o1, o2, o3 = (natural(r, d, scr[k]) for k, (r, d) in enumerate(zip((o1_ref, o2_ref, o3_ref), dils)))
    l1, l2, l3 = (natural(r, d, scr[3 + k]) for k, (r, d) in enumerate(zip((l1_ref, l2_ref, l3_ref), dils)))
    m = jnp.maximum(jnp.maximum(l1, l2), l3)
    e1, e2, e3 = jnp.exp(l1 - m), jnp.exp(l2 - m), jnp.exp(l3 - m)
    den = e1 + e2 + e3
    attn = (e1 / den) * o1 + (e2 / den) * o2 + (e3 / den) * o3
    y = jax.nn.gelu(y_ref[...] + d_ref[...] * u_ref[...])
    ssm_y = y * _sigmoid(_dot(y.astype(BF16), wglu_ref[...]) + bglu_ref[...])
    z = (_sigmoid(gs_ref[...].astype(F32)) * _dot(ssm_y.astype(BF16), wbs_ref[...])
         + _sigmoid(ga_ref[...].astype(F32)) * _dot(attn.astype(BF16), wba_ref[...]))
    x1_ref[...] = x_ref[...] + _dot(z.astype(BF16), wout_ref[...])


def _mix(x, os_, ls_, y_ssm, u, gs, ga, consts, tm):
    s = x.shape[0]
    row = lambda i: (i, 0)
    res = lambda i: (0, i, 0)
    dils = tuple(o.shape[0] for o in os_)
    grouped = [*os_, *ls_]
    flat = [y_ssm, u, gs, ga]
    return pl.pallas_call(
        functools.partial(_mix_kernel, dils=dils),
        grid=(s // tm,),
        in_specs=([pl.BlockSpec((tm, D_MODEL), row)]
                  + [pl.BlockSpec((a.shape[0], tm // a.shape[0], GROUP_W), res) for a in grouped]
                  + [pl.BlockSpec((tm, a.shape[1]), row) for a in flat]
                  + [_const_spec(c.shape) for c in consts]),
        out_specs=pl.BlockSpec((tm, D_MODEL), row),
        out_shape=jax.ShapeDtypeStruct((s, D_MODEL), F32),
        scratch_shapes=[pltpu.VMEM((N_TILES, tm, LANES), F32)] * 6,
        compiler_params=_params(1, 48), name="mix",
    )(x, *grouped, *flat, *consts)


def _mlp_kernel(*refs, n_phase, jobs):
    x1_ref, nm_ref, wup_ref, wdn_ref, nf_ref = refs[:5]
    n_in = 5 + sum(JOB_ARITY[j.part][0] for j in jobs)
    out_ref = refs[n_in]
    n_out = 1 + sum(JOB_ARITY[j.part][1] for j in jobs)
    up_scr = refs[n_in + n_out]
    _, in_phase = _phase_runner(n_phase)

    job_refs = (refs[5:n_in], refs[n_in + 1:n_in + n_out], refs[n_in + n_out + 1:])

    @pl.when(pl.program_id(0) == 0)
    def _():
        _run_cache_jobs(jobs, *job_refs, init=True)

    def ride(block):
        if block < n_phase:
            _run_cache_jobs(jobs, *job_refs)

    @in_phase(0)
    def _():
        hm = _rms(x1_ref[...], nm_ref[...]).astype(BF16)
        up = jnp.maximum(_dot(hm, wup_ref[...]), 0.0)
        up_scr[...] = (up * up).astype(BF16)
        ride(0)

    @in_phase(1 % n_phase)
    def _():
        x2 = x1_ref[...] + _dot(up_scr[...], wdn_ref[...])
        out_ref[...] = _rms(x2, nf_ref[...])
        ride(1)


def _mlp(x1, consts, tm, n_phase=1, jobs=(), job_args=()):
    s = x1.shape[0]
    ntiles = s // tm
    row = lambda st: (st // n_phase, 0)
    nb = job_args[0][1].shape[0] if jobs else 0
    job_in, job_out, job_shape, job_scr, job_alias = _cache_job_specs(jobs, nb, ntiles * n_phase)
    return pl.pallas_call(
        functools.partial(_mlp_kernel, n_phase=n_phase, jobs=jobs),
        grid=(ntiles * n_phase,),
        in_specs=[pl.BlockSpec((tm, D_MODEL), row)] + [_const_spec(c.shape) for c in consts] + job_in,
        out_specs=[pl.BlockSpec((tm, D_MODEL), row)] + job_out,
        out_shape=[jax.ShapeDtypeStruct((s, D_MODEL), F32)] + job_shape,
        scratch_shapes=[pltpu.VMEM((tm, consts[1].shape[1]), BF16)] + job_scr,
        input_output_aliases={5 + i: 1 + o for i, o in job_alias},
        compiler_params=_params(1, 58), name="mlp",
    )(x1, *consts, *[a for args in job_args for a in args])


def kernel(x_prompt, x_sample, cache_kv_g1, cache_kv_g2, cache_kv_g3, state_ssm, norm_mix, w_in,
           ssm_lambda_re, ssm_lambda_im, ssm_log_dt, ssm_b_re, ssm_b_im, ssm_c_re, ssm_c_im, ssm_d,
           w_glu, b_glu, w_branch_ssm, w_branch_attn, w_out, norm_mlp, w_up, w_down, norm_final):
    depth = w_in.shape[0]
    assert depth == 1, "one decoder layer"
    seq = x_prompt.shape[1]
    nb = x_sample.shape[0]
    assert x_prompt.shape[0] == 1 and x_sample.shape[1] == 1
    caches = (cache_kv_g1, cache_kv_g2, cache_kv_g3)
    windows = tuple(w for w, _ in ATTN_GROUPS)
    dils = tuple(d for _, d in ATTN_GROUPS)
    for c, w in zip(caches, windows):
        assert c.shape[2] == w, "cache holds exactly one window"

    w_in_bf = w_in[0].astype(BF16)
    gain_mix = norm_mix[0][None, :]
    mix_consts = (ssm_d[0][None, :], w_glu[0].astype(BF16), b_glu[0][None, :],
                  w_branch_ssm[0].astype(BF16), w_branch_attn[0].astype(BF16), w_out[0].astype(BF16))
    mlp_consts = (norm_mlp[0][None, :], w_up[0].astype(BF16), w_down[0].astype(BF16), norm_final[None, :])
    ssm_p = (ssm_lambda_re[0], ssm_lambda_im[0], ssm_log_dt[0], ssm_b_re[0], ssm_b_im[0],
             ssm_c_re[0], ssm_c_im[0])

    xs = x_sample[:, 0]
    res = _inproj(xs, gain_mix, w_in_bf, jnp.full((1,), PAST_LEN), jnp.zeros((nb,)), nb,
                  (nb,) * N_GROUPS, (1,) * N_GROUPS)
    qs_s, tails_s = res[0:3], res[9:12]
    u_s, gs_s, ga_s = res[12:15]

    cache_t = [c[0].transpose(0, 2, 3, 4, 1).reshape(nb * 2 * GROUP_W, c.shape[2]) for c in caches]
    q_s = [q[0].astype(F32) for q in qs_s]
    k_s = [t[:, :GROUP_W] for t in tails_s]
    v_s = [t[:, GROUP_W:] for t in tails_s]
    job = lambda g, part: _CacheJob(dils[g], caches[g].shape[2], part)

    xp = x_prompt[0]
    tm_p = 256
    res = _inproj(xp, gain_mix, w_in_bf, jnp.arange(seq // tm_p) * tm_p, jnp.arange(tm_p), tm_p,
                  windows, dils, n_phase=2, jobs=(job(2, "k"),),
                  job_args=((cache_t[2], q_s[2], k_s[2], _columns(k_s[2])),))
    qs, ks, vs, tails = res[0:3], res[3:6], res[6:9], res[9:12]
    u_p, gs_p, ga_p = res[12:15]
    cache3_k, p3, pn3, lse3 = res[15:19]
    os_p, ls_p = zip(*[_attn_prompt(qs[g], ks[g], vs[g]) for g in range(N_GROUPS)])
    y_p, hp_re, hp_im = _ssm_prompt(u_p, _ssm_tables(ssm_p))
    x1_p = _mix(xp, os_p, ls_p, y_p, u_p, gs_p, ga_p, mix_consts, 2 * tm_p)
    kv_args = lambda g: (cache_t[g], q_s[g], k_s[g], v_s[g], _columns(k_s[g]), _columns(v_s[g]))
    res = _mlp(x1_p, mlp_consts, tm_p, n_phase=2, jobs=(job(2, "v"), job(1, "kv"), job(0, "kv")),
               job_args=((cache_t[2], v_s[2], _columns(v_s[2]), p3, pn3, cache3_k), kv_args(1), kv_args(0)))
    y_prompt = res[0][None]
    new_caches = (res[6], res[3], res[1])
    os_s = (res[7], res[4], res[2])
    lses = (res[8], res[5], lse3)
    kv_prompt = [t.reshape(1, 1, w, 2, HEADS, HEAD_DIM) for t, w in zip(tails, windows)]
    ssm_prompt = jnp.stack([hp_re, hp_im], axis=-1)[None, None].astype(state_ssm.dtype)

    kv_sample = [t.reshape(nb, 2, HEADS, HEAD_DIM, t.shape[1]).transpose(0, 4, 1, 2, 3)[None]
                 for t in new_caches]
    os_s = [o[None] for o in os_s]
    ls_s = [jnp.repeat(l[:, :nb].T, HEAD_DIM, axis=1)[None] for l in lses]
    st = state_ssm[0].astype(F32)
    gn = SSM_G * SSM_N
    y_s, hs_re, hs_im = _ssm_sample(u_s, st[..., 0].reshape(nb, gn), st[..., 1].reshape(nb, gn), ssm_p)
    x1_s = _mix(xs, os_s, ls_s, y_s, u_s, gs_s, ga_s, mix_consts, nb)
    y_sample = _mlp(x1_s, mlp_consts, nb)[0][:, None]
    ssm_sample = jnp.stack([hs_re.reshape(nb, SSM_G, SSM_N), hs_im.reshape(nb, SSM_G, SSM_N)],
                           axis=-1)[None].astype(state_ssm.dtype)

    return (y_prompt, y_sample, kv_prompt[0], kv_prompt[1], kv_prompt[2], ssm_prompt,
            kv_sample[0], kv_sample[1], kv_sample[2], ssm_sample)
```

```python
import functools
import math
from typing import NamedTuple

import jax
import jax.numpy as jnp
from jax import lax
from jax.experimental import pallas as pl
from jax.experimental.pallas import tpu as pltpu

F32 = jnp.float32
BF16 = jnp.bfloat16

D_MODEL = 1024
HEAD_DIM = 64
HEADS = 8
GROUP_W = HEADS * HEAD_DIM
ATTN_GROUPS = ((128, 1), (512, 4), (2048, 16))
N_GROUPS = len(ATTN_GROUPS)
QKV_W = N_GROUPS * GROUP_W
BAND = 128
Q_BLOCK = 1024
ROT_DIM = HEAD_DIM // 4
ROT_HALF = ROT_DIM // 2
ROPE_THETA = 500000.0
PAST_LEN = 8192
SSM_WIDTH = 512
SSM_P = 16
SSM_G = SSM_WIDTH // SSM_P
SSM_N = 64
SSM_T = 16
EPS = 1e-6
NEG = -1e30
LN2 = math.log(2.0)
Q_SCALE = HEAD_DIM ** -0.5 * math.log2(math.e)
LANES = 128
N_TILES = GROUP_W // LANES
MIB = 1024 * 1024

SSM_Q = SSM_WIDTH // LANES
SSM_GQ = SSM_G // SSM_Q
SSM_SQ = SSM_GQ * SSM_N
SSM_PAIRS = SSM_T // 2

OFF_Q, OFF_K, OFF_V = 0, QKV_W, 2 * QKV_W
OFF_U = 3 * QKV_W
OFF_GS = OFF_U + SSM_WIDTH
OFF_GA = OFF_GS + D_MODEL
IN_W = OFF_GA + D_MODEL


def _const_spec(shape):
    nd = len(shape)
    return pl.BlockSpec(shape, lambda *_: (0,) * nd, pipeline_mode=pl.Buffered(1))


def _params(n_axes, vmem_mib, flags=None):
    return pltpu.CompilerParams(dimension_semantics=("arbitrary",) * n_axes,
                                vmem_limit_bytes=vmem_mib * MIB, flags=flags)


def _sigmoid(x):
    return 1.0 / (1.0 + jnp.exp(-x))


def _rms(x, g):
    return x * lax.rsqrt(jnp.mean(x * x, axis=-1, keepdims=True) + EPS) * g


def _dot(a, b):
    return jnp.dot(a, b, preferred_element_type=F32)


def _phase_runner(n_phase):
    step = pl.program_id(0)
    if n_phase == 1:
        return step, lambda ph: (lambda fn: fn())
    phase = step & (n_phase - 1)
    return lax.shift_right_logical(step, int(math.log2(n_phase))), lambda ph: pl.when(phase == ph)


def _inproj_kernel(*refs, tm, dils, tail_rows, first_tail, n_phase, jobs):
    n_in = 7 + sum(JOB_ARITY[j.part][0] for j in jobs)
    x_ref, g_ref, w_ref, ca_ref, sa_ref, cb_ref, sb_ref = refs[:7]
    outs = refs[n_in:n_in + 15]
    q_refs, k_refs, v_refs, tail_refs = outs[0:3], outs[3:6], outs[6:9], outs[9:12]
    u_ref, gs_ref, ga_ref = outs[12:15]
    n_out = 15 + sum(JOB_ARITY[j.part][1] for j in jobs)
    scr, h_scr, rot_scr = refs[n_in + n_out:n_in + n_out + 3]
    i, in_phase = _phase_runner(n_phase)

    def rot(a):
        cos, s1, s2 = rot_scr[0], rot_scr[1], rot_scr[2]
        parts = []
        for j in range(GROUP_W // LANES):
            seg = a[:, j * LANES:(j + 1) * LANES]
            parts.append(seg * cos + pltpu.roll(seg, LANES - ROT_HALF, 1) * s1
                         + pltpu.roll(seg, ROT_HALF, 1) * s2)
        return jnp.concatenate(parts, axis=1)

    def put(out_ref, val, dil):
        if dil == 1:
            out_ref[0] = val.astype(BF16)
            return
        for t in range(N_TILES):
            scr[t] = val[:, t * LANES:(t + 1) * LANES]
        for r in range(dil):
            rows = [scr[t, pl.ds(r, tm // dil, stride=dil), :] for t in range(N_TILES)]
            out_ref[r] = jnp.concatenate(rows, axis=1).astype(BF16)


    def put_tail(gi, half, val):
        tb = tail_rows[gi]

        @pl.when(i >= first_tail[gi])
        def _():
            tail_refs[gi][:, half * GROUP_W:(half + 1) * GROUP_W] = val[tm - tb:, :]

    job_refs = (refs[7:n_in], refs[n_in + 15:n_in + n_out], refs[n_in + n_out + 3:])

    @pl.when(pl.program_id(0) == 0)
    def _():
        _run_cache_jobs(jobs, *job_refs, init=True)

    def ride(block):
        if block < n_phase:
            _run_cache_jobs(jobs, *job_refs)

    @in_phase(0)
    def _():
        h_scr[...] = _rms(x_ref[...], g_ref[...]).astype(BF16)
        ca, sa = ca_ref[pl.ds(i, 1), :], sa_ref[pl.ds(i, 1), :]
        cb, sb = cb_ref[...], sb_ref[...]
        cos_f = ca * cb - sa * sb
        sin_f = sa * cb + ca * sb
        hd = lax.broadcasted_iota(jnp.int32, (tm, LANES), 1) & (HEAD_DIM - 1)
        rot_scr[0] = jnp.where(hd < ROT_DIM, cos_f, 1.0)
        rot_scr[1] = jnp.where(hd < ROT_HALF, -sin_f, 0.0)
        rot_scr[2] = jnp.where((hd >= ROT_HALF) & (hd < ROT_DIM), sin_f, 0.0)
        qk = _dot(h_scr[...], w_ref[:, OFF_Q:OFF_V])
        for gi in range(N_GROUPS):
            c0 = gi * GROUP_W
            put(q_refs[gi], rot(qk[:, OFF_Q + c0:OFF_Q + c0 + GROUP_W]) * Q_SCALE, dils[gi])
            ka = rot(qk[:, OFF_K + c0:OFF_K + c0 + GROUP_W])
            put(k_refs[gi], ka, dils[gi])
            put_tail(gi, 0, ka)
        ride(0)

    @in_phase(1 % n_phase)
    def _():
        rest = _dot(h_scr[...], w_ref[:, OFF_V:IN_W])
        for gi in range(N_GROUPS):
            va = rest[:, gi * GROUP_W:(gi + 1) * GROUP_W]
            put(v_refs[gi], va, dils[gi])
            put_tail(gi, 1, va)
        u_ref[...] = rest[:, OFF_U - OFF_V:OFF_GS - OFF_V]
        gs_ref[...] = rest[:, OFF_GS - OFF_V:OFF_GA - OFF_V].astype(BF16)
        ga_ref[...] = rest[:, OFF_GA - OFF_V:].astype(BF16)
        ride(1)


def _inproj(x, gain, w_bf, tile_pos, row_pos, tm, windows, dils, n_phase=1, jobs=(), job_args=()):
    s = x.shape[0]
    ntiles = s // tm
    inv_freq = jnp.exp(-math.log(ROPE_THETA) * jnp.arange(ROT_HALF, dtype=F32) * (2.0 / ROT_DIM))
    lane_freq = jnp.tile(inv_freq, LANES // ROT_HALF)[None, :]
    ang_a = tile_pos.astype(F32)[:, None] * lane_freq
    ang_b = row_pos.astype(F32)[:, None] * lane_freq
    tail_rows = tuple(min(w, tm) for w in windows)
    first_tail = tuple(ntiles - w // tb for w, tb in zip(windows, tail_rows))
    row = lambda st: (st // n_phase, 0)
    res = lambda st: (0, st // n_phase, 0)
    qkv_shapes = [jax.ShapeDtypeStruct((d, s // d, GROUP_W), BF16) for d in dils]
    qkv_specs = [pl.BlockSpec((d, tm // d, GROUP_W), res) for d in dils]
    out_shape = (qkv_shapes * 3
                 + [jax.ShapeDtypeStruct((w, 2 * GROUP_W), F32) for w in windows]
                 + [jax.ShapeDtypeStruct((s, SSM_WIDTH), F32),
                    jax.ShapeDtypeStruct((s, D_MODEL), BF16),
                    jax.ShapeDtypeStruct((s, D_MODEL), BF16)])
    out_specs = (qkv_specs * 3
                 + [pl.BlockSpec((tb, 2 * GROUP_W), functools.partial(
                     lambda st, ft: (jnp.maximum(st // n_phase - ft, 0), 0), ft=ft))
                    for tb, ft in zip(tail_rows, first_tail)]
                 + [pl.BlockSpec((tm, SSM_WIDTH), row),
                    pl.BlockSpec((tm, D_MODEL), row),
                    pl.BlockSpec((tm, D_MODEL), row)])
    nb = job_args[0][1].shape[0] if jobs else 0
    job_in, job_out, job_shape, job_scr, job_alias = _cache_job_specs(jobs, nb, ntiles * n_phase)
    kern = functools.partial(_inproj_kernel, tm=tm, dils=dils, tail_rows=tail_rows,
                             first_tail=first_tail, n_phase=n_phase, jobs=jobs)
    return pl.pallas_call(
        kern,
        grid=(ntiles * n_phase,),
        in_specs=[pl.BlockSpec((tm, D_MODEL), row), _const_spec((1, D_MODEL)),
                  _const_spec((D_MODEL, IN_W)),
                  _const_spec((ntiles, LANES)), _const_spec((ntiles, LANES)),
                  _const_spec((tm, LANES)), _const_spec((tm, LANES))] + job_in,
        out_specs=out_specs + job_out, out_shape=out_shape + job_shape,
        scratch_shapes=[pltpu.VMEM((N_TILES, tm, LANES), F32), pltpu.VMEM((tm, D_MODEL), BF16),
                        pltpu.VMEM((3, tm, LANES), F32)] + job_scr,
        input_output_aliases={7 + i: 15 + o for i, o in job_alias},
        compiler_params=_params(1, 58), name="inproj",
    )(x, gain, w_bf, jnp.cos(ang_a), jnp.sin(ang_a), jnp.cos(ang_b), jnp.sin(ang_b),
      *[a for args in job_args for a in args])


def _attn_prompt_kernel(q_ref, kp_ref, kc_ref, vp_ref, vc_ref, o_ref, l_ref):
    j = pl.program_id(1)
    lane = lax.broadcasted_iota(jnp.int32, (BAND, LANES), 1)
    head_a = lane < HEAD_DIM
    qi = lax.broadcasted_iota(jnp.int32, (BAND, 2 * BAND), 0)
    kj = lax.broadcasted_iota(jnp.int32, (BAND, 2 * BAND), 1)
    band = (kj >= qi) & (kj <= qi + BAND)
    bias = jnp.where(band, 0.0, NEG)
    bias_first = jnp.where(band & (kj >= BAND), 0.0, NEG)
    for t in range(q_ref.shape[0] // BAND):
        rows = slice(t * BAND, (t + 1) * BAND)
        b = bias if t > 0 else jnp.where(j > 0, bias, bias_first)
        for hp in range(N_TILES):
            sl = slice(hp * LANES, (hp + 1) * LANES)
            q2 = q_ref[rows, sl]
            if t == 0:
                kk = jnp.concatenate([kp_ref[:, sl], kc_ref[0:BAND, sl]], axis=0)
                vv = jnp.concatenate([vp_ref[:, sl], vc_ref[0:BAND, sl]], axis=0)
            else:
                kk = kc_ref[(t - 1) * BAND:(t + 1) * BAND, sl]
                vv = vc_ref[(t - 1) * BAND:(t + 1) * BAND, sl]
            zero = jnp.zeros_like(q2)
            res = []
            for qm in (jnp.where(head_a, q2, zero), jnp.where(head_a, zero, q2)):
                s = lax.dot_general(qm, kk, (((1,), (1,)), ((), ())), preferred_element_type=F32) + b
                m = jnp.max(s, axis=1, keepdims=True)
                p = jnp.exp2(s - m)
                l = jnp.sum(p, axis=1, keepdims=True)
                o = _dot(p.astype(BF16), vv) * (1.0 / l)
                res.append((o, (m + jnp.log2(l)) * LN2))
            o_ref[rows, sl] = jnp.where(head_a, res[0][0], res[1][0]).astype(BF16)
            l_ref[rows, sl] = jnp.where(head_a, res[0][1], res[1][1])


def _attn_prompt(q, k, v):
    dil, n, _ = q.shape
    qb = min(Q_BLOCK, n)
    cur = lambda r, j: (r, j, 0)
    prev = lambda r, j: (r, jnp.maximum(j * (qb // BAND) - 1, 0), 0)
    blk = (None, qb, GROUP_W)
    hist = (None, BAND, GROUP_W)
    return pl.pallas_call(
        _attn_prompt_kernel,
        grid=(dil, n // qb),
        in_specs=[pl.BlockSpec(blk, cur), pl.BlockSpec(hist, prev), pl.BlockSpec(blk, cur),
                  pl.BlockSpec(hist, prev), pl.BlockSpec(blk, cur)],
        out_specs=[pl.BlockSpec(blk, cur), pl.BlockSpec(blk, cur)],
        out_shape=[jax.ShapeDtypeStruct((dil, n, GROUP_W), BF16),
                   jax.ShapeDtypeStruct((dil, n, GROUP_W), F32)],
        compiler_params=_params(2, 32), name=f"attn_prompt_d{dil}",
    )(q, k, k, v, v)


class _CacheJob(NamedTuple):
    dil: int
    lc: int
    part: str


JOB_ARITY = {"kv": (6, 3), "k": (4, 4), "v": (6, 2)}


def _cache_step(job, ins, outs, scr, init):
    dil, lc = job.dil, job.lc
    seq = pl.program_id(0)
    lane = seq & (LANES - 1)
    lane_h = lax.broadcasted_iota(jnp.int32, (HEADS, LANES), 1)
    last = lax.broadcasted_iota(jnp.int32, (GROUP_W, LANES), 1) == LANES - 1
    own = (lax.broadcasted_iota(jnp.int32, (HEADS, GROUP_W), 1) >> int(math.log2(HEAD_DIM))
           == lax.broadcasted_iota(jnp.int32, (HEADS, GROUP_W), 0))
    row = lambda ref: ref[pl.ds(seq, 1), :]
    new_col = lambda ref: pltpu.roll(ref[...], LANES - 1 - lane, 1)
    c_ref, out_ref = ins[0], outs[0]
    if init:
        if job.part != "v":
            outs[-1][...] = jnp.zeros_like(outs[-1])
        return
    reads = (lax.broadcasted_iota(jnp.int32, (HEADS, lc), 1) & (dil - 1)) == 0

    def shift_into(row0, cin, col127):
        nt = lc // LANES
        rot = [pltpu.roll(cin[:, t * LANES:(t + 1) * LANES], LANES - 1, 1) for t in range(nt)]
        for t in range(nt):
            nxt = rot[t + 1] if t + 1 < nt else col127
            out_ref[row0:row0 + GROUP_W, t * LANES:(t + 1) * LANES] = jnp.where(last, nxt, rot[t])

    def scores(q_ref, k_ref, kt_ref, l_ref):
        kin = c_ref[0:GROUP_W, :]
        shift_into(0, kin, new_col(kt_ref))
        q_heads = jnp.where(own, row(q_ref), 0.0)
        s = _dot(q_heads.astype(BF16), kin.astype(BF16))
        s = jnp.where(reads, s, NEG)
        s_new = jnp.sum(q_heads * row(k_ref), axis=1, keepdims=True)
        m = jnp.maximum(jnp.max(s, axis=1, keepdims=True), s_new)
        p = jnp.exp2(s - m)
        pn = jnp.exp2(s_new - m)
        l = jnp.sum(p, axis=1, keepdims=True) + pn
        l_ref[...] = jnp.where(lane_h == lane, (m + jnp.log2(l)) * LN2, l_ref[...])
        return p / l, pn / l

    def attend(row0, p, pn, v_ref, vt_ref, o_ref):
        vin = c_ref[row0:row0 + GROUP_W, :]
        shift_into(row0, vin, new_col(vt_ref))
        o_all = lax.dot_general(p.astype(BF16), vin.astype(BF16), (((1,), (1,)), ((), ())),
                                preferred_element_type=F32)
        o_all = o_all + pn * row(v_ref)
        o_ref[pl.ds(seq, 1), :] = jnp.sum(jnp.where(own, o_all, 0.0), axis=0, keepdims=True)

    if job.part == "kv":
        _, q_ref, k_ref, v_ref, kt_ref, vt_ref = ins
        _, o_ref, l_ref = outs
        p, pn = scores(q_ref, k_ref, kt_ref, l_ref)
        attend(GROUP_W, p, pn, v_ref, vt_ref, o_ref)
    elif job.part == "k":
        _, q_ref, k_ref, kt_ref = ins
        _, p_ref, pn_ref, l_ref = outs
        p, pn = scores(q_ref, k_ref, kt_ref, l_ref)
        p_ref[...] = p
        pn_ref[...] = jnp.broadcast_to(pn, pn_ref.shape)
    else:
        _, v_ref, vt_ref, p_ref, pn_ref, _ = ins
        attend(0, p_ref[...], pn_ref[:, 0:1], v_ref, vt_ref, outs[1])


def _run_cache_jobs(jobs, in_refs, out_refs, scr_refs, init=False):
    i = o = k = 0
    for job in jobs:
        ni, no = JOB_ARITY[job.part]
        ns = 0
        _cache_step(job, in_refs[i:i + ni], out_refs[o:o + no], scr_refs[k:k + ns], init)
        i, o, k = i + ni, o + no, k + ns


def _cache_job_specs(jobs, nb, nsteps):
    in_specs, out_specs, out_shape, scratch, aliases = [], [], [], [], []
    for job in jobs:
        assert nsteps == nb, "one cache unit per grid step"
        half = {"kv": None, "k": 0, "v": 1}[job.part]
        if half is None:
            blk = pl.BlockSpec((2 * GROUP_W, job.lc), lambda st: (st, 0))
        else:
            blk = pl.BlockSpec((GROUP_W, job.lc), functools.partial(lambda st, h: (2 * st + h, 0), h=half))
        rows = _const_spec((nb, GROUP_W))
        nbp = -(-nb // LANES) * LANES
        lane_blk = lambda st: (0, st // LANES)
        cols = pl.BlockSpec((GROUP_W, LANES), lane_blk)
        lse = pl.BlockSpec((HEADS, LANES), lane_blk)
        per_seq = pl.BlockSpec((None, HEADS, job.lc), lambda st: (st, 0, 0))
        per_seq_new = pl.BlockSpec((None, HEADS, LANES), lambda st: (st, 0, 0))
        cache_shape = jax.ShapeDtypeStruct((nb * 2 * GROUP_W, job.lc), F32)
        rows_shape = jax.ShapeDtypeStruct((nb, GROUP_W), F32)
        per_seq_shape = jax.ShapeDtypeStruct((nb, HEADS, job.lc), F32)
        per_seq_new_shape = jax.ShapeDtypeStruct((nb, HEADS, LANES), F32)
        if job.part == "kv":
            in_specs += [blk, rows, rows, rows, cols, cols]
            out_specs += [blk, rows, lse]
            out_shape += [cache_shape, rows_shape, jax.ShapeDtypeStruct((HEADS, nbp), F32)]
        elif job.part == "k":
            in_specs += [blk, rows, rows, cols]
            out_specs += [blk, per_seq, per_seq_new, lse]
            out_shape += [cache_shape, per_seq_shape, per_seq_new_shape,
                          jax.ShapeDtypeStruct((HEADS, nbp), F32)]
        else:
            aliases.append((len(in_specs) + 5, len(out_specs)))
            in_specs += [blk, rows, cols, per_seq, per_seq_new, pl.BlockSpec(memory_space=pl.ANY)]
            out_specs += [blk, rows]
            out_shape += [cache_shape, rows_shape]
    return in_specs, out_specs, out_shape, scratch, aliases


def _columns(t):
    return jnp.pad(t.T, ((0, 0), (0, -t.shape[0] % LANES)))


def _ssm_discretise(lam_re, lam_im, log_dt, b_re, b_im, c_re, c_im):
    lam = lax.complex(lam_re.astype(F32), lam_im.astype(F32))
    dt = jnp.exp(log_dt.astype(F32))[:, None]
    ldt = lam * dt
    lam_bar = jnp.exp(ldt)
    b_bar = ((lam_bar - 1.0) / lam)[..., None] * lax.complex(b_re.astype(F32), b_im.astype(F32))
    c = lax.complex(c_re.astype(F32), c_im.astype(F32))
    return ldt, lam_bar, b_bar, c


def _ssm_tables(ssm_p):
    ldt, _, b_bar, c = _ssm_discretise(*ssm_p)
    t, q, gq, pr = SSM_T, SSM_Q, SSM_GQ, SSM_PAIRS
    steps = jnp.arange(t + 1, dtype=F32)
    pw = jnp.exp(ldt[None] * steps[:, None, None])
    hi = lax.Precision.HIGHEST
    eye = jnp.eye(gq, dtype=F32)
    kern = jnp.einsum('gqn,kgn,gnp->gkpq', c, pw[:t], b_bar, precision=hi).real
    kern = kern.reshape(q, gq, t, SSM_P, SSM_P)
    d_, a_, b_ = jnp.meshgrid(jnp.arange(pr), jnp.arange(2), jnp.arange(2), indexing='ij')
    lag = 2 * d_ + b_ - a_
    kl = jnp.where((lag >= 0)[None, None, :, :, :, None, None], kern[:, :, jnp.maximum(lag, 0)], 0.0)
    toep = kl.transpose(0, 2, 3, 1, 5, 4, 6).reshape(q, pr, 2 * LANES, 2 * SSM_P)
    win = pw[:t][::-1][:, :, :, None] * b_bar[None]
    win = win.reshape(pr, 2, q, gq, SSM_N, SSM_P).transpose(2, 0, 1, 3, 5, 4)
    w_in = jnp.concatenate([win.real, win.real, win.imag, win.imag], axis=-1)
    w_in = w_in.reshape(q, pr, 2 * LANES, 2 * LANES)
    cl = c[None] * pw[1:, :, None, :]
    cl = cl.reshape(pr, 2, q, gq, SSM_P, SSM_N).transpose(2, 0, 1, 3, 4, 5)
    w_out = jnp.concatenate([cl.real, cl.real, -cl.imag, -cl.imag], axis=-1)
    w_out = w_out.reshape(q, pr, 2 * LANES, 2 * LANES)
    return toep.astype(BF16), w_in.astype(BF16), w_out.astype(BF16), ldt


def _ssm_prompt_kernel(u_ref, toep_ref, win_ref, wout_ref, lp_ref, y_ref, hl_ref, carry_ref, win_bd,
                       wout_bd, toep_bd, *, nsteps):
    tt = pl.program_id(1)
    nc = u_ref.shape[0] // SSM_T
    sq = SSM_SQ

    @pl.when(tt == 0)
    def _():
        carry_ref[...] = jnp.zeros_like(carry_ref)
        r = lax.broadcasted_iota(jnp.int32, (2 * LANES, 2 * sq), 0)
        col = lax.broadcasted_iota(jnp.int32, (2 * LANES, 2 * sq), 1)
        own = (((r & (LANES - 1)) >> int(math.log2(SSM_P)))
               == ((col & (sq - 1)) >> int(math.log2(SSM_N))))
        for i in range(SSM_PAIRS):
            for src, dst in ((win_ref, win_bd), (wout_ref, wout_bd)):
                c = src[i]
                full = jnp.concatenate([c[:, :LANES]] * (sq // LANES) + [c[:, LANES:]] * (sq // LANES), axis=1)
                dst[i] = jnp.where(own, full, jnp.zeros_like(full))
        er = lax.broadcasted_iota(jnp.int32, (2 * SSM_P, 2 * LANES), 0)
        ec = lax.broadcasted_iota(jnp.int32, (2 * SSM_P, 2 * LANES), 1)
        lg_p, lg_l = int(math.log2(SSM_P)), int(math.log2(LANES))
        spread = jnp.where(((er >> lg_p) == (ec >> lg_l)) & ((er & (SSM_P - 1)) == (ec & (SSM_P - 1))),
                           1.0, 0.0).astype(BF16)
        r2 = lax.broadcasted_iota(jnp.int32, (2 * LANES, 2 * LANES), 0)
        c2 = lax.broadcasted_iota(jnp.int32, (2 * LANES, 2 * LANES), 1)
        own2 = ((r2 & (LANES - 1)) >> lg_p) == ((c2 & (LANES - 1)) >> lg_p)
        for d in range(SSM_PAIRS):
            toep_bd[d] = jnp.where(own2, _dot(toep_ref[d], spread), 0.0).astype(BF16)

    step = lambda t: u_ref[pl.ds(t, nc, stride=SSM_T), :].astype(BF16)
    a = [jnp.concatenate([step(2 * i), step(2 * i + 1)], axis=1) for i in range(SSM_PAIRS)]
    s = _dot(a[0], win_bd[0])
    for i in range(1, SSM_PAIRS):
        s = s + _dot(a[i], win_bd[i])
    xr, xi = s[:, :sq], s[:, sq:]
    row = lax.broadcasted_iota(jnp.int32, (nc, sq), 0)
    cr, ci = carry_ref[7:8, :sq], carry_ref[7:8, sq:]
    lr, li = lp_ref[0:1, :sq], lp_ref[0:1, sq:]
    xr = xr + jnp.where(row == 0, lr * cr - li * ci, 0.0)
    xi = xi + jnp.where(row == 0, lr * ci + li * cr, 0.0)

    def down(v, d):
        return jnp.where(row >= d, pltpu.roll(v, d, 0), 0.0)

    for k in range(nsteps):
        d = 1 << k
        lr, li = lp_ref[k:k + 1, :sq], lp_ref[k:k + 1, sq:]
        sr, si = down(xr, d), down(xi, d)
        xr, xi = xr + lr * sr - li * si, xi + lr * si + li * sr
    hr = jnp.where(row == 0, cr, pltpu.roll(xr, 1, 0))
    hi = jnp.where(row == 0, ci, pltpu.roll(xi, 1, 0))
    h_in = jnp.concatenate([hr, hi], axis=1).astype(BF16)
    last = jnp.concatenate([xr[nc - 8:], xi[nc - 8:]], axis=1)
    carry_ref[...] = last
    hl_ref[...] = last
    for j in range(SSM_PAIRS):
        acc = lax.dot_general(h_in, wout_bd[j], (((1,), (1,)), ((), ())), preferred_element_type=F32)
        for i in range(j + 1):
            acc = acc + _dot(a[i], toep_bd[j - i])
        y_ref[pl.ds(2 * j, nc, stride=SSM_T), :] = acc[:, :LANES]
        y_ref[pl.ds(2 * j + 1, nc, stride=SSM_T), :] = acc[:, LANES:]


def _ssm_prompt(u, tables):
    toep, w_in, w_out, ldt = tables
    s = u.shape[0]
    tile = min(s, 8192)
    nc = tile // SSM_T
    nsteps = int(math.log2(nc))
    scale = (SSM_T * (2.0 ** jnp.arange(nsteps, dtype=F32)))[:, None, None]
    lp = jnp.exp(ldt[None] * scale)
    lp = jnp.concatenate([lp.real.reshape(nsteps, SSM_Q, SSM_SQ), lp.imag.reshape(nsteps, SSM_Q, SSM_SQ)],
                         axis=2).transpose(1, 0, 2)
    per_q = lambda *shape: pl.BlockSpec((None,) + shape, lambda q, t: (q,) + (0,) * len(shape))
    y, hl = pl.pallas_call(
        functools.partial(_ssm_prompt_kernel, nsteps=nsteps),
        grid=(SSM_Q, s // tile),
        in_specs=[pl.BlockSpec((tile, LANES), lambda q, t: (t, q)),
                  per_q(SSM_PAIRS, 2 * LANES, 2 * SSM_P), per_q(SSM_PAIRS, 2 * LANES, 2 * LANES),
                  per_q(SSM_PAIRS, 2 * LANES, 2 * LANES), per_q(nsteps, 2 * SSM_SQ)],
        out_specs=[pl.BlockSpec((tile, LANES), lambda q, t: (t, q)), per_q(8, 2 * SSM_SQ)],
        out_shape=[jax.ShapeDtypeStruct((s, SSM_WIDTH), F32),
                   jax.ShapeDtypeStruct((SSM_Q, 8, 2 * SSM_SQ), F32)],
        scratch_shapes=[pltpu.VMEM((8, 2 * SSM_SQ), F32),
                        pltpu.VMEM((SSM_PAIRS, 2 * LANES, 2 * SSM_SQ), BF16),
                        pltpu.VMEM((SSM_PAIRS, 2 * LANES, 2 * SSM_SQ), BF16),
                        pltpu.VMEM((SSM_PAIRS, 2 * LANES, 2 * LANES), BF16)],
        compiler_params=_params(2, 56), name="ssm_prompt",
    )(u, toep, w_in, w_out, lp)
    hl = hl[:, 7, :]
    return y, hl[:, :SSM_SQ].reshape(SSM_G, SSM_N), hl[:, SSM_SQ:].reshape(SSM_G, SSM_N)


def _ssm_sample_kernel(u_ref, hr_ref, hi_ref, lr_ref, li_ref, br_ref, bi_ref, cr_ref, ci_ref,
                       y_ref, or_ref, oi_ref):
    u = u_ref[...].astype(BF16)
    hr, hi = hr_ref[...], hi_ref[...]
    lr, li = lr_ref[...], li_ref[...]
    nr = lr * hr - li * hi + _dot(u, br_ref[...])
    ni = lr * hi + li * hr + _dot(u, bi_ref[...])
    or_ref[...] = nr
    oi_ref[...] = ni
    y_ref[...] = _dot(nr.astype(BF16), cr_ref[...]) + _dot(ni.astype(BF16), ci_ref[...])


def _ssm_sample(u, h_re, h_im, ssm_p):
    nb = u.shape[0]
    _, lam_bar, b_bar, c = _ssm_discretise(*ssm_p)
    eye = jnp.eye(SSM_G, dtype=F32)
    gn = SSM_G * SSM_N
    bd = lambda part, eq, shape: jnp.einsum(eq, part, eye).reshape(shape).astype(BF16)
    args = (u, h_re, h_im, lam_bar.real.reshape(1, gn), lam_bar.imag.reshape(1, gn),
            bd(b_bar.real, 'gnp,gh->gphn', (SSM_WIDTH, gn)), bd(b_bar.imag, 'gnp,gh->gphn', (SSM_WIDTH, gn)),
            bd(c.real, 'gpn,gh->gnhp', (gn, SSM_WIDTH)), bd(-c.imag, 'gpn,gh->gnhp', (gn, SSM_WIDTH)))
    return pl.pallas_call(
        _ssm_sample_kernel,
        grid=(1,),
        in_specs=[_const_spec(a.shape) for a in args],
        out_specs=[_const_spec((nb, SSM_WIDTH)), _const_spec((nb, gn)), _const_spec((nb, gn))],
        out_shape=[jax.ShapeDtypeStruct((nb, SSM_WIDTH), F32),
                   jax.ShapeDtypeStruct((nb, gn), F32), jax.ShapeDtypeStruct((nb, gn), F32)],
        compiler_params=_params(1, 32), name="ssm_sample",
    )(*args)


def _mix_kernel(x_ref, o1_ref, o2_ref, o3_ref, l1_ref, l2_ref, l3_ref, y_ref, u_ref, gs_ref, ga_ref,
                d_ref, wglu_ref, bglu_ref, wbs_ref, wba_ref, wout_ref, x1_ref, *scr, dils):
    tm = x_ref.shape[0]

    def natural(ref, dil, buf):
        if dil == 1:
            return ref[0]
        for r in range(dil):
            for t in range(N_TILES):
                buf[t, pl.ds(r, tm // dil, stride=dil), :] = ref[r, :, t * LANES:(t + 1) * LANES].astype(F32)
        return jnp.concatenate([buf[t] for t in range(N_TILES)], axis=1)

    o1, o2, o3 = (natural(r, d, scr[k]) for k, (r, d) in enumerate(zip((o1_ref, o2_ref, o3_ref), dils)))
    l1, l2, l3 = (natural(r, d, scr[3 + k]) for k, (r, d) in enumerate(zip((l1_ref, l2_ref, l3_ref), dils)))
    m = jnp.maximum(jnp.maximum(l1, l2), l3)
    e1, e2, e3 = jnp.exp(l1 - m), jnp.exp(l2 - m), jnp.exp(l3 - m)
    den = e1 + e2 + e3
    attn = (e1 / den) * o1 + (e2 / den) * o2 + (e3 / den) * o3
    y = jax.nn.gelu(y_ref[...] + d_ref[...] * u_ref[...])
    ssm_y = y * _sigmoid(_dot(y.astype(BF16), wglu_ref[...]) + bglu_ref[...])
    z = (_sigmoid(gs_ref[...].astype(F32)) * _dot(ssm_y.astype(BF16), wbs_ref[...])
         + _sigmoid(ga_ref[...].astype(F32)) * _dot(attn.astype(BF16), wba_ref[...]))
    x1_ref[...] = x_ref[...] + _dot(z.astype(BF16), wout_ref[...])


def _mix(x, os_, ls_, y_ssm, u, gs, ga, consts, tm):
    s = x.shape[0]
    row = lambda i: (i, 0)
    res = lambda i: (0, i, 0)
    dils = tuple(o.shape[0] for o in os_)
    grouped = [*os_, *ls_]
    flat = [y_ssm, u, gs, ga]
    return pl.pallas_call(
        functools.partial(_mix_kernel, dils=dils),
        grid=(s // tm,),
        in_specs=([pl.BlockSpec((tm, D_MODEL), row)]
                  + [pl.BlockSpec((a.shape[0], tm // a.shape[0], GROUP_W), res) for a in grouped]
                  + [pl.BlockSpec((tm, a.shape[1]), row) for a in flat]
                  + [_const_spec(c.shape) for c in consts]),
        out_specs=pl.BlockSpec((tm, D_MODEL), row),
        out_shape=jax.ShapeDtypeStruct((s, D_MODEL), F32),
        scratch_shapes=[pltpu.VMEM((N_TILES, tm, LANES), F32)] * 6,
        compiler_params=_params(1, 48), name="mix",
    )(x, *grouped, *flat, *consts)


def _mlp_kernel(*refs, n_phase, jobs):
    x1_ref, nm_ref, wup_ref, wdn_ref, nf_ref = refs[:5]
    n_in = 5 + sum(JOB_ARITY[j.part][0] for j in jobs)
    out_ref = refs[n_in]
    n_out = 1 + sum(JOB_ARITY[j.part][1] for j in jobs)
    up_scr = refs[n_in + n_out]
    _, in_phase = _phase_runner(n_phase)

    job_refs = (refs[5:n_in], refs[n_in + 1:n_in + n_out], refs[n_in + n_out + 1:])

    @pl.when(pl.program_id(0) == 0)
    def _():
        _run_cache_jobs(jobs, *job_refs, init=True)

    def ride(block):
        if block < n_phase:
            _run_cache_jobs(jobs, *job_refs)

    @in_phase(0)
    def _():
        hm = _rms(x1_ref[...], nm_ref[...]).astype(BF16)
        up = jnp.maximum(_dot(hm, wup_ref[...]), 0.0)
        up_scr[...] = (up * up).astype(BF16)
        ride(0)

    @in_phase(1 % n_phase)
    def _():
        x2 = x1_ref[...] + _dot(up_scr[...], wdn_ref[...])
        out_ref[...] = _rms(x2, nf_ref[...])
        ride(1)


def _mlp(x1, consts, tm, n_phase=1, jobs=(), job_args=()):
    s = x1.shape[0]
    ntiles = s // tm
    row = lambda st: (st // n_phase, 0)
    nb = job_args[0][1].shape[0] if jobs else 0
    job_in, job_out, job_shape, job_scr, job_alias = _cache_job_specs(jobs, nb, ntiles * n_phase)
    return pl.pallas_call(
        functools.partial(_mlp_kernel, n_phase=n_phase, jobs=jobs),
        grid=(ntiles * n_phase,),
        in_specs=[pl.BlockSpec((tm, D_MODEL), row)] + [_const_spec(c.shape) for c in consts] + job_in,
        out_specs=[pl.BlockSpec((tm, D_MODEL), row)] + job_out,
        out_shape=[jax.ShapeDtypeStruct((s, D_MODEL), F32)] + job_shape,
        scratch_shapes=[pltpu.VMEM((tm, consts[1].shape[1]), BF16)] + job_scr,
        input_output_aliases={5 + i: 1 + o for i, o in job_alias},
        compiler_params=_params(1, 58), name="mlp",
    )(x1, *consts, *[a for args in job_args for a in args])


def kernel(x_prompt, x_sample, cache_kv_g1, cache_kv_g2, cache_kv_g3, state_ssm, norm_mix, w_in,
           ssm_lambda_re, ssm_lambda_im, ssm_log_dt, ssm_b_re, ssm_b_im, ssm_c_re, ssm_c_im, ssm_d,
           w_glu, b_glu, w_branch_ssm, w_branch_attn, w_out, norm_mlp, w_up, w_down, norm_final):
    depth = w_in.shape[0]
    assert depth == 1, "one decoder layer"
    seq = x_prompt.shape[1]
    nb = x_sample.shape[0]
    assert x_prompt.shape[0] == 1 and x_sample.shape[1] == 1
    caches = (cache_kv_g1, cache_kv_g2, cache_kv_g3)
    windows = tuple(w for w, _ in ATTN_GROUPS)
    dils = tuple(d for _, d in ATTN_GROUPS)
    for c, w in zip(caches, windows):
        assert c.shape[2] == w, "cache holds exactly one window"

    w_in_bf = w_in[0].astype(BF16)
    gain_mix = norm_mix[0][None, :]
    mix_consts = (ssm_d[0][None, :], w_glu[0].astype(BF16), b_glu[0][None, :],
                  w_branch_ssm[0].astype(BF16), w_branch_attn[0].astype(BF16), w_out[0].astype(BF16))
    mlp_consts = (norm_mlp[0][None, :], w_up[0].astype(BF16), w_down[0].astype(BF16), norm_final[None, :])
    ssm_p = (ssm_lambda_re[0], ssm_lambda_im[0], ssm_log_dt[0], ssm_b_re[0], ssm_b_im[0],
             ssm_c_re[0], ssm_c_im[0])

    xs = x_sample[:, 0]
    res = _inproj(xs, gain_mix, w_in_bf, jnp.full((1,), PAST_LEN), jnp.zeros((nb,)), nb,
                  (nb,) * N_GROUPS, (1,) * N_GROUPS)
    qs_s, tails_s = res[0:3], res[9:12]
    u_s, gs_s, ga_s = res[12:15]

    cache_t = [c[0].transpose(0, 2, 3, 4, 1).reshape(nb * 2 * GROUP_W, c.shape[2]) for c in caches]
    q_s = [q[0].astype(F32) for q in qs_s]
    k_s = [t[:, :GROUP_W] for t in tails_s]
    v_s = [t[:, GROUP_W:] for t in tails_s]
    job = lambda g, part: _CacheJob(dils[g], caches[g].shape[2], part)

    xp = x_prompt[0]
    tm_p = 256
    res = _inproj(xp, gain_mix, w_in_bf, jnp.arange(seq // tm_p) * tm_p, jnp.arange(tm_p), tm_p,
                  windows, dils, n_phase=2, jobs=(job(2, "k"),),
                  job_args=((cache_t[2], q_s[2], k_s[2], _columns(k_s[2])),))
    qs, ks, vs, tails = res[0:3], res[3:6], res[6:9], res[9:12]
    u_p, gs_p, ga_p = res[12:15]
    cache3_k, p3, pn3, lse3 = res[15:19]
    os_p, ls_p = zip(*[_attn_prompt(qs[g], ks[g], vs[g]) for g in range(N_GROUPS)])
    y_p, hp_re, hp_im = _ssm_prompt(u_p, _ssm_tables(ssm_p))
    x1_p = _mix(xp, os_p, ls_p, y_p, u_p, gs_p, ga_p, mix_consts, 2 * tm_p)
    kv_args = lambda g: (cache_t[g], q_s[g], k_s[g], v_s[g], _columns(k_s[g]), _columns(v_s[g]))
    res = _mlp(x1_p, mlp_consts, tm_p, n_phase=2, jobs=(job(2, "v"), job(1, "kv"), job(0, "kv")),
               job_args=((cache_t[2], v_s[2], _columns(v_s[2]), p3, pn3, cache3_k), kv_args(1), kv_args(0)))
    y_prompt = res[0][None]
    new_caches = (res[6], res[3], res[1])
    os_s = (res[7], res[4], res[2])
    lses = (res[8], res[5], lse3)
    kv_prompt = [t.reshape(1, 1, w, 2, HEADS, HEAD_DIM) for t, w in zip(tails, windows)]
    ssm_prompt = jnp.stack([hp_re, hp_im], axis=-1)[None, None].astype(state_ssm.dtype)

    kv_sample = [t.reshape(nb, 2, HEADS, HEAD_DIM, t.shape[1]).transpose(0, 4, 1, 2, 3)[None]
                 for t in new_caches]
    os_s = [o[None] for o in os_s]
    ls_s = [jnp.repeat(l[:, :nb].T, HEAD_DIM, axis=1)[None] for l in lses]
    st = state_ssm[0].astype(F32)
    gn = SSM_G * SSM_N
    y_s, hs_re, hs_im = _ssm_sample(u_s, st[..., 0].reshape(nb, gn), st[..., 1].reshape(nb, gn), ssm_p)
    x1_s = _mix(xs, os_s, ls_s, y_s, u_s, gs_s, ga_s, mix_consts, nb)
    y_sample = _mlp(x1_s, mlp_consts, nb)[0][:, None]
    ssm_sample = jnp.stack([hs_re.reshape(nb, SSM_G, SSM_N), hs_im.reshape(nb, SSM_G, SSM_N)],
                           axis=-1)[None].astype(state_ssm.dtype)

    return (y_prompt, y_sample, kv_prompt[0], kv_prompt[1], kv_prompt[2], ssm_prompt,
            kv_sample[0], kv_sample[1], kv_sample[2], ssm_sample)
```
